```python
import jax
import jax.numpy as jnp
from jax import lax
import numpy as np

D_MODEL = 1024
BATCH = 2
SEQ = 8192
DEPTH = 2

GRID_W = 64
CTX_LEN = 256
MLA_HEADS = 8
MLA_Q_RANK = 384
MLA_KV_RANK = 256
MLA_NOPE = 64
MLA_ROPE = 32
MLA_V = 64
MLA_W = MLA_HEADS * MLA_V
NA_HEADS = 4
NA_HEAD_DIM = 64
NA_W = NA_HEADS * NA_HEAD_DIM
NA_WIN_R = 8
NA_WIN_C = 16
POOL_WINDOWS = (2, 4, 8, 16)
POOL_GROUP = 64
POOL_WIDTH = POOL_GROUP * len(POOL_WINDOWS)
N_EXPERTS = 16
EC_CAPACITY = 2
EXPERT_FF = 2048
N_BRANCH = 3
Q_BLOCK = 128
ROPE_BASE = 10000.0
EPS = 1e-6
IN_SIZES = (MLA_Q_RANK, MLA_KV_RANK, MLA_ROPE, 3 * NA_W, POOL_WIDTH, N_BRANCH * D_MODEL)
D_IN = MLA_Q_RANK + MLA_KV_RANK + MLA_ROPE + 3 * NA_W + POOL_WIDTH + N_BRANCH * D_MODEL

kernel_name = 'hybrid_mla_natten_pool_ecmoe_dit'


def rmsnorm(x, g):
    xf = x.astype(jnp.float32)
    y = xf * lax.rsqrt(jnp.mean(xf * xf, axis=-1, keepdims=True) + EPS)
    return (y * g.astype(jnp.float32)).astype(x.dtype)


def modulate(x, g, shift, scale):
    return rmsnorm(x, g) * (1 + scale) + shift


def split_in(z):
    idx = np.cumsum(IN_SIZES)[:-1].tolist()
    return jnp.split(z, idx, axis=-1)


def axial_rope(x, pos_r, pos_c):
    half = MLA_ROPE // 2
    quarter = half // 2
    inv = ROPE_BASE ** (-jnp.arange(quarter, dtype=jnp.float32) / quarter)

    def rot(xa, pos):
        ang = pos.astype(jnp.float32)[:, None] * inv
        shape = (1, ang.shape[0]) + (1,) * (xa.ndim - 3) + (quarter,)
        cos = jnp.cos(ang).reshape(shape).astype(xa.dtype)
        sin = jnp.sin(ang).reshape(shape).astype(xa.dtype)
        x1, x2 = xa[..., :quarter], xa[..., quarter:]
        return jnp.concatenate([x1 * cos - x2 * sin, x2 * cos + x1 * sin], axis=-1)

    return jnp.concatenate([rot(x[..., :half], pos_r), rot(x[..., half:], pos_c)], axis=-1)


def attend(q, k, v, scale):
    s = jnp.einsum('bqhd,bkhd->bhqk', q, k, preferred_element_type=jnp.float32) * scale
    p = jax.nn.softmax(s, axis=-1).astype(v.dtype)
    return jnp.einsum('bhqk,bkhd->bqhd', p, v)


def blocked_attend(q, k, v, scale):
    B, N, H, dq = q.shape
    qb = q.reshape(B, N // Q_BLOCK, Q_BLOCK, H, dq).transpose(1, 0, 2, 3, 4)
    ob = lax.map(lambda qi: attend(qi, k, v, scale), qb)
    return ob.transpose(1, 0, 2, 3, 4).reshape(B, N, H, v.shape[-1])


def mla_project(cq, ckv, g_q, g_kv, w_uq, w_ukv):
    B, N, _ = cq.shape
    q = (rmsnorm(cq, g_q) @ w_uq).reshape(B, N, MLA_HEADS, MLA_NOPE + MLA_ROPE)
    kv = (rmsnorm(ckv, g_kv) @ w_ukv).reshape(B, N, MLA_HEADS, MLA_NOPE + MLA_V)
    return q[..., :MLA_NOPE], q[..., MLA_NOPE:], kv[..., :MLA_NOPE], kv[..., MLA_NOPE:]


def mla_keys(k_nope, k_rope):
    kr = jnp.broadcast_to(k_rope[:, :, None, :], k_nope.shape[:-1] + (MLA_ROPE,))
    return jnp.concatenate([k_nope, kr], axis=-1)


def na_heads(z):
    B, N, _ = z.shape
    qkv = z.reshape(B, N, 3, NA_HEADS, NA_HEAD_DIM)
    return qkv[:, :, 0], qkv[:, :, 1], qkv[:, :, 2]


def na_latent(q, k, v, k_ctx, v_ctx, rpb):
    B, N, H, d = q.shape
    rows = N // GRID_W
    wr = min(NA_WIN_R, rows)
    r = jnp.arange(rows)
    start_r = jnp.clip(r - wr // 2, 0, rows - wr)
    row_idx = start_r[:, None] + jnp.arange(wr)[None, :]
    cq = jnp.arange(GRID_W)
    start_c = jnp.clip(cq - NA_WIN_C // 2, 0, GRID_W - NA_WIN_C)
    ck = jnp.arange(GRID_W)
    col_mask = (ck[None, :] >= start_c[:, None]) & (ck[None, :] < start_c[:, None] + NA_WIN_C)
    dr = row_idx - r[:, None] + (NA_WIN_R - 1)
    dc = jnp.clip(ck[None, :] - cq[:, None], -(NA_WIN_C - 1), NA_WIN_C - 1) + (NA_WIN_C - 1)
    bias = rpb[:, dr[:, None, :, None], dc[None, :, None, :]].astype(jnp.float32)
    qg = q.reshape(B, rows, GRID_W, H, d)
    kr = k.reshape(B, rows, GRID_W, H, d)[:, row_idx]
    vr = v.reshape(B, rows, GRID_W, H, d)[:, row_idx]
    scale = d ** -0.5
    s_lat = jnp.einsum('brqhd,brwkhd->bhrqwk', qg, kr, preferred_element_type=jnp.float32) * scale + bias[None]
    s_lat = jnp.where(col_mask[:, None, :], s_lat, -jnp.inf)
    s_ctx = jnp.einsum('brqhd,bchd->bhrqc', qg, k_ctx, preferred_element_type=jnp.float32) * scale
    n_lat = wr * GRID_W
    s = jnp.concatenate([s_lat.reshape(B, H, rows, GRID_W, n_lat), s_ctx], axis=-1)
    p = jax.nn.softmax(s, axis=-1).astype(v.dtype)
    p_lat = p[..., :n_lat].reshape(B, H, rows, GRID_W, wr, GRID_W)
    o = jnp.einsum('bhrqwk,brwkhd->brqhd', p_lat, vr) + jnp.einsum('bhrqc,bchd->brqhd', p[..., n_lat:], v_ctx)
    return o.reshape(B, N, H * d)


def pool_mix(u, w_pool, pool_scale):
    B, N, _ = u.shape
    uf = u.astype(jnp.float32)
    cs = jnp.concatenate([jnp.zeros((B, 1, POOL_WIDTH), jnp.float32), lax.cumsum(uf, axis=1)], axis=1)
    t = jnp.arange(N)
    outs = []
    for gi, w in enumerate(POOL_WINDOWS):
        lo = jnp.clip(t - w // 2, 0, N)
        hi = jnp.clip(t + w // 2, 0, N)
        csg = cs[..., gi * POOL_GROUP:(gi + 1) * POOL_GROUP]
        mean = (csg[:, hi] - csg[:, lo]) / (hi - lo).astype(jnp.float32)[None, :, None]
        outs.append(mean - uf[..., gi * POOL_GROUP:(gi + 1) * POOL_GROUP])
    dlt = jnp.stack(outs, axis=2).astype(u.dtype)
    y = jnp.einsum('bngc,gce->bnge', dlt, w_pool).reshape(B, N, POOL_WIDTH)
    return y * pool_scale


def merge(z_gate, a, b, p, w_br_mla, w_br_na, w_br_pool, w_out):
    B, N, _ = a.shape
    g = jax.nn.sigmoid(z_gate.astype(jnp.float32)).astype(a.dtype).reshape(B, N, N_BRANCH, D_MODEL)
    m = g[:, :, 0] * (a @ w_br_mla) + g[:, :, 1] * (b @ w_br_na) + g[:, :, 2] * (p @ w_br_pool)
    return m @ w_out


def ec_moe(h, w_router, w_gate, w_up, w_down):
    B, N, _ = h.shape
    cap = EC_CAPACITY * N // N_EXPERTS
    aff = jax.nn.softmax(jnp.einsum('bnd,de->bne', h, w_router, preferred_element_type=jnp.float32), axis=-1)
    gval, idx = lax.top_k(aff.transpose(0, 2, 1), cap)
    bidx = jnp.arange(B)[:, None, None]
    xs = h[bidx, idx]
    a = jnp.einsum('becd,edf->becf', xs, w_gate)
    u = jnp.einsum('becd,edf->becf', xs, w_up)
    y = jnp.einsum('becf,efd->becd', jax.nn.silu(a) * u, w_down)
    y = y * gval[..., None].astype(y.dtype)
    return jnp.zeros_like(h).at[bidx, idx].add(y)


def setup_inputs(seed: int = 0) -> dict:
    key = jax.random.key(seed)
    ks = jax.random.split(key, 26)
    f32 = jnp.float32

    def nrm(k, shape, scale):
        return jax.random.normal(k, shape, f32) * scale

    L, D, E, F = DEPTH, D_MODEL, N_EXPERTS, EXPERT_FF
    return {
        'x': nrm(ks[0], (BATCH, SEQ, D), 1.0),
        'c': nrm(ks[1], (BATCH, D), 1.0),
        'ctx': nrm(ks[2], (BATCH, CTX_LEN, D), 1.0),
        'c_ctx': nrm(ks[3], (D,), 1.0),
        'norm1_g': 1.0 + nrm(ks[4], (L, D), 0.02),
        'norm2_g': 1.0 + nrm(ks[5], (L, D), 0.02),
        'w_ada': nrm(ks[6], (L, D, 6 * D), 0.5 * D ** -0.5),
        'b_ada': nrm(ks[7], (L, 6 * D), 0.02),
        'w_in': nrm(ks[8], (L, D, D_IN), D ** -0.5),
        'mla_q_norm': 1.0 + nrm(ks[9], (L, MLA_Q_RANK), 0.02),
        'mla_kv_norm': 1.0 + nrm(ks[10], (L, MLA_KV_RANK), 0.02),
        'w_uq': nrm(ks[11], (L, MLA_Q_RANK, MLA_HEADS * (MLA_NOPE + MLA_ROPE)), MLA_Q_RANK ** -0.5),
        'w_ukv': nrm(ks[12], (L, MLA_KV_RANK, MLA_HEADS * (MLA_NOPE + MLA_V)), MLA_KV_RANK ** -0.5),
        'na_rpb': nrm(ks[13], (L, NA_HEADS, 2 * NA_WIN_R - 1, 2 * NA_WIN_C - 1), 0.1),
        'w_pool': nrm(ks[14], (L, len(POOL_WINDOWS), POOL_GROUP, POOL_GROUP), POOL_GROUP ** -0.5),
        'pool_scale': 1.0 + nrm(ks[15], (L, POOL_WIDTH), 0.1),
        'w_br_mla': nrm(ks[16], (L, MLA_W, D), MLA_W ** -0.5),
        'w_br_na': nrm(ks[17], (L, NA_W, D), NA_W ** -0.5),
        'w_br_pool': nrm(ks[18], (L, POOL_WIDTH, D), POOL_WIDTH ** -0.5),
        'w_out': nrm(ks[19], (L, D, D), D ** -0.5),
        'w_router': nrm(ks[20], (L, D, E), D ** -0.5),
        'w_gate': nrm(ks[21], (L, E, D, F), D ** -0.5),
        'w_up': nrm(ks[22], (L, E, D, F), D ** -0.5),
        'w_down': nrm(ks[23], (L, E, F, D), F ** -0.5),
        'final_g': 1.0 + nrm(ks[24], (D,), 0.02),
    }


def reference(x, c, ctx, c_ctx, norm1_g, norm2_g, w_ada, b_ada, w_in, mla_q_norm, mla_kv_norm, w_uq, w_ukv,
              na_rpb, w_pool, pool_scale, w_br_mla, w_br_na, w_br_pool, w_out, w_router, w_gate, w_up, w_down,
              final_g):
    B, N, _ = x.shape
    t = jnp.arange(N)
    pos_r = t // GRID_W
    pos_c = t % GRID_W
    mla_scale = (MLA_NOPE + MLA_ROPE) ** -0.5
    na_scale = NA_HEAD_DIM ** -0.5
    xl, xc = x, ctx
    for l in range(DEPTH):
        last = l == DEPTH - 1
        mod_l = (jax.nn.silu(c) @ w_ada[l] + b_ada[l])[:, None, :]
        mod_c = (jax.nn.silu(c_ctx) @ w_ada[l] + b_ada[l])[None, None, :]
        sh1l, sc1l, g1l, sh2l, sc2l, g2l = jnp.split(mod_l, 6, axis=-1)
        sh1c, sc1c, g1c, sh2c, sc2c, g2c = jnp.split(mod_c, 6, axis=-1)

        hl = modulate(xl, norm1_g[l], sh1l, sc1l)
        hc = modulate(xc, norm1_g[l], sh1c, sc1c)
        cq_l, ckv_l, kr_l, na_l, pool_l, gate_l = split_in(hl @ w_in[l])
        cq_c, ckv_c, kr_c, na_c, pool_c, gate_c = split_in(hc @ w_in[l])

        qn_l, qr_l, kn_l, v_l = mla_project(cq_l, ckv_l, mla_q_norm[l], mla_kv_norm[l], w_uq[l], w_ukv[l])
        qn_c, qr_c, kn_c, v_c = mla_project(cq_c, ckv_c, mla_q_norm[l], mla_kv_norm[l], w_uq[l], w_ukv[l])
        q_l = jnp.concatenate([qn_l, axial_rope(qr_l, pos_r, pos_c)], axis=-1)
        k_l = mla_keys(kn_l, axial_rope(kr_l, pos_r, pos_c))
        k_c = mla_keys(kn_c, kr_c)
        a_l = blocked_attend(q_l, jnp.concatenate([k_c, k_l], axis=1), jnp.concatenate([v_c, v_l], axis=1),
                             mla_scale).reshape(B, N, MLA_W)

        nq_l, nk_l, nv_l = na_heads(na_l)
        nq_c, nk_c, nv_c = na_heads(na_c)
        b_l = na_latent(nq_l, nk_l, nv_l, nk_c, nv_c, na_rpb[l])

        p_l = pool_mix(pool_l, w_pool[l], pool_scale[l])

        xl = xl + g1l * merge(gate_l, a_l, b_l, p_l, w_br_mla[l], w_br_na[l], w_br_pool[l], w_out[l])
        if not last:
            q_c = jnp.concatenate([qn_c, qr_c], axis=-1)
            a_c = attend(q_c, k_c, v_c, mla_scale).reshape(B, -1, MLA_W)
            b_c = attend(nq_c, nk_c, nv_c, na_scale).reshape(B, -1, NA_W)
            p_c = pool_mix(pool_c, w_pool[l], pool_scale[l])
            xc = xc + g1c * merge(gate_c, a_c, b_c, p_c, w_br_mla[l], w_br_na[l], w_br_pool[l], w_out[l])

        xl = xl + g2l * ec_moe(modulate(xl, norm2_g[l], sh2l, sc2l), w_router[l], w_gate[l], w_up[l], w_down[l])
        if not last:
            xc = xc + g2c * ec_moe(modulate(xc, norm2_g[l], sh2c, sc2c), w_router[l], w_gate[l], w_up[l], w_down[l])
    return rmsnorm(xl, final_g)
```

```python
import functools

import jax
import jax.numpy as jnp
from jax import lax
from jax.experimental import pallas as pl
from jax.experimental.pallas import tpu as pltpu

F32 = jnp.float32
BF16 = jnp.bfloat16
I32 = jnp.int32

GRID_W = 64
MLA_HEADS = 8
MLA_Q_RANK = 384
MLA_KV_RANK = 256
MLA_NOPE = 64
MLA_ROPE = 32
MLA_V = 64
MLA_W = MLA_HEADS * MLA_V
NA_HEADS = 4
NA_HEAD_DIM = 64
NA_W = NA_HEADS * NA_HEAD_DIM
NA_WIN_R = 8
NA_WIN_C = 16
POOL_WINDOWS = (2, 4, 8, 16)
POOL_GROUP = 64
POOL_WIDTH = POOL_GROUP * len(POOL_WINDOWS)
N_EXPERTS = 16
EC_CAPACITY = 2
N_BRANCH = 3
ROPE_BASE = 10000.0
EPS = 1e-6
IN_SIZES = (MLA_Q_RANK, MLA_KV_RANK, MLA_ROPE, 3 * NA_W, POOL_WIDTH)

LANES = 128
SUBLANES = 8
HEAD_PAD = 128
ROW_TILE = 256
POOL_HALO = 8
MASK_VALUE = -1e30
VMEM_LIMIT = 56 * 1024 * 1024


def _cparams(sem, vmem=VMEM_LIMIT):
    return pltpu.CompilerParams(dimension_semantics=sem, vmem_limit_bytes=vmem)


def _rms(x, g):
    return x * lax.rsqrt(jnp.mean(x * x, axis=-1, keepdims=True) + EPS) * g


def _dot(a, b):
    return jnp.dot(a, b, preferred_element_type=F32)


def _dot_nt(a, b):
    return lax.dot_general(a, b, (((1,), (1,)), ((), ())), preferred_element_type=F32)


def _ada_kernel(c_ref, w_ref, b_ref, o_ref):
    c = c_ref[...]
    s = c * jax.nn.sigmoid(c)
    o_ref[0] = _dot(s.astype(BF16), w_ref[0].astype(BF16)) + b_ref[0]


def _ada(cc, w_ada, b_ada):
    L, D, D6 = w_ada.shape
    tn = 1536
    return pl.pallas_call(
        _ada_kernel,
        grid=(L, D6 // tn),
        in_specs=[
            pl.BlockSpec((8, D), lambda l, j: (0, 0)),
            pl.BlockSpec((1, D, tn), lambda l, j: (l, 0, j)),
            pl.BlockSpec((1, 1, tn), lambda l, j: (l, 0, j)),
        ],
        out_specs=pl.BlockSpec((1, 8, tn), lambda l, j: (l, 0, j)),
        out_shape=jax.ShapeDtypeStruct((L, 8, D6), F32),
        compiler_params=_cparams(("parallel", "parallel")),
        name="ada",
    )(cc, w_ada, b_ada.reshape(L, 1, D6))


def _inproj_kernel(x_ref, sh_ref, sc_ref, g1_ref, wb_ref, gq_ref, gkv_ref, wqm_ref, wqs_ref, wk_ref, wv_ref,
                   cos_ref, sin_ref, q_ref, k_ref, v_ref, na_ref, pool_ref):
    x = x_ref[...]
    h = _rms(x, g1_ref[...]) * (1.0 + sc_ref[0]) + sh_ref[0]
    z = _dot(h.astype(BF16), wb_ref[...])
    o = 0
    cq = z[:, o:o + MLA_Q_RANK]; o += MLA_Q_RANK
    ckv = z[:, o:o + MLA_KV_RANK]; o += MLA_KV_RANK
    kr = z[:, o:o + HEAD_PAD]; o += HEAD_PAD
    krs = z[:, o:o + HEAD_PAD]; o += HEAD_PAD
    na_ref[...] = z[:, o:o + 3 * NA_W].astype(BF16); o += 3 * NA_W
    pool_ref[...] = z[:, o:o + POOL_WIDTH]
    cqn = _rms(cq, gq_ref[...]).astype(BF16)
    ckvn = _rms(ckv, gkv_ref[...]).astype(BF16)
    cos = cos_ref[...]
    sin = sin_ref[...]
    cos8 = jnp.concatenate([cos] * MLA_HEADS, axis=1)
    sin8 = jnp.concatenate([sin] * MLA_HEADS, axis=1)
    q = _dot(cqn, wqm_ref[...]) * cos8 + _dot(cqn, wqs_ref[...]) * sin8
    q_ref[...] = q.astype(BF16)
    krot = kr * cos + krs * sin
    k = _dot(ckvn, wk_ref[...]) + jnp.concatenate([krot] * MLA_HEADS, axis=1)
    k_ref[...] = k.astype(BF16)
    v_ref[...] = _dot(ckvn, wv_ref[...]).astype(BF16)


def _inproj(x_all, mod, g1, wts, cos_t, sin_t, *, n_lat_tiles, tiles_per_batch, n_batch):
    R, D = x_all.shape
    T = ROW_TILE
    nb = wts["wb"].shape[1]

    def seg(i):
        return jnp.minimum(i // tiles_per_batch, n_batch)

    def tab(i):
        return jnp.where(i < n_lat_tiles, i % tiles_per_batch, tiles_per_batch)

    full = lambda a: pl.BlockSpec(a.shape, lambda i: (0,) * a.ndim)
    outs = [
        jax.ShapeDtypeStruct((R, MLA_HEADS * HEAD_PAD), BF16),
        jax.ShapeDtypeStruct((R, MLA_HEADS * HEAD_PAD), BF16),
        jax.ShapeDtypeStruct((R, MLA_W), BF16),
        jax.ShapeDtypeStruct((R, 3 * NA_W), BF16),
        jax.ShapeDtypeStruct((R, POOL_WIDTH), F32),
    ]
    return pl.pallas_call(
        _inproj_kernel,
        grid=(R // T,),
        in_specs=[
            pl.BlockSpec((T, D), lambda i: (i, 0)),
            pl.BlockSpec((1, 1, D), lambda i: (seg(i), 0, 0)),
            pl.BlockSpec((1, 1, D), lambda i: (seg(i), 0, 1)),
            full(g1), full(wts["wb"]), full(wts["gq"]), full(wts["gkv"]), full(wts["wqm"]), full(wts["wqs"]),
            full(wts["wk"]), full(wts["wv"]),
            pl.BlockSpec((T, HEAD_PAD), lambda i: (tab(i), 0)),
            pl.BlockSpec((T, HEAD_PAD), lambda i: (tab(i), 0)),
        ],
        out_specs=[pl.BlockSpec((T, s.shape[1]), lambda i: (i, 0)) for s in outs],
        out_shape=outs,
        compiler_params=_cparams(("parallel",)),
        name="inproj",
    )(x_all, mod, mod, g1, wts["wb"], wts["gq"], wts["gkv"], wts["wqm"], wts["wqs"], wts["wk"], wts["wv"],
      cos_t, sin_t)


def _flash_kernel(*refs, tk, n_chunks, scale):
    if n_chunks:
        q_ref, kc_ref, vc_ref, kl_ref, vl_ref, o_ref, m_ref, l_ref, acc_ref = refs
    else:
        q_ref, kc_ref, vc_ref, o_ref, m_ref, l_ref, acc_ref = refs
    m_ref[...] = jnp.full(m_ref.shape, -jnp.inf, F32)
    l_ref[...] = jnp.zeros(l_ref.shape, F32)
    acc_ref[...] = jnp.zeros(acc_ref.shape, F32)

    def step(k2, v2):
        for h in range(2):
            qh = q_ref[:, h * HEAD_PAD:(h + 1) * HEAD_PAD]
            kh = k2[:, h * HEAD_PAD:(h + 1) * HEAD_PAD]
            s = _dot_nt(qh, kh) * scale
            m_old = m_ref[h]
            m_new = jnp.maximum(m_old, jnp.max(s, axis=1, keepdims=True))
            alpha = jnp.exp(m_old - m_new)
            p = jnp.exp(s - m_new)
            l_ref[h] = alpha * l_ref[h] + jnp.sum(p, axis=1, keepdims=True)
            acc_ref[h] = alpha * acc_ref[h] + _dot(p.astype(BF16), v2)
            m_ref[h] = m_new

    step(kc_ref[...], vc_ref[...])
    if n_chunks:
        def body(c, carry):
            r0 = pl.multiple_of(c * tk, tk)
            step(kl_ref[pl.ds(r0, tk), :], vl_ref[pl.ds(r0, tk), :])
            return carry
        lax.fori_loop(0, n_chunks, body, 0)
    lane = lax.broadcasted_iota(I32, (1, LANES), 1)
    o0 = acc_ref[0] / l_ref[0]
    o1 = acc_ref[1] / l_ref[1]
    o_ref[...] = jnp.where(lane < MLA_V, o0, o1).astype(BF16)


def _flash(q_all, k_all, v_all, *, n_batch, n_lat, n_ctx, latent):
    scale = (MLA_NOPE + MLA_ROPE) ** -0.5
    ctx_blk0 = (n_batch * n_lat) // n_ctx
    n_pairs = MLA_HEADS // 2
    if latent:
        tq = min(512, n_lat)
        tk = min(512, n_lat)
        nq = n_lat // tq
        q_map = lambda b, p, i: (b * nq + i, p)
        n_rows = n_batch * n_lat
        n_chunks = n_lat // tk
    else:
        tq, nq, tk, n_chunks = n_ctx, 1, n_ctx, 0
        q_map = lambda b, p, i: (ctx_blk0 + b, p)
        n_rows = n_batch * n_ctx
    in_specs = [
        pl.BlockSpec((tq, 2 * HEAD_PAD), q_map),
        pl.BlockSpec((n_ctx, 2 * HEAD_PAD), lambda b, p, i: (ctx_blk0 + b, p)),
        pl.BlockSpec((n_ctx, 2 * MLA_V), lambda b, p, i: (ctx_blk0 + b, p)),
    ]
    args = [q_all, k_all, v_all]
    if latent:
        in_specs += [
            pl.BlockSpec((n_lat, 2 * HEAD_PAD), lambda b, p, i: (b, p)),
            pl.BlockSpec((n_lat, 2 * MLA_V), lambda b, p, i: (b, p)),
        ]
        args += [k_all, v_all]
    return pl.pallas_call(
        functools.partial(_flash_kernel, tk=tk, n_chunks=n_chunks, scale=scale),
        grid=(n_batch, n_pairs, nq),
        in_specs=in_specs,
        out_specs=pl.BlockSpec((tq, 2 * MLA_V), lambda b, p, i: (b * nq + i, p)),
        out_shape=jax.ShapeDtypeStruct((n_rows, MLA_W), BF16),
        scratch_shapes=[
            pltpu.VMEM((2, tq, 1), F32),
            pltpu.VMEM((2, tq, 1), F32),
            pltpu.VMEM((2, tq, 2 * MLA_V), F32),
        ],
        compiler_params=_cparams(("parallel", "parallel", "arbitrary")),
        name="flash_lat" if latent else "flash_ctx",
    )(*args)


NA_QROWS = 8
NA_KROWS = 16


def _na_softmax_pv(parts):
    m = None
    for s, _ in parts:
        mi = jnp.max(s, axis=1, keepdims=True)
        m = mi if m is None else jnp.maximum(m, mi)
    l = None
    o = None
    for s, v in parts:
        p = jnp.exp(s - m)
        li = jnp.sum(p, axis=1, keepdims=True)
        oi = _dot(p.astype(BF16), v)
        l = li if l is None else l + li
        o = oi if o is None else o + oi
    return o / l


def _na_lat_kernel(q_ref, k_ref, v_ref, kc_ref, vc_ref, bias_ref, o_ref, *, rows, scale):
    i = pl.program_id(1)
    s0 = jnp.clip(NA_QROWS * i - NA_WIN_R // 2, 0, rows - NA_KROWS) * GRID_W
    s0 = pl.multiple_of(s0, 256)
    nk = NA_KROWS * GRID_W
    lo = lax.broadcasted_iota(I32, (1, LANES), 1) < NA_HEAD_DIM
    outs = []
    for pr in range(NA_HEADS // 2):
        cs = slice(pr * LANES, (pr + 1) * LANES)
        qp = q_ref[:, cs]
        kw = k_ref[pl.ds(s0, nk), cs]
        vw = v_ref[pl.ds(s0, nk), cs]
        kc = kc_ref[:, cs]
        vc = vc_ref[:, cs]
        o_pair = None
        for hh in range(2):
            msk = lo if hh == 0 else jnp.logical_not(lo)
            qh = jnp.where(msk, qp, jnp.zeros_like(qp))
            s_lat = _dot_nt(qh, kw) * scale + bias_ref[0, 2 * pr + hh]
            s_ctx = _dot_nt(qh, kc) * scale
            o = _na_softmax_pv([(s_lat, vw), (s_ctx, vc)])
            o_pair = o if hh == 0 else jnp.where(lo, o_pair, o)
        outs.append(o_pair)
    o_ref[...] = jnp.concatenate(outs, axis=1).astype(BF16)


def _na_lat(na_all, bias, *, n_batch, n_lat, n_ctx):
    rows = n_lat // GRID_W
    nblk = rows // NA_QROWS
    tq = NA_QROWS * GRID_W
    ctx_blk0 = (n_batch * n_lat) // n_ctx
    scale = NA_HEAD_DIM ** -0.5

    def variant(i):
        return jnp.where(i == 0, 0, jnp.where(i == nblk - 1, 2, 1))

    return pl.pallas_call(
        functools.partial(_na_lat_kernel, rows=rows, scale=scale),
        grid=(n_batch, nblk),
        in_specs=[
            pl.BlockSpec((tq, NA_W), lambda b, i: (b * nblk + i, 0)),
            pl.BlockSpec((n_lat, NA_W), lambda b, i: (b, 1)),
            pl.BlockSpec((n_lat, NA_W), lambda b, i: (b, 2)),
            pl.BlockSpec((n_ctx, NA_W), lambda b, i: (ctx_blk0 + b, 1)),
            pl.BlockSpec((n_ctx, NA_W), lambda b, i: (ctx_blk0 + b, 2)),
            pl.BlockSpec((1, NA_HEADS, tq, NA_KROWS * GRID_W), lambda b, i: (variant(i), 0, 0, 0)),
        ],
        out_specs=pl.BlockSpec((tq, NA_W), lambda b, i: (b * nblk + i, 0)),
        out_shape=jax.ShapeDtypeStruct((n_batch * n_lat, NA_W), BF16),
        compiler_params=_cparams(("parallel", "arbitrary")),
        name="na_lat",
    )(na_all, na_all, na_all, na_all, na_all, bias)


def _na_ctx_kernel(q_ref, k_ref, v_ref, o_ref, *, scale):
    lo = lax.broadcasted_iota(I32, (1, LANES), 1) < NA_HEAD_DIM
    outs = []
    for pr in range(NA_HEADS // 2):
        cs = slice(pr * LANES, (pr + 1) * LANES)
        qp = q_ref[:, cs]
        kp = k_ref[:, cs]
        vp = v_ref[:, cs]
        o_pair = None
        for hh in range(2):
            msk = lo if hh == 0 else jnp.logical_not(lo)
            qh = jnp.where(msk, qp, jnp.zeros_like(qp))
            o = _na_softmax_pv([(_dot_nt(qh, kp) * scale, vp)])
            o_pair = o if hh == 0 else jnp.where(lo, o_pair, o)
        outs.append(o_pair)
    o_ref[...] = jnp.concatenate(outs, axis=1).astype(BF16)


def _na_ctx(na_all, *, n_batch, n_lat, n_ctx):
    ctx_blk0 = (n_batch * n_lat) // n_ctx
    return pl.pallas_call(
        functools.partial(_na_ctx_kernel, scale=NA_HEAD_DIM ** -0.5),
        grid=(n_batch,),
        in_specs=[pl.BlockSpec((n_ctx, NA_W), lambda b, j=j: (ctx_blk0 + b, j)) for j in range(3)],
        out_specs=pl.BlockSpec((n_ctx, NA_W), lambda b: (b, 0)),
        out_shape=jax.ShapeDtypeStruct((n_batch * n_ctx, NA_W), BF16),
        compiler_params=_cparams(("parallel",)),
        name="na_ctx",
    )(na_all, na_all, na_all)


def _na_bias(rpb, rows):
    nblk = rows // NA_QROWS
    qc = jnp.arange(GRID_W)
    kc = jnp.arange(GRID_W)
    start_c = jnp.clip(qc - NA_WIN_C // 2, 0, GRID_W - NA_WIN_C)
    valid_c = (kc[None, :] >= start_c[:, None]) & (kc[None, :] < start_c[:, None] + NA_WIN_C)
    dc = jnp.clip(kc[None, :] - qc[:, None], -(NA_WIN_C - 1), NA_WIN_C - 1) + (NA_WIN_C - 1)
    tables = []
    for i in (0, min(1, nblk - 1), nblk - 1):
        s0 = min(max(NA_QROWS * i - NA_WIN_R // 2, 0), rows - NA_KROWS)
        qrow = NA_QROWS * i + jnp.arange(NA_QROWS)
        krow = s0 + jnp.arange(NA_KROWS)
        start_r = jnp.clip(qrow - NA_WIN_R // 2, 0, rows - NA_WIN_R)
        valid_r = (krow[None, :] >= start_r[:, None]) & (krow[None, :] < start_r[:, None] + NA_WIN_R)
        dr = jnp.clip(krow[None, :] - qrow[:, None] + (NA_WIN_R - 1), 0, 2 * NA_WIN_R - 2)
        b = rpb[:, dr[:, None, :, None], dc[None, :, None, :]].astype(F32)
        valid = valid_r[:, None, :, None] & valid_c[None, :, None, :]
        b = jnp.where(valid[None], b, MASK_VALUE)
        tables.append(b.reshape(NA_HEADS, NA_QROWS * GRID_W, NA_KROWS * GRID_W))
    return jnp.stack(tables)


def _pool_kernel(prev_ref, cur_ref, next_ref, w_ref, ps_ref, o_ref, scr, *, tiles_per_batch, n_lat_tiles, n_lat, n_ctx):
    i = pl.program_id(0)
    T = ROW_TILE
    H = POOL_HALO
    is_lat = i < n_lat_tiles
    tile_in_seq = jnp.where(is_lat, i % tiles_per_batch, 0)
    n_seq = jnp.where(is_lat, n_lat, n_ctx)
    first = tile_in_seq == 0
    last = jnp.where(is_lat, tile_in_seq == tiles_per_batch - 1, True)
    u = cur_ref[...]
    scr[0:H, :] = jnp.where(first, 0.0, prev_ref[...])
    scr[H:H + T, :] = u
    scr[H + T:H + T + H, :] = jnp.where(last, 0.0, next_ref[...])

    def win(lo, hi):
        acc = None
        for j in range(lo, hi):
            piece = scr[H + j:H + j + T, :]
            acc = piece if acc is None else acc + piece
        return acc

    t = (tile_in_seq * T + lax.broadcasted_iota(I32, (T, 1), 0))
    lane = lax.broadcasted_iota(I32, (1, POOL_WIDTH), 1)
    sums = {}
    acc = win(-1, 1)
    sums[2] = acc
    acc = acc + win(-2, -1) + win(1, 2)
    sums[4] = acc
    acc = acc + win(-4, -2) + win(2, 4)
    sums[8] = acc
    acc = acc + win(-8, -4) + win(4, 8)
    sums[16] = acc
    mean = None
    for gi, w in enumerate(POOL_WINDOWS):
        cnt = (jnp.minimum(t + w // 2, n_seq) - jnp.maximum(t - w // 2, 0)).astype(F32)
        mg = sums[w] / cnt
        mean = mg if mean is None else jnp.where(lane < gi * POOL_GROUP, mean, mg)
    dlt = mean - u
    o_ref[...] = (_dot(dlt.astype(BF16), w_ref[...]) * ps_ref[...]).astype(BF16)


def _pool(u_all, w_bd, pool_scale, *, n_batch, n_lat, n_ctx):
    R = u_all.shape[0]
    T = ROW_TILE
    H = POOL_HALO
    assert n_ctx == T
    tiles_per_batch = n_lat // T
    n_lat_tiles = n_batch * tiles_per_batch
    nb8 = R // H
    return pl.pallas_call(
        functools.partial(_pool_kernel, tiles_per_batch=tiles_per_batch, n_lat_tiles=n_lat_tiles, n_lat=n_lat,
                          n_ctx=n_ctx),
        grid=(R // T,),
        in_specs=[
            pl.BlockSpec((H, POOL_WIDTH), lambda i: (jnp.maximum(i * (T // H) - 1, 0), 0)),
            pl.BlockSpec((T, POOL_WIDTH), lambda i: (i, 0)),
            pl.BlockSpec((H, POOL_WIDTH), lambda i: (jnp.minimum((i + 1) * (T // H), nb8 - 1), 0)),
            pl.BlockSpec((POOL_WIDTH, POOL_WIDTH), lambda i: (0, 0)),
            pl.BlockSpec((1, POOL_WIDTH), lambda i: (0, 0)),
        ],
        out_specs=pl.BlockSpec((T, POOL_WIDTH), lambda i: (i, 0)),
        out_shape=jax.ShapeDtypeStruct((R, POOL_WIDTH), BF16),
        scratch_shapes=[pltpu.VMEM((T + 2 * H, POOL_WIDTH), F32)],
        compiler_params=_cparams(("parallel",)),
        name="pool",
    )(u_all, u_all, u_all, w_bd, pool_scale)


def _merge_kernel(x_ref, sh1_ref, sc1_ref, gt1_ref, sh2_ref, sc2_ref, g1_ref, g2_ref, a_ref, b_ref, p_ref,
                  wg_ref, wa_ref, wb_ref, wp_ref, wo_ref, wrt_ref, xn_ref, h2_ref, afft_ref):
    D = x_ref.shape[1]
    x = x_ref[...]
    h = _rms(x, g1_ref[...]) * (1.0 + sc1_ref[0]) + sh1_ref[0]
    g = jax.nn.sigmoid(_dot(h.astype(BF16), wg_ref[...]))
    m = (g[:, 0:D] * _dot(a_ref[...], wa_ref[...]) + g[:, D:2 * D] * _dot(b_ref[...], wb_ref[...])
         + g[:, 2 * D:3 * D] * _dot(p_ref[...], wp_ref[...]))
    xn = x + gt1_ref[0] * _dot(m.astype(BF16), wo_ref[...])
    xn_ref[...] = xn
    h2 = _rms(xn, g2_ref[...]) * (1.0 + sc2_ref[0]) + sh2_ref[0]
    h2_ref[...] = h2
    logit_t = _dot_nt(wrt_ref[...], h2.astype(BF16))
    ex_t = jnp.exp(logit_t - jnp.max(logit_t, axis=0, keepdims=True))
    afft_ref[...] = ex_t / jnp.sum(ex_t, axis=0, keepdims=True)


def _merge(x_all, mod, g1, g2, a_all, b_all, p_all, wts, *, tiles_per_batch, n_batch):
    R, D = x_all.shape
    T = ROW_TILE

    def seg(i):
        return jnp.minimum(i // tiles_per_batch, n_batch)

    modspec = lambda k: pl.BlockSpec((1, 1, D), lambda i: (seg(i), 0, k))
    full = lambda a: pl.BlockSpec(a.shape, lambda i: (0,) * a.ndim)
    row = lambda w: pl.BlockSpec((T, w), lambda i: (i, 0))
    outs = [
        jax.ShapeDtypeStruct((R, D), F32),
        jax.ShapeDtypeStruct((R, D), F32),
        jax.ShapeDtypeStruct((N_EXPERTS, R), F32),
    ]
    wnames = ("wg", "wa", "wbn", "wp", "wo", "wrt")
    return pl.pallas_call(
        _merge_kernel,
        grid=(R // T,),
        in_specs=[row(D), modspec(0), modspec(1), modspec(2), modspec(3), modspec(4), full(g1), full(g2),
                  row(MLA_W), row(NA_W), row(POOL_WIDTH)] + [full(wts[n]) for n in wnames],
        out_specs=[row(D), row(D), pl.BlockSpec((N_EXPERTS, T), lambda i: (0, i))],
        out_shape=outs,
        compiler_params=_cparams(("parallel",)),
        name="merge",
    )(x_all, mod, mod, mod, mod, mod, g1, g2, a_all, b_all, p_all, *[wts[n] for n in wnames])


def _cumsum_lanes(maskf):
    n = maskf.shape[1]
    r = lax.broadcasted_iota(I32, (LANES, LANES), 0)
    c = lax.broadcasted_iota(I32, (LANES, LANES), 1)
    upper = jnp.where(r <= c, 1.0, 0.0).astype(BF16)
    off = jnp.zeros((maskf.shape[0], 1), F32)
    pieces = []
    for ch in range(n // LANES):
        w = _dot(maskf[:, ch * LANES:(ch + 1) * LANES].astype(BF16), upper)
        pieces.append(w + off)
        off = off + w[:, LANES - 1:LANES]
    return jnp.concatenate(pieces, axis=1)


def _topk_kernel(aff_ref, idx_ref, gate_ref, cs_ref, *, n, cap, row0, jt):
    s = pl.program_id(0)
    bits = lax.bitcast_convert_type(aff_ref[...], I32)

    def search(it, thr):
        cand = thr | jnp.left_shift(jnp.int32(1), 30 - it)
        cnt = jnp.sum((bits >= cand).astype(I32), axis=1, keepdims=True)
        return jnp.where(cnt >= cap, cand, thr)

    thr = lax.fori_loop(0, 31, search, jnp.zeros((N_EXPERTS, 1), I32))
    gt = bits > thr
    eq = bits == thr
    need = (cap - jnp.sum(gt.astype(I32), axis=1, keepdims=True)).astype(F32)
    eq_rank = _cumsum_lanes(jnp.where(eq, 1.0, 0.0))
    sel = gt | (eq & (eq_rank <= need))
    cs_ref[...] = _cumsum_lanes(jnp.where(sel, 1.0, 0.0))
    base = row0 + s * n
    njt = cap // jt

    def fill(it, carry):
        e = it // njt
        j0 = pl.multiple_of((it % njt) * jt, jt)
        jcol = (j0 + lax.broadcasted_iota(I32, (jt, 1), 0)).astype(F32)
        cs_e = cs_ref[pl.ds(e, 1), :]
        cnt = jnp.sum((cs_e <= jcol).astype(I32), axis=1, keepdims=True)
        idx_ref[0, e, pl.ds(j0, jt), :] = cnt + base
        tok = lax.broadcasted_iota(I32, (1, n), 1)
        gate_ref[0, e, pl.ds(j0, jt), :] = jnp.sum(jnp.where(tok == cnt, aff_ref[pl.ds(e, 1), :], 0.0), axis=1,
                                                   keepdims=True)
        return carry

    lax.fori_loop(0, N_EXPERTS * njt, fill, 0)


def _topk(aff_t, *, n_sets, n, cap, row0):
    jt = min(128, cap)
    blk0 = row0 // n
    spec = pl.BlockSpec((1, N_EXPERTS, cap, 1), lambda s: (s, 0, 0, 0))
    idx, gate = pl.pallas_call(
        functools.partial(_topk_kernel, n=n, cap=cap, row0=row0, jt=jt),
        grid=(n_sets,),
        in_specs=[pl.BlockSpec((N_EXPERTS, n), lambda s: (0, blk0 + s))],
        out_specs=[spec, spec],
        out_shape=[jax.ShapeDtypeStruct((n_sets, N_EXPERTS, cap, 1), I32),
                   jax.ShapeDtypeStruct((n_sets, N_EXPERTS, cap, 1), F32)],
        scratch_shapes=[pltpu.VMEM((N_EXPERTS, n), F32)],
        compiler_params=_cparams(("parallel",)),
        name="topk",
    )(aff_t)
    idx = jnp.transpose(idx[..., 0], (1, 0, 2)).reshape(N_EXPERTS, n_sets * cap)
    gate = jnp.transpose(gate, (1, 0, 2, 3)).reshape(N_EXPERTS, n_sets * cap, 1)
    return idx, gate


def _moe_kernel(idx_ref, h_hbm, gate_ref, wg_ref, wu_ref, wd_ref, y_ref, xs, xb, sem, *, rows, nf):
    e = pl.program_id(0)
    f = pl.program_id(1)

    def row_copy(j, t):
        return pltpu.make_async_copy(h_hbm.at[t], xs.at[j], sem)

    @pl.when(f == 0)
    def _gather():
        def issue(j, carry):
            row_copy(j, idx_ref[e, j]).start()
            return carry
        lax.fori_loop(0, rows, issue, 0)

        def wait(j, carry):
            row_copy(j, 0).wait()
            return carry
        lax.fori_loop(0, rows, wait, 0)
        for s in range(SUBLANES):
            xb[:, s * LANES:(s + 1) * LANES] = xs[:, s, :].astype(BF16)
        y_ref[0] = jnp.zeros(y_ref.shape[1:], F32)

    xv = xb[...]
    a = _dot(xv, wg_ref[0].astype(BF16))
    u = _dot(xv, wu_ref[0].astype(BF16))
    hmid = (a * jax.nn.sigmoid(a)) * u
    y_ref[0] += _dot(hmid.astype(BF16), wd_ref[0].astype(BF16))

    @pl.when(f == nf - 1)
    def _scale():
        y_ref[0] = y_ref[0] * gate_ref[0]


def _moe(idx, gate, h2, w_gate, w_up, w_down):
    E, rows = idx.shape
    _, D, F = w_gate.shape
    assert D == SUBLANES * LANES
    tf = 256
    nf = F // tf
    grid_spec = pltpu.PrefetchScalarGridSpec(
        num_scalar_prefetch=1,
        grid=(E, nf),
        in_specs=[
            pl.BlockSpec(memory_space=pl.ANY),
            pl.BlockSpec((1, rows, 1), lambda e, f, idx: (e, 0, 0)),
            pl.BlockSpec((1, D, tf), lambda e, f, idx: (e, 0, f)),
            pl.BlockSpec((1, D, tf), lambda e, f, idx: (e, 0, f)),
            pl.BlockSpec((1, tf, D), lambda e, f, idx: (e, f, 0)),
        ],
        out_specs=pl.BlockSpec((1, rows, D), lambda e, f, idx: (e, 0, 0)),
        scratch_shapes=[
            pltpu.VMEM((rows, SUBLANES, LANES), F32),
            pltpu.VMEM((rows, D), BF16),
            pltpu.SemaphoreType.DMA,
        ],
    )
    return pl.pallas_call(
        functools.partial(_moe_kernel, rows=rows, nf=nf),
        grid_spec=grid_spec,
        out_shape=jax.ShapeDtypeStruct((E, rows, D), F32),
        compiler_params=_cparams(("arbitrary", "arbitrary")),
        name="moe",
    )(idx, h2.reshape(h2.shape[0], SUBLANES, LANES), gate, w_gate, w_up, w_down)


def _combine_kernel(idx_ref, y_hbm, x_ref, g2_ref, fg_ref, o_ref, slab, sem, *, tc, cap, set_col0, row0,
                    tiles_per_set, win, chunk, final):
    t_id = pl.program_id(0)
    e = pl.program_id(1)
    n_e = pl.num_programs(1)
    rows = y_hbm.shape[1]
    col0 = set_col0 + (t_id // tiles_per_set) * cap
    tile_start = row0 + t_id * tc

    def lower_bound(v):
        def body(_, lh):
            lo, hi = lh
            mid = (lo + hi) // 2
            less = idx_ref[e, col0 + jnp.minimum(mid, cap - 1)] < v
            active = lo < hi
            return (jnp.where(active & less, mid + 1, lo), jnp.where(active & jnp.logical_not(less), mid, hi))
        return lax.fori_loop(0, cap.bit_length() + 1, body, (jnp.int32(0), jnp.int32(cap)))[0]

    r0 = col0 + lower_bound(tile_start)
    r1 = col0 + lower_bound(tile_start + tc)
    w0 = pl.multiple_of(jnp.minimum((r0 // SUBLANES) * SUBLANES, rows - win), SUBLANES)

    def chunk_copy(k):
        return pltpu.make_async_copy(y_hbm.at[e, pl.ds(w0 + k * chunk, chunk), :],
                                     slab.at[pl.ds(k * chunk, chunk), :], sem)

    def needed(k):
        return (w0 + k * chunk < r1) & (w0 + (k + 1) * chunk > r0)

    for k in range(win // chunk):
        @pl.when(needed(k))
        def _start(k=k):
            chunk_copy(k).start()

    @pl.when(e == 0)
    def _init():
        o_ref[...] = jnp.zeros(o_ref.shape, F32)

    for k in range(win // chunk):
        @pl.when(needed(k))
        def _wait(k=k):
            chunk_copy(k).wait()

    def add_row(j, carry):
        t = idx_ref[e, j] - tile_start
        o_ref[pl.ds(t, 1), :] = o_ref[pl.ds(t, 1), :] + slab[pl.ds(j - w0, 1), :]
        return carry
    lax.fori_loop(r0, r1, add_row, 0)

    @pl.when(e == n_e - 1)
    def _finish():
        out = x_ref[...] + g2_ref[0] * o_ref[...]
        if final:
            out = _rms(out, fg_ref[...])
        o_ref[...] = out


def _combine(idx, y, x_all, mod, final_g, *, n_sets, n, cap, set_col0, row0, n_batch, latent, final):
    E, rows, D = y.shape
    tc = min(2048, n)
    tiles_per_set = n // tc
    chunk = min(256, tc, cap)
    win = min(tc, cap) + chunk
    assert rows >= win and (rows - win) % SUBLANES == 0
    blk0 = row0 // tc
    if latent:
        seg = lambda t: t // tiles_per_set
    else:
        seg = lambda t: n_batch
    grid_spec = pltpu.PrefetchScalarGridSpec(
        num_scalar_prefetch=1,
        grid=(n_sets * tiles_per_set, E),
        in_specs=[
            pl.BlockSpec(memory_space=pl.ANY),
            pl.BlockSpec((tc, D), lambda t, e, idx: (blk0 + t, 0)),
            pl.BlockSpec((1, 1, D), lambda t, e, idx: (seg(t), 0, 5)),
            pl.BlockSpec((1, D), lambda t, e, idx: (0, 0)),
        ],
        out_specs=pl.BlockSpec((tc, D), lambda t, e, idx: (t, 0)),
        scratch_shapes=[pltpu.VMEM((win, D), F32), pltpu.SemaphoreType.DMA],
    )
    return pl.pallas_call(
        functools.partial(_combine_kernel, tc=tc, cap=cap, set_col0=set_col0, row0=row0,
                          tiles_per_set=tiles_per_set, win=win, chunk=chunk, final=final),
        grid_spec=grid_spec,
        out_shape=jax.ShapeDtypeStruct((n_sets * n, D), F32),
        compiler_params=_cparams(("arbitrary", "arbitrary")),
        name="combine_lat" if latent else "combine_ctx",
    )(idx, y, x_all, mod, final_g)


def _rope_tables(n_lat, n_ctx):
    half = MLA_ROPE // 2
    quarter = half // 2
    inv = ROPE_BASE ** (-jnp.arange(quarter, dtype=F32) / quarter)
    t = jnp.arange(n_lat)
    ang_r = (t // GRID_W).astype(F32)[:, None] * inv
    ang_c = (t % GRID_W).astype(F32)[:, None] * inv
    cos32 = jnp.concatenate([jnp.cos(ang_r), jnp.cos(ang_r), jnp.cos(ang_c), jnp.cos(ang_c)], axis=1)
    sin32 = jnp.concatenate([-jnp.sin(ang_r), jnp.sin(ang_r), -jnp.sin(ang_c), jnp.sin(ang_c)], axis=1)
    pad = HEAD_PAD - MLA_NOPE - MLA_ROPE
    cos_l = jnp.concatenate([jnp.ones((n_lat, MLA_NOPE), F32), cos32, jnp.zeros((n_lat, pad), F32)], axis=1)
    sin_l = jnp.concatenate([jnp.zeros((n_lat, MLA_NOPE), F32), sin32, jnp.zeros((n_lat, pad), F32)], axis=1)
    cos_c = jnp.concatenate([jnp.ones((ROW_TILE, MLA_NOPE + MLA_ROPE), F32), jnp.zeros((ROW_TILE, pad), F32)], axis=1)
    sin_c = jnp.zeros((ROW_TILE, HEAD_PAD), F32)
    return jnp.concatenate([cos_l, cos_c]), jnp.concatenate([sin_l, sin_c])


def _rope_partner():
    half = MLA_ROPE // 2
    quarter = half // 2
    p = jnp.arange(MLA_ROPE)
    return (p // half) * half + (p % half + quarter) % half


def _layer_weights(l, w_in, mla_q_norm, mla_kv_norm, w_uq, w_ukv, w_pool, w_br_mla, w_br_na, w_br_pool, w_out,
                   w_router):
    D = w_in.shape[1]
    offs = [0]
    for s in IN_SIZES:
        offs.append(offs[-1] + s)
    wi = w_in[l]
    w_cq, w_ckv, w_kr, w_na, w_pl = [wi[:, offs[k]:offs[k + 1]] for k in range(5)]
    w_gt = wi[:, offs[5]:]
    partner = _rope_partner()
    zl = jnp.zeros((D, MLA_NOPE), F32)
    zr = jnp.zeros((D, HEAD_PAD - MLA_NOPE - MLA_ROPE), F32)
    w_kr_p = jnp.concatenate([zl, w_kr, zr], axis=1)
    w_krs_p = jnp.concatenate([zl, w_kr[:, partner], zr], axis=1)
    wb = jnp.concatenate([w_cq, w_ckv, w_kr_p, w_krs_p, w_na, w_pl], axis=1).astype(BF16)

    dq = MLA_NOPE + MLA_ROPE
    uq = w_uq[l].reshape(MLA_Q_RANK, MLA_HEADS, dq)
    zq = jnp.zeros((MLA_Q_RANK, MLA_HEADS, HEAD_PAD - dq), F32)
    wqm = jnp.concatenate([uq, zq], axis=2).reshape(MLA_Q_RANK, MLA_HEADS * HEAD_PAD).astype(BF16)
    wqs = jnp.concatenate([jnp.zeros((MLA_Q_RANK, MLA_HEADS, MLA_NOPE), F32), uq[:, :, MLA_NOPE:][:, :, partner], zq],
                          axis=2).reshape(MLA_Q_RANK, MLA_HEADS * HEAD_PAD).astype(BF16)
    ukv = w_ukv[l].reshape(MLA_KV_RANK, MLA_HEADS, MLA_NOPE + MLA_V)
    wk = jnp.concatenate([ukv[:, :, :MLA_NOPE], jnp.zeros((MLA_KV_RANK, MLA_HEADS, HEAD_PAD - MLA_NOPE), F32)],
                         axis=2).reshape(MLA_KV_RANK, MLA_HEADS * HEAD_PAD).astype(BF16)
    wv = ukv[:, :, MLA_NOPE:].reshape(MLA_KV_RANK, MLA_W).astype(BF16)

    w_bd = jnp.zeros((POOL_WIDTH, POOL_WIDTH), F32)
    for g in range(len(POOL_WINDOWS)):
        w_bd = w_bd.at[g * POOL_GROUP:(g + 1) * POOL_GROUP, g * POOL_GROUP:(g + 1) * POOL_GROUP].set(w_pool[l, g])
    return dict(
        wb=wb, gq=mla_q_norm[l][None], gkv=mla_kv_norm[l][None], wqm=wqm, wqs=wqs, wk=wk, wv=wv,
        w_bd=w_bd.astype(BF16), wg=w_gt.astype(BF16), wa=w_br_mla[l].astype(BF16), wbn=w_br_na[l].astype(BF16),
        wp=w_br_pool[l].astype(BF16), wo=w_out[l].astype(BF16), wrt=w_router[l].T.astype(BF16))


def kernel(x, c, ctx, c_ctx, norm1_g, norm2_g, w_ada, b_ada, w_in, mla_q_norm, mla_kv_norm, w_uq, w_ukv, na_rpb, w_pool, pool_scale, w_br_mla, w_br_na, w_br_pool, w_out, w_router, w_gate, w_up, w_down, final_g):
    B, N, D = x.shape
    Cx = ctx.shape[1]
    L = w_in.shape[0]
    E = N_EXPERTS
    T = ROW_TILE
    assert Cx == T and N % (NA_KROWS * GRID_W) == 0 and B + 1 <= 8
    tiles_per_batch = N // T
    n_lat_tiles = B * tiles_per_batch
    cap_l = EC_CAPACITY * N // E
    cap_c = EC_CAPACITY * Cx // E

    x_all = jnp.concatenate([x.reshape(B * N, D), ctx.reshape(B * Cx, D)], axis=0)
    cc = jnp.concatenate([c, c_ctx[None], jnp.zeros((8 - B - 1, D), F32)], axis=0)
    mod_all = _ada(cc, w_ada, b_ada).reshape(L, 8, 1, 6 * D)
    cos_t, sin_t = _rope_tables(N, Cx)
    dims = dict(n_batch=B, n_lat=N, n_ctx=Cx)

    out = None
    for l in range(L):
        last = l == L - 1
        wts = _layer_weights(l, w_in, mla_q_norm, mla_kv_norm, w_uq, w_ukv, w_pool, w_br_mla, w_br_na, w_br_pool,
                             w_out, w_router)
        mod = mod_all[l]
        g1 = norm1_g[l][None]
        g2 = norm2_g[l][None]
        q_all, k_all, v_all, na_all, pool_u = _inproj(x_all, mod, g1, wts, cos_t, sin_t, n_lat_tiles=n_lat_tiles,
                                                      tiles_per_batch=tiles_per_batch, n_batch=B)
        a_l = _flash(q_all, k_all, v_all, latent=True, **dims)
        b_l = _na_lat(na_all, _na_bias(na_rpb[l], N // GRID_W), **dims)
        p_all = _pool(pool_u, wts["w_bd"], pool_scale[l][None], **dims)
        if not last:
            a_c = _flash(q_all, k_all, v_all, latent=False, **dims)
            b_c = _na_ctx(na_all, **dims)
        else:
            a_c = jnp.zeros((B * Cx, MLA_W), BF16)
            b_c = jnp.zeros((B * Cx, NA_W), BF16)
        a_all = jnp.concatenate([a_l, a_c], axis=0)
        b_all = jnp.concatenate([b_l, b_c], axis=0)
        xn_all, h2_all, aff_t = _merge(x_all, mod, g1, g2, a_all, b_all, p_all, wts, tiles_per_batch=tiles_per_batch,
                                       n_batch=B)

        idx, gate = _topk(aff_t, n_sets=B, n=N, cap=cap_l, row0=0)
        if not last:
            idx_c, gate_c = _topk(aff_t, n_sets=B, n=Cx, cap=cap_c, row0=B * N)
            idx = jnp.concatenate([idx, idx_c], axis=1)
            gate = jnp.concatenate([gate, gate_c], axis=1)
        y = _moe(idx, gate, h2_all, w_gate[l], w_up[l], w_down[l])
        fg = final_g[None]
        xl_new = _combine(idx, y, xn_all, mod, fg, n_sets=B, n=N, cap=cap_l, set_col0=0, row0=0, n_batch=B,
                          latent=True, final=last)
        if last:
            out = xl_new.reshape(B, N, D)
        else:
            xc_new = _combine(idx, y, xn_all, mod, fg, n_sets=B, n=Cx, cap=cap_c, set_col0=B * cap_l, row0=B * N,
                              n_batch=B, latent=False, final=False)
            x_all = jnp.concatenate([xl_new, xc_new], axis=0)
    return out
```

```python
import functools

import numpy as np
import jax
import jax.numpy as jnp
from jax import lax
from jax.experimental import pallas as pl
from jax.experimental.pallas import tpu as pltpu

F32 = jnp.float32
BF16 = jnp.bfloat16
I32 = jnp.int32

GRID_W = 64
MLA_HEADS = 8
MLA_Q_RANK = 384
MLA_KV_RANK = 256
MLA_NOPE = 64
MLA_ROPE = 32
MLA_V = 64
MLA_W = MLA_HEADS * MLA_V
NA_HEADS = 4
NA_HEAD_DIM = 64
NA_W = NA_HEADS * NA_HEAD_DIM
NA_WIN_R = 8
NA_WIN_C = 16
POOL_WINDOWS = (2, 4, 8, 16)
POOL_GROUP = 64
POOL_WIDTH = POOL_GROUP * len(POOL_WINDOWS)
N_EXPERTS = 16
EC_CAPACITY = 2
N_BRANCH = 3
ROPE_BASE = 10000.0
EPS = 1e-6
IN_SIZES = (MLA_Q_RANK, MLA_KV_RANK, MLA_ROPE, 3 * NA_W, POOL_WIDTH)

LANES = 128
SUBLANES = 8
HEAD_PAD = 128
V_ROWS = MLA_V + 16
QK_EXP2_SCALE = (MLA_NOPE + MLA_ROPE) ** -0.5 * 1.4426950408889634
ROW_TILE = 256
POOL_HALO = 8
GATHER_UNROLL = 8
COMBINE_UNROLL = 4
MASK_VALUE = -1e30
VMEM_LIMIT = 56 * 1024 * 1024


def _cparams(sem, vmem=VMEM_LIMIT):
    return pltpu.CompilerParams(dimension_semantics=sem, vmem_limit_bytes=vmem)


def _rms(x, g):
    return x * lax.rsqrt(jnp.mean(x * x, axis=-1, keepdims=True) + EPS) * g


def _dot(a, b):
    return jnp.dot(a, b, preferred_element_type=F32)


def _dot_nt(a, b):
    return lax.dot_general(a, b, (((1,), (1,)), ((), ())), preferred_element_type=F32)


def _ada_kernel(c_ref, w_ref, b_ref, o_ref):
    c = c_ref[...]
    s = c * jax.nn.sigmoid(c)
    o_ref[0] = _dot(s.astype(BF16), w_ref[0].astype(BF16)) + b_ref[0]


def _ada(cc, w_ada, b_ada):
    L, D, D6 = w_ada.shape
    tn = 1536
    return pl.pallas_call(
        _ada_kernel,
        grid=(L, D6 // tn),
        in_specs=[
            pl.BlockSpec((8, D), lambda l, j: (0, 0)),
            pl.BlockSpec((1, D, tn), lambda l, j: (l, 0, j)),
            pl.BlockSpec((1, 1, tn), lambda l, j: (l, 0, j)),
        ],
        out_specs=pl.BlockSpec((1, 8, tn), lambda l, j: (l, 0, j)),
        out_shape=jax.ShapeDtypeStruct((L, 8, D6), F32),
        compiler_params=_cparams(("parallel", "parallel")),
        name="ada",
    )(cc, w_ada, b_ada.reshape(L, 1, D6))


def _inproj_kernel(x_ref, sh_ref, sc_ref, g1_ref, wb_ref, gq_ref, gkv_ref, wqm_ref, wqs_ref, wk_ref, wv_ref,
                   vone_ref, cos_ref, sin_ref, q_ref, k_ref, vt_ref, na_ref, pool_ref):
    x = x_ref[...]
    h = _rms(x, g1_ref[...]) * (1.0 + sc_ref[0]) + sh_ref[0]
    z = _dot(h.astype(BF16), wb_ref[...])
    o = 0
    cq = z[:, o:o + MLA_Q_RANK]; o += MLA_Q_RANK
    ckv = z[:, o:o + MLA_KV_RANK]; o += MLA_KV_RANK
    kr = z[:, o:o + HEAD_PAD]; o += HEAD_PAD
    krs = z[:, o:o + HEAD_PAD]; o += HEAD_PAD
    na_ref[...] = z[:, o:o + 3 * NA_W].astype(BF16); o += 3 * NA_W
    pool_ref[...] = z[:, o:o + POOL_WIDTH]
    cqn = _rms(cq, gq_ref[...]).astype(BF16)
    ckvn = _rms(ckv, gkv_ref[...]).astype(BF16)
    cos = cos_ref[...]
    sin = sin_ref[...]
    cos8 = jnp.concatenate([cos] * MLA_HEADS, axis=1)
    sin8 = jnp.concatenate([sin] * MLA_HEADS, axis=1)
    q = _dot(cqn, wqm_ref[...]) * cos8 + _dot(cqn, wqs_ref[...]) * sin8
    q_ref[...] = (q * QK_EXP2_SCALE).astype(BF16)
    krot = kr * cos + krs * sin
    k = _dot(ckvn, wk_ref[...]) + jnp.concatenate([krot] * MLA_HEADS, axis=1)
    k_ref[...] = k.astype(BF16)
    vt_ref[...] = (_dot(ckvn, wv_ref[...]) + vone_ref[...]).T.astype(BF16)


def _inproj(x_all, mod, g1, wts, cos_t, sin_t, *, n_lat_tiles, tiles_per_batch, n_batch):
    R, D = x_all.shape
    T = ROW_TILE
    nb = wts["wb"].shape[1]

    def seg(i):
        return jnp.minimum(i // tiles_per_batch, n_batch)

    def tab(i):
        return jnp.where(i < n_lat_tiles, i % tiles_per_batch, tiles_per_batch)

    full = lambda a: pl.BlockSpec(a.shape, lambda i: (0,) * a.ndim)
    outs = [
        jax.ShapeDtypeStruct((R, MLA_HEADS * HEAD_PAD), BF16),
        jax.ShapeDtypeStruct((R, MLA_HEADS * HEAD_PAD), BF16),
        jax.ShapeDtypeStruct((MLA_HEADS * V_ROWS, R), BF16),
        jax.ShapeDtypeStruct((R, 3 * NA_W), BF16),
        jax.ShapeDtypeStruct((R, POOL_WIDTH), F32),
    ]
    row_major = lambda s: pl.BlockSpec((T, s.shape[1]), lambda i: (i, 0))
    out_specs = [row_major(outs[0]), row_major(outs[1]), pl.BlockSpec((MLA_HEADS * V_ROWS, T), lambda i: (0, i)),
                 row_major(outs[3]), row_major(outs[4])]
    return pl.pallas_call(
        _inproj_kernel,
        grid=(R // T,),
        in_specs=[
            pl.BlockSpec((T, D), lambda i: (i, 0)),
            pl.BlockSpec((1, 1, D), lambda i: (seg(i), 0, 0)),
            pl.BlockSpec((1, 1, D), lambda i: (seg(i), 0, 1)),
            full(g1), full(wts["wb"]), full(wts["gq"]), full(wts["gkv"]), full(wts["wqm"]), full(wts["wqs"]),
            full(wts["wk"]), full(wts["wv"]), full(wts["vone"]),
            pl.BlockSpec((T, HEAD_PAD), lambda i: (tab(i), 0)),
            pl.BlockSpec((T, HEAD_PAD), lambda i: (tab(i), 0)),
        ],
        out_specs=out_specs,
        out_shape=outs,
        compiler_params=_cparams(("parallel",)),
        name="inproj",
    )(x_all, mod, mod, g1, wts["wb"], wts["gq"], wts["gkv"], wts["wqm"], wts["wqs"], wts["wk"], wts["wv"],
      wts["vone"], cos_t, sin_t)


def _flash_kernel(*refs, tk, n_chunks):
    if n_chunks:
        q_ref, kc_ref, vtc_ref, kl_ref, vtl_ref, o_ref, m_ref, acc_ref, s_ref = refs
    else:
        q_ref, kc_ref, vtc_ref, o_ref, m_ref, acc_ref = refs
    m_ref[...] = jnp.full(m_ref.shape, -jnp.inf, F32)
    acc_ref[...] = jnp.zeros(acc_ref.shape, F32)

    def scores(h, k2):
        return _dot_nt(k2[:, h * HEAD_PAD:(h + 1) * HEAD_PAD], q_ref[:, h * HEAD_PAD:(h + 1) * HEAD_PAD])

    def softmax_pv(h, s, vt):
        m_old = m_ref[h]
        m_new = jnp.maximum(m_old, jnp.max(s, axis=0, keepdims=True))
        p = jnp.exp2(s - m_new)
        acc_ref[h] = jnp.exp2(m_old - m_new) * acc_ref[h] + _dot(vt[h * V_ROWS:(h + 1) * V_ROWS, :], p.astype(BF16))
        m_ref[h] = m_new

    kc = kc_ref[...]
    vtc = vtc_ref[...]
    for h in range(2):
        softmax_pv(h, scores(h, kc), vtc)

    if n_chunks:
        def kchunk(i):
            return kl_ref[pl.ds(pl.multiple_of(i * tk, tk), tk), :]

        def vchunk(i):
            return vtl_ref[:, pl.ds(pl.multiple_of(i * tk, tk), tk)]

        def qk(slot, k2):
            for h in range(2):
                s_ref[slot, h] = scores(h, k2)

        def spv(slot, vt):
            for h in range(2):
                softmax_pv(h, s_ref[slot, h], vt)

        qk(0, kchunk(0))

        def body(i2, carry):
            c0 = 2 * i2
            qk(1, kchunk(c0 + 1))
            spv(0, vchunk(c0))
            qk(0, kchunk(jnp.minimum(c0 + 2, n_chunks - 1)))
            spv(1, vchunk(c0 + 1))
            return carry
        lax.fori_loop(0, n_chunks // 2, body, 0)
    a0 = acc_ref[0]
    a1 = acc_ref[1]
    o_t = jnp.concatenate([a0[0:MLA_V] / a0[MLA_V:MLA_V + 1], a1[0:MLA_V] / a1[MLA_V:MLA_V + 1]], axis=0)
    o_ref[...] = o_t.T.astype(BF16)


def _flash(q_all, k_all, vt_all, *, n_batch, n_lat, n_ctx, latent):
    ctx_blk0 = (n_batch * n_lat) // n_ctx
    n_pairs = MLA_HEADS // 2
    if latent:
        tq = min(1024, n_lat)
        tk = min(512, n_lat // 2)
        nq = n_lat // tq
        q_map = lambda b, p, i: (b * nq + i, p)
        n_rows = n_batch * n_lat
        n_chunks = n_lat // tk
        assert n_chunks % 2 == 0
    else:
        tq, nq, tk, n_chunks = n_ctx, 1, n_ctx, 0
        q_map = lambda b, p, i: (ctx_blk0 + b, p)
        n_rows = n_batch * n_ctx
    in_specs = [
        pl.BlockSpec((tq, 2 * HEAD_PAD), q_map),
        pl.BlockSpec((n_ctx, 2 * HEAD_PAD), lambda b, p, i: (ctx_blk0 + b, p)),
        pl.BlockSpec((2 * V_ROWS, n_ctx), lambda b, p, i: (p, ctx_blk0 + b)),
    ]
    args = [q_all, k_all, vt_all]
    scratch = [pltpu.VMEM((2, 1, tq), F32), pltpu.VMEM((2, V_ROWS, tq), F32)]
    if latent:
        in_specs += [
            pl.BlockSpec((n_lat, 2 * HEAD_PAD), lambda b, p, i: (b, p)),
            pl.BlockSpec((2 * V_ROWS, n_lat), lambda b, p, i: (p, b)),
        ]
        args += [k_all, vt_all]
        scratch.append(pltpu.VMEM((2, 2, tk, tq), F32))
    return pl.pallas_call(
        functools.partial(_flash_kernel, tk=tk, n_chunks=n_chunks),
        grid=(n_batch, n_pairs, nq),
        in_specs=in_specs,
        out_specs=pl.BlockSpec((tq, 2 * MLA_V), lambda b, p, i: (b * nq + i, p)),
        out_shape=jax.ShapeDtypeStruct((n_rows, MLA_W), BF16),
        scratch_shapes=scratch,
        compiler_params=_cparams(("parallel", "parallel", "arbitrary")),
        name="flash_lat" if latent else "flash_ctx",
    )(*args)


NA_QROWS = 8
NA_KROWS = 16


def _na_softmax_pv(parts):
    m = None
    for s, _ in parts:
        mi = jnp.max(s, axis=1, keepdims=True)
        m = mi if m is None else jnp.maximum(m, mi)
    l = None
    o = None
    for s, v in parts:
        p = jnp.exp(s - m)
        li = jnp.sum(p, axis=1, keepdims=True)
        oi = _dot(p.astype(BF16), v)
        l = li if l is None else l + li
        o = oi if o is None else o + oi
    return o / l


def _na_lat_kernel(q_ref, k_ref, v_ref, kc_ref, vc_ref, bias_ref, o_ref, *, rows, scale):
    i = pl.program_id(1)
    s0 = jnp.clip(NA_QROWS * i - NA_WIN_R // 2, 0, rows - NA_KROWS) * GRID_W
    s0 = pl.multiple_of(s0, 256)
    nk = NA_KROWS * GRID_W
    lo = lax.broadcasted_iota(I32, (1, LANES), 1) < NA_HEAD_DIM
    outs = []
    for pr in range(NA_HEADS // 2):
        cs = slice(pr * LANES, (pr + 1) * LANES)
        qp = q_ref[:, cs]
        kw = k_ref[pl.ds(s0, nk), cs]
        vw = v_ref[pl.ds(s0, nk), cs]
        kc = kc_ref[:, cs]
        vc = vc_ref[:, cs]
        o_pair = None
        for hh in range(2):
            msk = lo if hh == 0 else jnp.logical_not(lo)
            qh = jnp.where(msk, qp, jnp.zeros_like(qp))
            s_lat = _dot_nt(qh, kw) * scale + bias_ref[0, 2 * pr + hh]
            s_ctx = _dot_nt(qh, kc) * scale
            o = _na_softmax_pv([(s_lat, vw), (s_ctx, vc)])
            o_pair = o if hh == 0 else jnp.where(lo, o_pair, o)
        outs.append(o_pair)
    o_ref[...] = jnp.concatenate(outs, axis=1).astype(BF16)


def _na_lat(na_all, bias, *, n_batch, n_lat, n_ctx):
    rows = n_lat // GRID_W
    nblk = rows // NA_QROWS
    tq = NA_QROWS * GRID_W
    ctx_blk0 = (n_batch * n_lat) // n_ctx
    scale = NA_HEAD_DIM ** -0.5

    def variant(i):
        return jnp.where(i == 0, 0, jnp.where(i == nblk - 1, 2, 1))

    return pl.pallas_call(
        functools.partial(_na_lat_kernel, rows=rows, scale=scale),
        grid=(n_batch, nblk),
        in_specs=[
            pl.BlockSpec((tq, NA_W), lambda b, i: (b * nblk + i, 0)),
            pl.BlockSpec((n_lat, NA_W), lambda b, i: (b, 1)),
            pl.BlockSpec((n_lat, NA_W), lambda b, i: (b, 2)),
            pl.BlockSpec((n_ctx, NA_W), lambda b, i: (ctx_blk0 + b, 1)),
            pl.BlockSpec((n_ctx, NA_W), lambda b, i: (ctx_blk0 + b, 2)),
            pl.BlockSpec((1, NA_HEADS, tq, NA_KROWS * GRID_W), lambda b, i: (variant(i), 0, 0, 0)),
        ],
        out_specs=pl.BlockSpec((tq, NA_W), lambda b, i: (b * nblk + i, 0)),
        out_shape=jax.ShapeDtypeStruct((n_batch * n_lat, NA_W), BF16),
        compiler_params=_cparams(("parallel", "arbitrary")),
        name="na_lat",
    )(na_all, na_all, na_all, na_all, na_all, bias)


def _na_ctx_kernel(q_ref, k_ref, v_ref, o_ref, *, scale):
    lo = lax.broadcasted_iota(I32, (1, LANES), 1) < NA_HEAD_DIM
    outs = []
    for pr in range(NA_HEADS // 2):
        cs = slice(pr * LANES, (pr + 1) * LANES)
        qp = q_ref[:, cs]
        kp = k_ref[:, cs]
        vp = v_ref[:, cs]
        o_pair = None
        for hh in range(2):
            msk = lo if hh == 0 else jnp.logical_not(lo)
            qh = jnp.where(msk, qp, jnp.zeros_like(qp))
            o = _na_softmax_pv([(_dot_nt(qh, kp) * scale, vp)])
            o_pair = o if hh == 0 else jnp.where(lo, o_pair, o)
        outs.append(o_pair)
    o_ref[...] = jnp.concatenate(outs, axis=1).astype(BF16)


def _na_ctx(na_all, *, n_batch, n_lat, n_ctx):
    ctx_blk0 = (n_batch * n_lat) // n_ctx
    return pl.pallas_call(
        functools.partial(_na_ctx_kernel, scale=NA_HEAD_DIM ** -0.5),
        grid=(n_batch,),
        in_specs=[pl.BlockSpec((n_ctx, NA_W), lambda b, j=j: (ctx_blk0 + b, j)) for j in range(3)],
        out_specs=pl.BlockSpec((n_ctx, NA_W), lambda b: (b, 0)),
        out_shape=jax.ShapeDtypeStruct((n_batch * n_ctx, NA_W), BF16),
        compiler_params=_cparams(("parallel",)),
        name="na_ctx",
    )(na_all, na_all, na_all)


def _na_bias(rpb, rows):
    nblk = rows // NA_QROWS
    qc = np.arange(GRID_W)
    kc = np.arange(GRID_W)
    start_c = np.clip(qc - NA_WIN_C // 2, 0, GRID_W - NA_WIN_C)
    valid_c = (kc[None, :] >= start_c[:, None]) & (kc[None, :] < start_c[:, None] + NA_WIN_C)
    dc = np.clip(kc[None, :] - qc[:, None], -(NA_WIN_C - 1), NA_WIN_C - 1) + (NA_WIN_C - 1)
    onehot = (dc.reshape(1, -1) == np.arange(2 * NA_WIN_C - 1)[:, None]).astype(np.float32)
    by_dr = jnp.einsum("hab,bn->han", rpb.astype(F32), jnp.asarray(onehot), precision=lax.Precision.HIGHEST)
    by_dr = by_dr.reshape(NA_HEADS, 2 * NA_WIN_R - 1, GRID_W, GRID_W)
    tables = []
    for i in (0, min(1, nblk - 1), nblk - 1):
        s0 = min(max(NA_QROWS * i - NA_WIN_R // 2, 0), rows - NA_KROWS)
        qrow = NA_QROWS * i + np.arange(NA_QROWS)
        krow = s0 + np.arange(NA_KROWS)
        start_r = np.clip(qrow - NA_WIN_R // 2, 0, rows - NA_WIN_R)
        valid_r = (krow[None, :] >= start_r[:, None]) & (krow[None, :] < start_r[:, None] + NA_WIN_R)
        dr = np.clip(krow[None, :] - qrow[:, None] + (NA_WIN_R - 1), 0, 2 * NA_WIN_R - 2)
        b = jnp.transpose(by_dr[:, dr], (0, 1, 3, 2, 4))
        valid = valid_r[:, None, :, None] & valid_c[None, :, None, :]
        b = jnp.where(jnp.asarray(valid)[None], b, MASK_VALUE)
        tables.append(b.reshape(NA_HEADS, NA_QROWS * GRID_W, NA_KROWS * GRID_W))
    return jnp.stack(tables)


def _pool_kernel(prev_ref, cur_ref, next_ref, w_ref, ps_ref, o_ref, scr, *, tiles_per_batch, n_lat_tiles, n_lat, n_ctx):
    i = pl.program_id(0)
    T = ROW_TILE
    H = POOL_HALO
    is_lat = i < n_lat_tiles
    tile_in_seq = jnp.where(is_lat, i % tiles_per_batch, 0)
    n_seq = jnp.where(is_lat, n_lat, n_ctx)
    first = tile_in_seq == 0
    last = jnp.where(is_lat, tile_in_seq == tiles_per_batch - 1, True)
    u = cur_ref[...]
    scr[0:H, :] = jnp.where(first, 0.0, prev_ref[...])
    scr[H:H + T, :] = u
    scr[H + T:H + T + H, :] = jnp.where(last, 0.0, next_ref[...])

    def win(lo, hi):
        acc = None
        for j in range(lo, hi):
            piece = scr[H + j:H + j + T, :]
            acc = piece if acc is None else acc + piece
        return acc

    t = (tile_in_seq * T + lax.broadcasted_iota(I32, (T, 1), 0))
    lane = lax.broadcasted_iota(I32, (1, POOL_WIDTH), 1)
    sums = {}
    acc = win(-1, 1)
    sums[2] = acc
    acc = acc + win(-2, -1) + win(1, 2)
    sums[4] = acc
    acc = acc + win(-4, -2) + win(2, 4)
    sums[8] = acc
    acc = acc + win(-8, -4) + win(4, 8)
    sums[16] = acc
    mean = None
    for gi, w in enumerate(POOL_WINDOWS):
        cnt = (jnp.minimum(t + w // 2, n_seq) - jnp.maximum(t - w // 2, 0)).astype(F32)
        mg = sums[w] / cnt
        mean = mg if mean is None else jnp.where(lane < gi * POOL_GROUP, mean, mg)
    dlt = mean - u
    o_ref[...] = (_dot(dlt.astype(BF16), w_ref[...]) * ps_ref[...]).astype(BF16)


def _pool(u_all, w_bd, pool_scale, *, n_batch, n_lat, n_ctx):
    R = u_all.shape[0]
    T = ROW_TILE
    H = POOL_HALO
    assert n_ctx == T
    tiles_per_batch = n_lat // T
    n_lat_tiles = n_batch * tiles_per_batch
    nb8 = R // H
    return pl.pallas_call(
        functools.partial(_pool_kernel, tiles_per_batch=tiles_per_batch, n_lat_tiles=n_lat_tiles, n_lat=n_lat,
                          n_ctx=n_ctx),
        grid=(R // T,),
        in_specs=[
            pl.BlockSpec((H, POOL_WIDTH), lambda i: (jnp.maximum(i * (T // H) - 1, 0), 0)),
            pl.BlockSpec((T, POOL_WIDTH), lambda i: (i, 0)),
            pl.BlockSpec((H, POOL_WIDTH), lambda i: (jnp.minimum((i + 1) * (T // H), nb8 - 1), 0)),
            pl.BlockSpec((POOL_WIDTH, POOL_WIDTH), lambda i: (0, 0)),
            pl.BlockSpec((1, POOL_WIDTH), lambda i: (0, 0)),
        ],
        out_specs=pl.BlockSpec((T, POOL_WIDTH), lambda i: (i, 0)),
        out_shape=jax.ShapeDtypeStruct((R, POOL_WIDTH), BF16),
        scratch_shapes=[pltpu.VMEM((T + 2 * H, POOL_WIDTH), F32)],
        compiler_params=_cparams(("parallel",)),
        name="pool",
    )(u_all, u_all, u_all, w_bd, pool_scale)


def _merge_kernel(x_ref, sh1_ref, sc1_ref, gt1_ref, sh2_ref, sc2_ref, g1_ref, g2_ref, a_ref, b_ref, p_ref,
                  wg_ref, wa_ref, wb_ref, wp_ref, wo_ref, wrt_ref, xn_ref, h2_ref, afft_ref):
    D = x_ref.shape[1]
    x = x_ref[...]
    h = _rms(x, g1_ref[...]) * (1.0 + sc1_ref[0]) + sh1_ref[0]
    g = jax.nn.sigmoid(_dot(h.astype(BF16), wg_ref[...]))
    m = (g[:, 0:D] * _dot(a_ref[...], wa_ref[...]) + g[:, D:2 * D] * _dot(b_ref[...], wb_ref[...])
         + g[:, 2 * D:3 * D] * _dot(p_ref[...], wp_ref[...]))
    xn = x + gt1_ref[0] * _dot(m.astype(BF16), wo_ref[...])
    xn_ref[...] = xn
    h2 = _rms(xn, g2_ref[...]) * (1.0 + sc2_ref[0]) + sh2_ref[0]
    h2_ref[...] = h2
    logit_t = _dot_nt(wrt_ref[...], h2.astype(BF16))
    ex_t = jnp.exp(logit_t - jnp.max(logit_t, axis=0, keepdims=True))
    afft_ref[...] = ex_t / jnp.sum(ex_t, axis=0, keepdims=True)


def _merge(x_all, mod, g1, g2, a_all, b_all, p_all, wts, *, tiles_per_batch, n_batch):
    R, D = x_all.shape
    T = ROW_TILE

    def seg(i):
        return jnp.minimum(i // tiles_per_batch, n_batch)

    modspec = lambda k: pl.BlockSpec((1, 1, D), lambda i: (seg(i), 0, k))
    full = lambda a: pl.BlockSpec(a.shape, lambda i: (0,) * a.ndim)
    row = lambda w: pl.BlockSpec((T, w), lambda i: (i, 0))
    outs = [
        jax.ShapeDtypeStruct((R, D), F32),
        jax.ShapeDtypeStruct((R, D), F32),
        jax.ShapeDtypeStruct((N_EXPERTS, R), F32),
    ]
    wnames = ("wg", "wa", "wbn", "wp", "wo", "wrt")
    return pl.pallas_call(
        _merge_kernel,
        grid=(R // T,),
        in_specs=[row(D), modspec(0), modspec(1), modspec(2), modspec(3), modspec(4), full(g1), full(g2),
                  row(MLA_W), row(NA_W), row(POOL_WIDTH)] + [full(wts[n]) for n in wnames],
        out_specs=[row(D), row(D), pl.BlockSpec((N_EXPERTS, T), lambda i: (0, i))],
        out_shape=outs,
        compiler_params=_cparams(("parallel",)),
        name="merge",
    )(x_all, mod, mod, mod, mod, mod, g1, g2, a_all, b_all, p_all, *[wts[n] for n in wnames])


def _cumsum_lanes(maskf):
    n = maskf.shape[1]
    r = lax.broadcasted_iota(I32, (LANES, LANES), 0)
    c = lax.broadcasted_iota(I32, (LANES, LANES), 1)
    upper = jnp.where(r <= c, 1.0, 0.0).astype(BF16)
    off = jnp.zeros((maskf.shape[0], 1), F32)
    pieces = []
    for ch in range(n // LANES):
        w = _dot(maskf[:, ch * LANES:(ch + 1) * LANES].astype(BF16), upper)
        pieces.append(w + off)
        off = off + w[:, LANES - 1:LANES]
    return jnp.concatenate(pieces, axis=1)


def _topk_kernel(aff_ref, idx_ref, gate_ref, cs_ref, *, n, cap, row0, jt):
    s = pl.program_id(0)
    bits = lax.bitcast_convert_type(aff_ref[...], I32)

    def search(it, thr):
        cand = thr | jnp.left_shift(jnp.int32(1), 30 - it)
        cnt = jnp.sum((bits >= cand).astype(I32), axis=1, keepdims=True)
        return jnp.where(cnt >= cap, cand, thr)

    thr = lax.fori_loop(0, 31, search, jnp.zeros((N_EXPERTS, 1), I32))
    gt = bits > thr
    eq = bits == thr
    need = (cap - jnp.sum(gt.astype(I32), axis=1, keepdims=True)).astype(F32)
    eq_rank = _cumsum_lanes(jnp.where(eq, 1.0, 0.0))
    sel = gt | (eq & (eq_rank <= need))
    cs_ref[...] = _cumsum_lanes(jnp.where(sel, 1.0, 0.0))
    base = row0 + s * n
    njt = cap // jt

    def fill(it, carry):
        e = it // njt
        j0 = pl.multiple_of((it % njt) * jt, jt)
        jcol = (j0 + lax.broadcasted_iota(I32, (jt, 1), 0)).astype(F32)
        cs_e = cs_ref[pl.ds(e, 1), :]
        cnt = jnp.sum((cs_e <= jcol).astype(I32), axis=1, keepdims=True)
        idx_ref[0, e, pl.ds(j0, jt), :] = cnt + base
        tok = lax.broadcasted_iota(I32, (1, n), 1)
        gate_ref[0, e, pl.ds(j0, jt), :] = jnp.sum(jnp.where(tok == cnt, aff_ref[pl.ds(e, 1), :], 0.0), axis=1,
                                                   keepdims=True)
        return carry

    lax.fori_loop(0, N_EXPERTS * njt, fill, 0)


def _topk(aff_t, *, n_sets, n, cap, row0):
    jt = min(128, cap)
    blk0 = row0 // n
    spec = pl.BlockSpec((1, N_EXPERTS, cap, 1), lambda s: (s, 0, 0, 0))
    idx, gate = pl.pallas_call(
        functools.partial(_topk_kernel, n=n, cap=cap, row0=row0, jt=jt),
        grid=(n_sets,),
        in_specs=[pl.BlockSpec((N_EXPERTS, n), lambda s: (0, blk0 + s))],
        out_specs=[spec, spec],
        out_shape=[jax.ShapeDtypeStruct((n_sets, N_EXPERTS, cap, 1), I32),
                   jax.ShapeDtypeStruct((n_sets, N_EXPERTS, cap, 1), F32)],
        scratch_shapes=[pltpu.VMEM((N_EXPERTS, n), F32)],
        compiler_params=_cparams(("parallel",)),
        name="topk",
    )(aff_t)
    idx = jnp.transpose(idx[..., 0], (1, 0, 2)).reshape(N_EXPERTS, n_sets * cap)
    gate = jnp.transpose(gate, (1, 0, 2, 3)).reshape(N_EXPERTS, n_sets * cap, 1)
    return idx, gate


def _moe_kernel(idx_ref, h_hbm, gate_ref, wg_ref, wu_ref, wd_ref, y_ref, xs, xb, sem, *, rows, nf):
    e = pl.program_id(0)
    f = pl.program_id(1)

    def row_copy(j, t):
        return pltpu.make_async_copy(h_hbm.at[t], xs.at[pl.ds(pl.multiple_of(j * SUBLANES, SUBLANES), SUBLANES), :],
                                     sem)

    @pl.when(f == 0)
    def _gather():
        base = e * rows

        def issue(j8, carry):
            for r in range(GATHER_UNROLL):
                j = j8 * GATHER_UNROLL + r
                row_copy(j, idx_ref[base + j]).start()
            return carry
        lax.fori_loop(0, rows // GATHER_UNROLL, issue, 0)

        def wait(j8, carry):
            for r in range(GATHER_UNROLL):
                row_copy(j8 * GATHER_UNROLL + r, 0).wait()
            return carry
        lax.fori_loop(0, rows // GATHER_UNROLL, wait, 0)
        for s in range(SUBLANES):
            xb[:, s * LANES:(s + 1) * LANES] = xs[pl.ds(s, rows, stride=SUBLANES), :].astype(BF16)
        y_ref[0] = jnp.zeros(y_ref.shape[1:], F32)

    xv = xb[...]
    a = _dot(xv, wg_ref[0].astype(BF16))
    u = _dot(xv, wu_ref[0].astype(BF16))
    hmid = (a * jax.nn.sigmoid(a)) * u
    y_ref[0] += _dot(hmid.astype(BF16), wd_ref[0].astype(BF16))

    @pl.when(f == nf - 1)
    def _scale():
        y_ref[0] = y_ref[0] * gate_ref[0]


def _moe(idx, gate, h2, w_gate, w_up, w_down):
    E, rows = idx.shape
    _, D, F = w_gate.shape
    assert D == SUBLANES * LANES and rows % GATHER_UNROLL == 0
    tf = 256
    nf = F // tf
    grid_spec = pltpu.PrefetchScalarGridSpec(
        num_scalar_prefetch=1,
        grid=(E, nf),
        in_specs=[
            pl.BlockSpec(memory_space=pl.ANY),
            pl.BlockSpec((1, rows, 1), lambda e, f, idx: (e, 0, 0)),
            pl.BlockSpec((1, D, tf), lambda e, f, idx: (e, 0, f)),
            pl.BlockSpec((1, D, tf), lambda e, f, idx: (e, 0, f)),
            pl.BlockSpec((1, tf, D), lambda e, f, idx: (e, f, 0)),
        ],
        out_specs=pl.BlockSpec((1, rows, D), lambda e, f, idx: (e, 0, 0)),
        scratch_shapes=[
            pltpu.VMEM((rows * SUBLANES, LANES), F32),
            pltpu.VMEM((rows, D), BF16),
            pltpu.SemaphoreType.DMA,
        ],
    )
    return pl.pallas_call(
        functools.partial(_moe_kernel, rows=rows, nf=nf),
        grid_spec=grid_spec,
        out_shape=jax.ShapeDtypeStruct((E, rows, D), F32),
        compiler_params=_cparams(("arbitrary", "arbitrary")),
        name="moe",
    )(idx.reshape(-1), h2.reshape(h2.shape[0], SUBLANES, LANES), gate, w_gate, w_up, w_down)


def _combine_kernel(idx_ref, y_hbm, x_ref, g2_ref, fg_ref, o_ref, slab, sem, *, tc, cap, set_col0, row0,
                    tiles_per_set, win, chunk, final):
    t_id = pl.program_id(0)
    e = pl.program_id(1)
    n_e = pl.num_programs(1)
    rows = y_hbm.shape[1]
    col0 = e * rows + set_col0 + (t_id // tiles_per_set) * cap
    row_base = e * rows
    tile_start = row0 + t_id * tc

    def lower_bound(v):
        def body(_, lh):
            lo, hi = lh
            mid = (lo + hi) // 2
            less = idx_ref[col0 + jnp.minimum(mid, cap - 1)] < v
            active = lo < hi
            return (jnp.where(active & less, mid + 1, lo), jnp.where(active & jnp.logical_not(less), mid, hi))
        return lax.fori_loop(0, cap.bit_length() + 1, body, (jnp.int32(0), jnp.int32(cap)))[0]

    r0 = col0 - row_base + lower_bound(tile_start)
    r1 = col0 - row_base + lower_bound(tile_start + tc)
    w0 = pl.multiple_of(jnp.minimum((r0 // SUBLANES) * SUBLANES, rows - win), SUBLANES)

    def chunk_copy(k):
        return pltpu.make_async_copy(y_hbm.at[e, pl.ds(w0 + k * chunk, chunk), :],
                                     slab.at[pl.ds(k * chunk, chunk), :], sem)

    def needed(k):
        return (w0 + k * chunk < r1) & (w0 + (k + 1) * chunk > r0)

    for k in range(win // chunk):
        @pl.when(needed(k))
        def _start(k=k):
            chunk_copy(k).start()

    @pl.when(e == 0)
    def _init():
        o_ref[...] = jnp.zeros(o_ref.shape, F32)

    for k in range(win // chunk):
        @pl.when(needed(k))
        def _wait(k=k):
            chunk_copy(k).wait()

    def add_rows(j0, n):
        ts = [idx_ref[row_base + j0 + r] - tile_start for r in range(n)]
        cur = [o_ref[pl.ds(t, 1), :] for t in ts]
        new = [slab[pl.ds(j0 + r - w0, 1), :] for r in range(n)]
        for r in range(n):
            o_ref[pl.ds(ts[r], 1), :] = cur[r] + new[r]

    n_groups = (r1 - r0) // COMBINE_UNROLL

    def add_group(g, carry):
        add_rows(r0 + g * COMBINE_UNROLL, COMBINE_UNROLL)
        return carry
    lax.fori_loop(0, n_groups, add_group, 0)

    def add_tail(j, carry):
        add_rows(j, 1)
        return carry
    lax.fori_loop(r0 + n_groups * COMBINE_UNROLL, r1, add_tail, 0)

    @pl.when(e == n_e - 1)
    def _finish():
        out = x_ref[...] + g2_ref[0] * o_ref[...]
        if final:
            out = _rms(out, fg_ref[...])
        o_ref[...] = out


def _combine(idx, y, x_all, mod, final_g, *, n_sets, n, cap, set_col0, row0, n_batch, latent, final):
    E, rows, D = y.shape
    tc = min(2048, n)
    tiles_per_set = n // tc
    chunk = min(256, tc, cap)
    win = min(tc, cap) + chunk
    assert rows >= win and (rows - win) % SUBLANES == 0
    blk0 = row0 // tc
    if latent:
        seg = lambda t: t // tiles_per_set
    else:
        seg = lambda t: n_batch
    grid_spec = pltpu.PrefetchScalarGridSpec(
        num_scalar_prefetch=1,
        grid=(n_sets * tiles_per_set, E),
        in_specs=[
            pl.BlockSpec(memory_space=pl.ANY),
            pl.BlockSpec((tc, D), lambda t, e, idx: (blk0 + t, 0)),
            pl.BlockSpec((1, 1, D), lambda t, e, idx: (seg(t), 0, 5)),
            pl.BlockSpec((1, D), lambda t, e, idx: (0, 0)),
        ],
        out_specs=pl.BlockSpec((tc, D), lambda t, e, idx: (t, 0)),
        scratch_shapes=[pltpu.VMEM((win, D), F32), pltpu.SemaphoreType.DMA],
    )
    return pl.pallas_call(
        functools.partial(_combine_kernel, tc=tc, cap=cap, set_col0=set_col0, row0=row0,
                          tiles_per_set=tiles_per_set, win=win, chunk=chunk, final=final),
        grid_spec=grid_spec,
        out_shape=jax.ShapeDtypeStruct((n_sets * n, D), F32),
        compiler_params=_cparams(("arbitrary", "arbitrary")),
        name="combine_lat" if latent else "combine_ctx",
    )(idx.reshape(-1), y, x_all, mod, final_g)


def _rope_tables(n_lat, n_ctx):
    half = MLA_ROPE // 2
    quarter = half // 2
    inv = ROPE_BASE ** (-jnp.arange(quarter, dtype=F32) / quarter)
    t = jnp.arange(n_lat)
    ang_r = (t // GRID_W).astype(F32)[:, None] * inv
    ang_c = (t % GRID_W).astype(F32)[:, None] * inv
    cos32 = jnp.concatenate([jnp.cos(ang_r), jnp.cos(ang_r), jnp.cos(ang_c), jnp.cos(ang_c)], axis=1)
    sin32 = jnp.concatenate([-jnp.sin(ang_r), jnp.sin(ang_r), -jnp.sin(ang_c), jnp.sin(ang_c)], axis=1)
    pad = HEAD_PAD - MLA_NOPE - MLA_ROPE
    cos_l = jnp.concatenate([jnp.ones((n_lat, MLA_NOPE), F32), cos32, jnp.zeros((n_lat, pad), F32)], axis=1)
    sin_l = jnp.concatenate([jnp.zeros((n_lat, MLA_NOPE), F32), sin32, jnp.zeros((n_lat, pad), F32)], axis=1)
    cos_c = jnp.concatenate([jnp.ones((ROW_TILE, MLA_NOPE + MLA_ROPE), F32), jnp.zeros((ROW_TILE, pad), F32)], axis=1)
    sin_c = jnp.zeros((ROW_TILE, HEAD_PAD), F32)
    return jnp.concatenate([cos_l, cos_c]), jnp.concatenate([sin_l, sin_c])


def _rope_partner():
    half = MLA_ROPE // 2
    quarter = half // 2
    p = jnp.arange(MLA_ROPE)
    return (p // half) * half + (p % half + quarter) % half


def _layer_weights(l, w_in, mla_q_norm, mla_kv_norm, w_uq, w_ukv, w_pool, w_br_mla, w_br_na, w_br_pool, w_out,
                   w_router):
    D = w_in.shape[1]
    offs = [0]
    for s in IN_SIZES:
        offs.append(offs[-1] + s)
    wi = w_in[l]
    w_cq, w_ckv, w_kr, w_na, w_pl = [wi[:, offs[k]:offs[k + 1]] for k in range(5)]
    w_gt = wi[:, offs[5]:]
    partner = _rope_partner()
    zl = jnp.zeros((D, MLA_NOPE), F32)
    zr = jnp.zeros((D, HEAD_PAD - MLA_NOPE - MLA_ROPE), F32)
    w_kr_p = jnp.concatenate([zl, w_kr, zr], axis=1)
    w_krs_p = jnp.concatenate([zl, w_kr[:, partner], zr], axis=1)
    wb = jnp.concatenate([w_cq, w_ckv, w_kr_p, w_krs_p, w_na, w_pl], axis=1).astype(BF16)

    dq = MLA_NOPE + MLA_ROPE
    uq = w_uq[l].reshape(MLA_Q_RANK, MLA_HEADS, dq)
    zq = jnp.zeros((MLA_Q_RANK, MLA_HEADS, HEAD_PAD - dq), F32)
    wqm = jnp.concatenate([uq, zq], axis=2).reshape(MLA_Q_RANK, MLA_HEADS * HEAD_PAD).astype(BF16)
    wqs = jnp.concatenate([jnp.zeros((MLA_Q_RANK, MLA_HEADS, MLA_NOPE), F32), uq[:, :, MLA_NOPE:][:, :, partner], zq],
                          axis=2).reshape(MLA_Q_RANK, MLA_HEADS * HEAD_PAD).astype(BF16)
    ukv = w_ukv[l].reshape(MLA_KV_RANK, MLA_HEADS, MLA_NOPE + MLA_V)
    wk = jnp.concatenate([ukv[:, :, :MLA_NOPE], jnp.zeros((MLA_KV_RANK, MLA_HEADS, HEAD_PAD - MLA_NOPE), F32)],
                         axis=2).reshape(MLA_KV_RANK, MLA_HEADS * HEAD_PAD).astype(BF16)
    wv = jnp.concatenate([ukv[:, :, MLA_NOPE:], jnp.zeros((MLA_KV_RANK, MLA_HEADS, V_ROWS - MLA_V), F32)],
                         axis=2).reshape(MLA_KV_RANK, MLA_HEADS * V_ROWS).astype(BF16)
    vone = jnp.tile(jnp.concatenate([jnp.zeros((MLA_V,), F32), jnp.ones((V_ROWS - MLA_V,), F32)]), MLA_HEADS)[None]

    w_bd = jnp.zeros((POOL_WIDTH, POOL_WIDTH), F32)
    for g in range(len(POOL_WINDOWS)):
        w_bd = w_bd.at[g * POOL_GROUP:(g + 1) * POOL_GROUP, g * POOL_GROUP:(g + 1) * POOL_GROUP].set(w_pool[l, g])
    return dict(
        wb=wb, gq=mla_q_norm[l][None], gkv=mla_kv_norm[l][None], wqm=wqm, wqs=wqs, wk=wk, wv=wv, vone=vone,
        w_bd=w_bd.astype(BF16), wg=w_gt.astype(BF16), wa=w_br_mla[l].astype(BF16), wbn=w_br_na[l].astype(BF16),
        wp=w_br_pool[l].astype(BF16), wo=w_out[l].astype(BF16), wrt=w_router[l].T.astype(BF16))


def kernel(x, c, ctx, c_ctx, norm1_g, norm2_g, w_ada, b_ada, w_in, mla_q_norm, mla_kv_norm, w_uq, w_ukv, na_rpb, w_pool, pool_scale, w_br_mla, w_br_na, w_br_pool, w_out, w_router, w_gate, w_up, w_down, final_g):
    B, N, D = x.shape
    Cx = ctx.shape[1]
    L = w_in.shape[0]
    E = N_EXPERTS
    T = ROW_TILE
    assert Cx == T and N % (NA_KROWS * GRID_W) == 0 and B + 1 <= 8
    tiles_per_batch = N // T
    n_lat_tiles = B * tiles_per_batch
    cap_l = EC_CAPACITY * N // E
    cap_c = EC_CAPACITY * Cx // E

    x_all = jnp.concatenate([x.reshape(B * N, D), ctx.reshape(B * Cx, D)], axis=0)
    cc = jnp.concatenate([c, c_ctx[None], jnp.zeros((8 - B - 1, D), F32)], axis=0)
    mod_all = _ada(cc, w_ada, b_ada).reshape(L, 8, 1, 6 * D)
    cos_t, sin_t = _rope_tables(N, Cx)
    dims = dict(n_batch=B, n_lat=N, n_ctx=Cx)

    out = None
    for l in range(L):
        last = l == L - 1
        wts = _layer_weights(l, w_in, mla_q_norm, mla_kv_norm, w_uq, w_ukv, w_pool, w_br_mla, w_br_na, w_br_pool,
                             w_out, w_router)
        mod = mod_all[l]
        g1 = norm1_g[l][None]
        g2 = norm2_g[l][None]
        q_all, k_all, vt_all, na_all, pool_u = _inproj(x_all, mod, g1, wts, cos_t, sin_t, n_lat_tiles=n_lat_tiles,
                                                      tiles_per_batch=tiles_per_batch, n_batch=B)
        a_l = _flash(q_all, k_all, vt_all,latent=True, **dims)
        b_l = _na_lat(na_all, _na_bias(na_rpb[l], N // GRID_W), **dims)
        p_all = _pool(pool_u, wts["w_bd"], pool_scale[l][None], **dims)
        if not last:
            a_c = _flash(q_all, k_all, vt_all,latent=False, **dims)
            b_c = _na_ctx(na_all, **dims)
        else:
            a_c = jnp.zeros((B * Cx, MLA_W), BF16)
            b_c = jnp.zeros((B * Cx, NA_W), BF16)
        a_all = jnp.concatenate([a_l, a_c], axis=0)
        b_all = jnp.concatenate([b_l, b_c], axis=0)
        xn_all, h2_all, aff_t = _merge(x_all, mod, g1, g2, a_all, b_all, p_all, wts, tiles_per_batch=tiles_per_batch,
                                       n_batch=B)

        idx, gate = _topk(aff_t, n_sets=B, n=N, cap=cap_l, row0=0)
        if not last:
            idx_c, gate_c = _topk(aff_t, n_sets=B, n=Cx, cap=cap_c, row0=B * N)
            idx = jnp.concatenate([idx, idx_c], axis=1)
            gate = jnp.concatenate([gate, gate_c], axis=1)
        y = _moe(idx, gate, h2_all, w_gate[l], w_up[l], w_down[l])
        fg = final_g[None]
        xl_new = _combine(idx, y, xn_all, mod, fg, n_sets=B, n=N, cap=cap_l, set_col0=0, row0=0, n_batch=B,
                          latent=True, final=last)
        if last:
            out = xl_new.reshape(B, N, D)
        else:
            xc_new = _combine(idx, y, xn_all, mod, fg, n_sets=B, n=Cx, cap=cap_c, set_col0=B * cap_l, row0=B * N,
                              n_batch=B, latent=False, final=False)
            x_all = jnp.concatenate([xl_new, xc_new], axis=0)
    return out
```

```python
import functools

import numpy as np
import jax
import jax.numpy as jnp
from jax import lax
from jax.experimental import pallas as pl
from jax.experimental.pallas import tpu as pltpu

F32 = jnp.float32
BF16 = jnp.bfloat16
I32 = jnp.int32

GRID_W = 64
MLA_HEADS = 8
MLA_Q_RANK = 384
MLA_KV_RANK = 256
MLA_NOPE = 64
MLA_ROPE = 32
MLA_V = 64
MLA_W = MLA_HEADS * MLA_V
NA_HEADS = 4
NA_HEAD_DIM = 64
NA_W = NA_HEADS * NA_HEAD_DIM
NA_WIN_R = 8
NA_WIN_C = 16
POOL_WINDOWS = (2, 4, 8, 16)
POOL_GROUP = 64
POOL_WIDTH = POOL_GROUP * len(POOL_WINDOWS)
N_EXPERTS = 16
EC_CAPACITY = 2
N_BRANCH = 3
ROPE_BASE = 10000.0
EPS = 1e-6
IN_SIZES = (MLA_Q_RANK, MLA_KV_RANK, MLA_ROPE, 3 * NA_W, POOL_WIDTH)

LANES = 128
SUBLANES = 8
HEAD_PAD = 128
V_ROWS = MLA_V + 16
QK_EXP2_SCALE = (MLA_NOPE + MLA_ROPE) ** -0.5 * 1.4426950408889634
ROW_TILE = 256
POOL_HALO = 8
GATHER_UNROLL = 8
COMBINE_UNROLL = 4
MASK_VALUE = -1e30
VMEM_LIMIT = 56 * 1024 * 1024


def _cparams(sem, vmem=VMEM_LIMIT):
    return pltpu.CompilerParams(dimension_semantics=sem, vmem_limit_bytes=vmem)


def _rms(x, g):
    return x * lax.rsqrt(jnp.mean(x * x, axis=-1, keepdims=True) + EPS) * g


def _dot(a, b):
    return jnp.dot(a, b, preferred_element_type=F32)


def _dot_nt(a, b):
    return lax.dot_general(a, b, (((1,), (1,)), ((), ())), preferred_element_type=F32)


def _ada_kernel(c_ref, w_ref, b_ref, o_ref):
    c = c_ref[...]
    s = c * jax.nn.sigmoid(c)
    o_ref[0] = _dot(s.astype(BF16), w_ref[0].astype(BF16)) + b_ref[0]


def _ada(cc, w_ada, b_ada):
    L, D, D6 = w_ada.shape
    tn = 1536
    return pl.pallas_call(
        _ada_kernel,
        grid=(L, D6 // tn),
        in_specs=[
            pl.BlockSpec((8, D), lambda l, j: (0, 0)),
            pl.BlockSpec((1, D, tn), lambda l, j: (l, 0, j)),
            pl.BlockSpec((1, 1, tn), lambda l, j: (l, 0, j)),
        ],
        out_specs=pl.BlockSpec((1, 8, tn), lambda l, j: (l, 0, j)),
        out_shape=jax.ShapeDtypeStruct((L, 8, D6), F32),
        compiler_params=_cparams(("parallel", "parallel")),
        name="ada",
    )(cc, w_ada, b_ada.reshape(L, 1, D6))


def _inproj_kernel(x_ref, sh_ref, sc_ref, g1_ref, wb_ref, gq_ref, gkv_ref, wqm_ref, wqs_ref, wk_ref, wv_ref,
                   vone_ref, cos_ref, sin_ref, q_ref, k_ref, vt_ref, na_ref, pool_ref):
    x = x_ref[...]
    h = _rms(x, g1_ref[...]) * (1.0 + sc_ref[0]) + sh_ref[0]
    z = _dot(h.astype(BF16), wb_ref[...])
    o = 0
    cq = z[:, o:o + MLA_Q_RANK]; o += MLA_Q_RANK
    ckv = z[:, o:o + MLA_KV_RANK]; o += MLA_KV_RANK
    kr = z[:, o:o + HEAD_PAD]; o += HEAD_PAD
    krs = z[:, o:o + HEAD_PAD]; o += HEAD_PAD
    na_ref[...] = z[:, o:o + 3 * NA_W].astype(BF16); o += 3 * NA_W
    pool_ref[...] = z[:, o:o + POOL_WIDTH]
    cqn = _rms(cq, gq_ref[...]).astype(BF16)
    ckvn = _rms(ckv, gkv_ref[...]).astype(BF16)
    cos = cos_ref[...]
    sin = sin_ref[...]
    cos8 = jnp.concatenate([cos] * MLA_HEADS, axis=1)
    sin8 = jnp.concatenate([sin] * MLA_HEADS, axis=1)
    q = _dot(cqn, wqm_ref[...]) * cos8 + _dot(cqn, wqs_ref[...]) * sin8
    q_ref[...] = (q * QK_EXP2_SCALE).astype(BF16)
    krot = kr * cos + krs * sin
    k = _dot(ckvn, wk_ref[...]) + jnp.concatenate([krot] * MLA_HEADS, axis=1)
    k_ref[...] = k.astype(BF16)
    vt_ref[...] = (_dot(ckvn, wv_ref[...]) + vone_ref[...]).T.astype(BF16)


def _inproj(x_all, mod, g1, wts, cos_t, sin_t, *, n_lat_tiles, tiles_per_batch, n_batch):
    R, D = x_all.shape
    T = ROW_TILE
    nb = wts["wb"].shape[1]

    def seg(i):
        return jnp.minimum(i // tiles_per_batch, n_batch)

    def tab(i):
        return jnp.where(i < n_lat_tiles, i % tiles_per_batch, tiles_per_batch)

    full = lambda a: pl.BlockSpec(a.shape, lambda i: (0,) * a.ndim)
    outs = [
        jax.ShapeDtypeStruct((R, MLA_HEADS * HEAD_PAD), BF16),
        jax.ShapeDtypeStruct((R, MLA_HEADS * HEAD_PAD), BF16),
        jax.ShapeDtypeStruct((MLA_HEADS * V_ROWS, R), BF16),
        jax.ShapeDtypeStruct((R, 3 * NA_W), BF16),
        jax.ShapeDtypeStruct((R, POOL_WIDTH), F32),
    ]
    row_major = lambda s: pl.BlockSpec((T, s.shape[1]), lambda i: (i, 0))
    out_specs = [row_major(outs[0]), row_major(outs[1]), pl.BlockSpec((MLA_HEADS * V_ROWS, T), lambda i: (0, i)),
                 row_major(outs[3]), row_major(outs[4])]
    return pl.pallas_call(
        _inproj_kernel,
        grid=(R // T,),
        in_specs=[
            pl.BlockSpec((T, D), lambda i: (i, 0)),
            pl.BlockSpec((1, 1, D), lambda i: (seg(i), 0, 0)),
            pl.BlockSpec((1, 1, D), lambda i: (seg(i), 0, 1)),
            full(g1), full(wts["wb"]), full(wts["gq"]), full(wts["gkv"]), full(wts["wqm"]), full(wts["wqs"]),
            full(wts["wk"]), full(wts["wv"]), full(wts["vone"]),
            pl.BlockSpec((T, HEAD_PAD), lambda i: (tab(i), 0)),
            pl.BlockSpec((T, HEAD_PAD), lambda i: (tab(i), 0)),
        ],
        out_specs=out_specs,
        out_shape=outs,
        compiler_params=_cparams(("parallel",)),
        name="inproj",
    )(x_all, mod, mod, g1, wts["wb"], wts["gq"], wts["gkv"], wts["wqm"], wts["wqs"], wts["wk"], wts["wv"],
      wts["vone"], cos_t, sin_t)


def _flash_kernel(*refs, tk, n_chunks):
    if n_chunks:
        q_ref, kc_ref, vtc_ref, kl_ref, vtl_ref, o_ref, m_ref, acc_ref, s_ref = refs
    else:
        q_ref, kc_ref, vtc_ref, o_ref, m_ref, acc_ref = refs
    m_ref[...] = jnp.full(m_ref.shape, -jnp.inf, F32)
    acc_ref[...] = jnp.zeros(acc_ref.shape, F32)

    def scores(h, k2):
        return _dot_nt(k2[:, h * HEAD_PAD:(h + 1) * HEAD_PAD], q_ref[:, h * HEAD_PAD:(h + 1) * HEAD_PAD])

    def softmax_pv(h, s, vt):
        m_old = m_ref[h]
        m_new = jnp.maximum(m_old, jnp.max(s, axis=0, keepdims=True))
        p = jnp.exp2(s - m_new)
        acc_ref[h] = jnp.exp2(m_old - m_new) * acc_ref[h] + _dot(vt[h * V_ROWS:(h + 1) * V_ROWS, :], p.astype(BF16))
        m_ref[h] = m_new

    kc = kc_ref[...]
    vtc = vtc_ref[...]
    for h in range(2):
        softmax_pv(h, scores(h, kc), vtc)

    if n_chunks:
        def kchunk(i):
            return kl_ref[pl.ds(pl.multiple_of(i * tk, tk), tk), :]

        def vchunk(i):
            return vtl_ref[:, pl.ds(pl.multiple_of(i * tk, tk), tk)]

        def qk(slot, k2):
            for h in range(2):
                s_ref[slot, h] = scores(h, k2)

        def spv(slot, vt):
            for h in range(2):
                softmax_pv(h, s_ref[slot, h], vt)

        qk(0, kchunk(0))

        def body(i2, carry):
            c0 = 2 * i2
            qk(1, kchunk(c0 + 1))
            spv(0, vchunk(c0))
            qk(0, kchunk(jnp.minimum(c0 + 2, n_chunks - 1)))
            spv(1, vchunk(c0 + 1))
            return carry
        lax.fori_loop(0, n_chunks // 2, body, 0)
    a0 = acc_ref[0]
    a1 = acc_ref[1]
    o_t = jnp.concatenate([a0[0:MLA_V] / a0[MLA_V:MLA_V + 1], a1[0:MLA_V] / a1[MLA_V:MLA_V + 1]], axis=0)
    o_ref[...] = o_t.T.astype(BF16)


def _flash(q_all, k_all, vt_all, *, n_batch, n_lat, n_ctx, latent):
    ctx_blk0 = (n_batch * n_lat) // n_ctx
    n_pairs = MLA_HEADS // 2
    if latent:
        tq = min(1024, n_lat)
        tk = min(512, n_lat // 2)
        nq = n_lat // tq
        q_map = lambda b, p, i: (b * nq + i, p)
        n_rows = n_batch * n_lat
        n_chunks = n_lat // tk
        assert n_chunks % 2 == 0
    else:
        tq, nq, tk, n_chunks = n_ctx, 1, n_ctx, 0
        q_map = lambda b, p, i: (ctx_blk0 + b, p)
        n_rows = n_batch * n_ctx
    in_specs = [
        pl.BlockSpec((tq, 2 * HEAD_PAD), q_map),
        pl.BlockSpec((n_ctx, 2 * HEAD_PAD), lambda b, p, i: (ctx_blk0 + b, p)),
        pl.BlockSpec((2 * V_ROWS, n_ctx), lambda b, p, i: (p, ctx_blk0 + b)),
    ]
    args = [q_all, k_all, vt_all]
    scratch = [pltpu.VMEM((2, 1, tq), F32), pltpu.VMEM((2, V_ROWS, tq), F32)]
    if latent:
        in_specs += [
            pl.BlockSpec((n_lat, 2 * HEAD_PAD), lambda b, p, i: (b, p)),
            pl.BlockSpec((2 * V_ROWS, n_lat), lambda b, p, i: (p, b)),
        ]
        args += [k_all, vt_all]
        scratch.append(pltpu.VMEM((2, 2, tk, tq), F32))
    return pl.pallas_call(
        functools.partial(_flash_kernel, tk=tk, n_chunks=n_chunks),
        grid=(n_batch, n_pairs, nq),
        in_specs=in_specs,
        out_specs=pl.BlockSpec((tq, 2 * MLA_V), lambda b, p, i: (b * nq + i, p)),
        out_shape=jax.ShapeDtypeStruct((n_rows, MLA_W), BF16),
        scratch_shapes=scratch,
        compiler_params=_cparams(("parallel", "parallel", "arbitrary")),
        name="flash_lat" if latent else "flash_ctx",
    )(*args)


NA_QROWS = 8
NA_KROWS = 16


def _na_softmax_pv(parts):
    m = None
    for s, _ in parts:
        mi = jnp.max(s, axis=1, keepdims=True)
        m = mi if m is None else jnp.maximum(m, mi)
    l = None
    o = None
    for s, v in parts:
        p = jnp.exp(s - m)
        li = jnp.sum(p, axis=1, keepdims=True)
        oi = _dot(p.astype(BF16), v)
        l = li if l is None else l + li
        o = oi if o is None else o + oi
    return o / l


def _na_lat_kernel(q_ref, k_ref, v_ref, kc_ref, vc_ref, bias_ref, o_ref, *, rows, scale):
    i = pl.program_id(1)
    s0 = jnp.clip(NA_QROWS * i - NA_WIN_R // 2, 0, rows - NA_KROWS) * GRID_W
    s0 = pl.multiple_of(s0, 256)
    nk = NA_KROWS * GRID_W
    lo = lax.broadcasted_iota(I32, (1, LANES), 1) < NA_HEAD_DIM
    outs = []
    for pr in range(NA_HEADS // 2):
        cs = slice(pr * LANES, (pr + 1) * LANES)
        qp = q_ref[:, cs]
        kw = k_ref[pl.ds(s0, nk), cs]
        vw = v_ref[pl.ds(s0, nk), cs]
        kc = kc_ref[:, cs]
        vc = vc_ref[:, cs]
        o_pair = None
        for hh in range(2):
            msk = lo if hh == 0 else jnp.logical_not(lo)
            qh = jnp.where(msk, qp, jnp.zeros_like(qp))
            s_lat = _dot_nt(qh, kw) * scale + bias_ref[0, 2 * pr + hh]
            s_ctx = _dot_nt(qh, kc) * scale
            o = _na_softmax_pv([(s_lat, vw), (s_ctx, vc)])
            o_pair = o if hh == 0 else jnp.where(lo, o_pair, o)
        outs.append(o_pair)
    o_ref[...] = jnp.concatenate(outs, axis=1).astype(BF16)


def _na_lat(na_all, bias, *, n_batch, n_lat, n_ctx):
    rows = n_lat // GRID_W
    nblk = rows // NA_QROWS
    tq = NA_QROWS * GRID_W
    ctx_blk0 = (n_batch * n_lat) // n_ctx
    scale = NA_HEAD_DIM ** -0.5

    def variant(i):
        return jnp.where(i == 0, 0, jnp.where(i == nblk - 1, 2, 1))

    return pl.pallas_call(
        functools.partial(_na_lat_kernel, rows=rows, scale=scale),
        grid=(n_batch, nblk),
        in_specs=[
            pl.BlockSpec((tq, NA_W), lambda b, i: (b * nblk + i, 0)),
            pl.BlockSpec((n_lat, NA_W), lambda b, i: (b, 1)),
            pl.BlockSpec((n_lat, NA_W), lambda b, i: (b, 2)),
            pl.BlockSpec((n_ctx, NA_W), lambda b, i: (ctx_blk0 + b, 1)),
            pl.BlockSpec((n_ctx, NA_W), lambda b, i: (ctx_blk0 + b, 2)),
            pl.BlockSpec((1, NA_HEADS, tq, NA_KROWS * GRID_W), lambda b, i: (variant(i), 0, 0, 0)),
        ],
        out_specs=pl.BlockSpec((tq, NA_W), lambda b, i: (b * nblk + i, 0)),
        out_shape=jax.ShapeDtypeStruct((n_batch * n_lat, NA_W), BF16),
        compiler_params=_cparams(("parallel", "arbitrary")),
        name="na_lat",
    )(na_all, na_all, na_all, na_all, na_all, bias)


def _na_ctx_kernel(q_ref, k_ref, v_ref, o_ref, *, scale):
    lo = lax.broadcasted_iota(I32, (1, LANES), 1) < NA_HEAD_DIM
    outs = []
    for pr in range(NA_HEADS // 2):
        cs = slice(pr * LANES, (pr + 1) * LANES)
        qp = q_ref[:, cs]
        kp = k_ref[:, cs]
        vp = v_ref[:, cs]
        o_pair = None
        for hh in range(2):
            msk = lo if hh == 0 else jnp.logical_not(lo)
            qh = jnp.where(msk, qp, jnp.zeros_like(qp))
            o = _na_softmax_pv([(_dot_nt(qh, kp) * scale, vp)])
            o_pair = o if hh == 0 else jnp.where(lo, o_pair, o)
        outs.append(o_pair)
    o_ref[...] = jnp.concatenate(outs, axis=1).astype(BF16)


def _na_ctx(na_all, *, n_batch, n_lat, n_ctx):
    ctx_blk0 = (n_batch * n_lat) // n_ctx
    return pl.pallas_call(
        functools.partial(_na_ctx_kernel, scale=NA_HEAD_DIM ** -0.5),
        grid=(n_batch,),
        in_specs=[pl.BlockSpec((n_ctx, NA_W), lambda b, j=j: (ctx_blk0 + b, j)) for j in range(3)],
        out_specs=pl.BlockSpec((n_ctx, NA_W), lambda b: (b, 0)),
        out_shape=jax.ShapeDtypeStruct((n_batch * n_ctx, NA_W), BF16),
        compiler_params=_cparams(("parallel",)),
        name="na_ctx",
    )(na_all, na_all, na_all)


def _na_bias(rpb, rows):
    nblk = rows // NA_QROWS
    qc = np.arange(GRID_W)
    kc = np.arange(GRID_W)
    start_c = np.clip(qc - NA_WIN_C // 2, 0, GRID_W - NA_WIN_C)
    valid_c = (kc[None, :] >= start_c[:, None]) & (kc[None, :] < start_c[:, None] + NA_WIN_C)
    dc = np.clip(kc[None, :] - qc[:, None], -(NA_WIN_C - 1), NA_WIN_C - 1) + (NA_WIN_C - 1)
    onehot = (dc.reshape(1, -1) == np.arange(2 * NA_WIN_C - 1)[:, None]).astype(np.float32)
    by_dr = jnp.einsum("hab,bn->han", rpb.astype(F32), jnp.asarray(onehot), precision=lax.Precision.HIGHEST)
    by_dr = by_dr.reshape(NA_HEADS, 2 * NA_WIN_R - 1, GRID_W, GRID_W)
    tables = []
    for i in (0, min(1, nblk - 1), nblk - 1):
        s0 = min(max(NA_QROWS * i - NA_WIN_R // 2, 0), rows - NA_KROWS)
        qrow = NA_QROWS * i + np.arange(NA_QROWS)
        krow = s0 + np.arange(NA_KROWS)
        start_r = np.clip(qrow - NA_WIN_R // 2, 0, rows - NA_WIN_R)
        valid_r = (krow[None, :] >= start_r[:, None]) & (krow[None, :] < start_r[:, None] + NA_WIN_R)
        dr = np.clip(krow[None, :] - qrow[:, None] + (NA_WIN_R - 1), 0, 2 * NA_WIN_R - 2)
        b = jnp.transpose(by_dr[:, dr], (0, 1, 3, 2, 4))
        valid = valid_r[:, None, :, None] & valid_c[None, :, None, :]
        b = jnp.where(jnp.asarray(valid)[None], b, MASK_VALUE)
        tables.append(b.reshape(NA_HEADS, NA_QROWS * GRID_W, NA_KROWS * GRID_W))
    return jnp.stack(tables)


def _pool_kernel(prev_ref, cur_ref, next_ref, w_ref, ps_ref, o_ref, scr, *, tiles_per_batch, n_lat_tiles, n_lat, n_ctx):
    i = pl.program_id(0)
    T = ROW_TILE
    H = POOL_HALO
    is_lat = i < n_lat_tiles
    tile_in_seq = jnp.where(is_lat, i % tiles_per_batch, 0)
    n_seq = jnp.where(is_lat, n_lat, n_ctx)
    first = tile_in_seq == 0
    last = jnp.where(is_lat, tile_in_seq == tiles_per_batch - 1, True)
    u = cur_ref[...]
    scr[0:H, :] = jnp.where(first, 0.0, prev_ref[...])
    scr[H:H + T, :] = u
    scr[H + T:H + T + H, :] = jnp.where(last, 0.0, next_ref[...])

    def win(lo, hi):
        acc = None
        for j in range(lo, hi):
            piece = scr[H + j:H + j + T, :]
            acc = piece if acc is None else acc + piece
        return acc

    t = (tile_in_seq * T + lax.broadcasted_iota(I32, (T, 1), 0))
    lane = lax.broadcasted_iota(I32, (1, POOL_WIDTH), 1)
    sums = {}
    acc = win(-1, 1)
    sums[2] = acc
    acc = acc + win(-2, -1) + win(1, 2)
    sums[4] = acc
    acc = acc + win(-4, -2) + win(2, 4)
    sums[8] = acc
    acc = acc + win(-8, -4) + win(4, 8)
    sums[16] = acc
    mean = None
    for gi, w in enumerate(POOL_WINDOWS):
        cnt = (jnp.minimum(t + w // 2, n_seq) - jnp.maximum(t - w // 2, 0)).astype(F32)
        mg = sums[w] / cnt
        mean = mg if mean is None else jnp.where(lane < gi * POOL_GROUP, mean, mg)
    dlt = mean - u
    o_ref[...] = (_dot(dlt.astype(BF16), w_ref[...]) * ps_ref[...]).astype(BF16)


def _pool(u_all, w_bd, pool_scale, *, n_batch, n_lat, n_ctx):
    R = u_all.shape[0]
    T = ROW_TILE
    H = POOL_HALO
    assert n_ctx == T
    tiles_per_batch = n_lat // T
    n_lat_tiles = n_batch * tiles_per_batch
    nb8 = R // H
    return pl.pallas_call(
        functools.partial(_pool_kernel, tiles_per_batch=tiles_per_batch, n_lat_tiles=n_lat_tiles, n_lat=n_lat,
                          n_ctx=n_ctx),
        grid=(R // T,),
        in_specs=[
            pl.BlockSpec((H, POOL_WIDTH), lambda i: (jnp.maximum(i * (T // H) - 1, 0), 0)),
            pl.BlockSpec((T, POOL_WIDTH), lambda i: (i, 0)),
            pl.BlockSpec((H, POOL_WIDTH), lambda i: (jnp.minimum((i + 1) * (T // H), nb8 - 1), 0)),
            pl.BlockSpec((POOL_WIDTH, POOL_WIDTH), lambda i: (0, 0)),
            pl.BlockSpec((1, POOL_WIDTH), lambda i: (0, 0)),
        ],
        out_specs=pl.BlockSpec((T, POOL_WIDTH), lambda i: (i, 0)),
        out_shape=jax.ShapeDtypeStruct((R, POOL_WIDTH), BF16),
        scratch_shapes=[pltpu.VMEM((T + 2 * H, POOL_WIDTH), F32)],
        compiler_params=_cparams(("parallel",)),
        name="pool",
    )(u_all, u_all, u_all, w_bd, pool_scale)


def _merge_kernel(x_ref, sh1_ref, sc1_ref, gt1_ref, sh2_ref, sc2_ref, g1_ref, g2_ref, a_ref, b_ref, p_ref,
                  wg_ref, wa_ref, wb_ref, wp_ref, wo_ref, wrt_ref, xn_ref, h2_ref, afft_ref):
    D = x_ref.shape[1]
    x = x_ref[...]
    h = _rms(x, g1_ref[...]) * (1.0 + sc1_ref[0]) + sh1_ref[0]
    g = jax.nn.sigmoid(_dot(h.astype(BF16), wg_ref[...]))
    m = (g[:, 0:D] * _dot(a_ref[...], wa_ref[...]) + g[:, D:2 * D] * _dot(b_ref[...], wb_ref[...])
         + g[:, 2 * D:3 * D] * _dot(p_ref[...], wp_ref[...]))
    xn = x + gt1_ref[0] * _dot(m.astype(BF16), wo_ref[...])
    xn_ref[...] = xn
    h2 = _rms(xn, g2_ref[...]) * (1.0 + sc2_ref[0]) + sh2_ref[0]
    h2_ref[...] = h2
    logit_t = _dot_nt(wrt_ref[...], h2.astype(BF16))
    ex_t = jnp.exp(logit_t - jnp.max(logit_t, axis=0, keepdims=True))
    afft_ref[...] = ex_t / jnp.sum(ex_t, axis=0, keepdims=True)


def _merge(x_all, mod, g1, g2, a_all, b_all, p_all, wts, *, tiles_per_batch, n_batch):
    R, D = x_all.shape
    T = ROW_TILE

    def seg(i):
        return jnp.minimum(i // tiles_per_batch, n_batch)

    modspec = lambda k: pl.BlockSpec((1, 1, D), lambda i: (seg(i), 0, k))
    full = lambda a: pl.BlockSpec(a.shape, lambda i: (0,) * a.ndim)
    row = lambda w: pl.BlockSpec((T, w), lambda i: (i, 0))
    outs = [
        jax.ShapeDtypeStruct((R, D), F32),
        jax.ShapeDtypeStruct((R, D), F32),
        jax.ShapeDtypeStruct((N_EXPERTS, R), F32),
    ]
    wnames = ("wg", "wa", "wbn", "wp", "wo", "wrt")
    return pl.pallas_call(
        _merge_kernel,
        grid=(R // T,),
        in_specs=[row(D), modspec(0), modspec(1), modspec(2), modspec(3), modspec(4), full(g1), full(g2),
                  row(MLA_W), row(NA_W), row(POOL_WIDTH)] + [full(wts[n]) for n in wnames],
        out_specs=[row(D), row(D), pl.BlockSpec((N_EXPERTS, T), lambda i: (0, i))],
        out_shape=outs,
        compiler_params=_cparams(("parallel",)),
        name="merge",
    )(x_all, mod, mod, mod, mod, mod, g1, g2, a_all, b_all, p_all, *[wts[n] for n in wnames])


def _onehot(mask):
    return jnp.where(mask, 1.0, 0.0).astype(BF16)


def _topk_kernel(aff_ref, idx_ref, gate_ref, wc_ref, affc_ref, cend_ref, off_ref, *, n, cap, row0, jt):
    s = pl.program_id(0)
    nc = n // LANES
    aff = aff_ref[...]
    bits = lax.bitcast_convert_type(aff, I32)

    def search(it, thr):
        cand = thr | jnp.left_shift(jnp.int32(1), 30 - it)
        cnt = jnp.sum((bits >= cand).astype(I32), axis=1, keepdims=True)
        return jnp.where(cnt >= cap, cand, thr)

    thr = lax.fori_loop(0, 31, search, jnp.zeros((N_EXPERTS, 1), I32))
    gt = bits > thr
    eq = bits == thr
    need = (cap - jnp.sum(gt.astype(I32), axis=1, keepdims=True)).astype(F32)
    lane = lax.broadcasted_iota(I32, (1, LANES), 1)
    upper = _onehot(lax.broadcasted_iota(I32, (LANES, LANES), 0) <= lax.broadcasted_iota(I32, (LANES, LANES), 1))

    wc_ref[:, nc:, :] = jnp.zeros((N_EXPERTS, LANES - nc, LANES), F32)
    affc_ref[:, nc:, :] = jnp.zeros((N_EXPERTS, LANES - nc, LANES), F32)
    ties_before = jnp.zeros((N_EXPERTS, 1), F32)
    tot = jnp.zeros((N_EXPERTS, LANES), F32)
    for ch in range(nc):
        sl = slice(ch * LANES, (ch + 1) * LANES)
        tie_rank = _dot(_onehot(eq[:, sl]), upper) + ties_before
        ties_before = tie_rank[:, LANES - 1:LANES]
        sel = gt[:, sl] | (eq[:, sl] & (tie_rank <= need))
        w = _dot(_onehot(sel), upper)
        wc_ref[:, ch, :] = w
        affc_ref[:, ch, :] = aff[:, sl]
        tot = jnp.where(lane == ch, w[:, LANES - 1:LANES], tot)
    cend = _dot(tot.astype(BF16), upper)
    cend_ref[...] = cend
    off_ref[...] = cend - tot
    base = row0 + s * n

    def per_expert(e, carry):
        wc = wc_ref[e].astype(BF16)
        a = affc_ref[e]
        hi = a.astype(BF16)
        rest = a - hi.astype(F32)
        mid = rest.astype(BF16)
        lo = (rest - mid.astype(F32)).astype(BF16)
        cend_e = cend_ref[pl.ds(e, 1), :]
        off_e = off_ref[pl.ds(e, 1), :]
        for t in range(cap // jt):
            rows = slice(t * jt, (t + 1) * jt)
            jcol = (t * jt + lax.broadcasted_iota(I32, (jt, 1), 0)).astype(F32)
            cj = jnp.sum((cend_e <= jcol).astype(I32), axis=1, keepdims=True)
            in_chunk = lane == cj
            g = _onehot(in_chunk)
            rank = jcol - jnp.sum(jnp.where(in_chunk, off_e, 0.0), axis=1, keepdims=True)
            pos = jnp.sum((_dot(g, wc) <= rank).astype(I32), axis=1, keepdims=True)
            idx_ref[0, e, rows, :] = cj * LANES + pos + base
            aff_rows = _dot(g, hi) + _dot(g, mid) + _dot(g, lo)
            gate_ref[0, e, rows, :] = jnp.sum(jnp.where(lane == pos, aff_rows, 0.0), axis=1, keepdims=True)
        return carry

    lax.fori_loop(0, N_EXPERTS, per_expert, 0)


def _topk(aff_t, *, n_sets, n, cap, row0):
    jt = min(128, cap)
    blk0 = row0 // n
    spec = pl.BlockSpec((1, N_EXPERTS, cap, 1), lambda s: (s, 0, 0, 0))
    idx, gate = pl.pallas_call(
        functools.partial(_topk_kernel, n=n, cap=cap, row0=row0, jt=jt),
        grid=(n_sets,),
        in_specs=[pl.BlockSpec((N_EXPERTS, n), lambda s: (0, blk0 + s))],
        out_specs=[spec, spec],
        out_shape=[jax.ShapeDtypeStruct((n_sets, N_EXPERTS, cap, 1), I32),
                   jax.ShapeDtypeStruct((n_sets, N_EXPERTS, cap, 1), F32)],
        scratch_shapes=[pltpu.VMEM((N_EXPERTS, LANES, LANES), F32), pltpu.VMEM((N_EXPERTS, LANES, LANES), F32),
                        pltpu.VMEM((N_EXPERTS, LANES), F32), pltpu.VMEM((N_EXPERTS, LANES), F32)],
        compiler_params=_cparams(("parallel",)),
        name="topk",
    )(aff_t)
    idx = jnp.transpose(idx[..., 0], (1, 0, 2)).reshape(N_EXPERTS, n_sets * cap)
    gate = jnp.transpose(gate, (1, 0, 2, 3)).reshape(N_EXPERTS, n_sets * cap, 1)
    return idx, gate


def _moe_kernel(idx_ref, h_hbm, gate_ref, wg_ref, wu_ref, wd_ref, y_ref, xs, xb, sem, *, rows, nf):
    e = pl.program_id(0)
    f = pl.program_id(1)

    def row_copy(j, t):
        return pltpu.make_async_copy(h_hbm.at[t], xs.at[pl.ds(pl.multiple_of(j * SUBLANES, SUBLANES), SUBLANES), :],
                                     sem)

    def issue_gather(expert):
        base = expert * rows

        def issue(j8, carry):
            for r in range(GATHER_UNROLL):
                j = j8 * GATHER_UNROLL + r
                row_copy(j, idx_ref[base + j]).start()
            return carry
        lax.fori_loop(0, rows // GATHER_UNROLL, issue, 0)

    @pl.when(f == 0)
    def _gather():
        @pl.when(e == 0)
        def _first():
            issue_gather(0)

        def wait(j8, carry):
            for r in range(GATHER_UNROLL):
                row_copy(j8 * GATHER_UNROLL + r, 0).wait()
            return carry
        lax.fori_loop(0, rows // GATHER_UNROLL, wait, 0)
        for s in range(SUBLANES):
            xb[:, s * LANES:(s + 1) * LANES] = xs[pl.ds(s, rows, stride=SUBLANES), :].astype(BF16)

        @pl.when(e + 1 < pl.num_programs(0))
        def _next():
            issue_gather(e + 1)
        y_ref[0] = jnp.zeros(y_ref.shape[1:], F32)

    xv = xb[...]
    a = _dot(xv, wg_ref[0, 0].astype(BF16))
    u = _dot(xv, wu_ref[0, 0].astype(BF16))
    hmid = (a * jax.nn.sigmoid(a)) * u
    y_ref[0] += _dot(hmid.astype(BF16), wd_ref[0, 0].astype(BF16))

    @pl.when(f == nf - 1)
    def _scale():
        y_ref[0] = y_ref[0] * gate_ref[0]


def _moe(idx, gate, h2, w_gate, w_up, w_down, layer):
    E, rows = idx.shape
    _, _, D, F = w_gate.shape
    assert D == SUBLANES * LANES and rows % GATHER_UNROLL == 0
    tf = 256
    nf = F // tf
    grid_spec = pltpu.PrefetchScalarGridSpec(
        num_scalar_prefetch=1,
        grid=(E, nf),
        in_specs=[
            pl.BlockSpec(memory_space=pl.ANY),
            pl.BlockSpec((1, rows, 1), lambda e, f, idx: (e, 0, 0)),
            pl.BlockSpec((1, 1, D, tf), lambda e, f, idx: (layer, e, 0, f)),
            pl.BlockSpec((1, 1, D, tf), lambda e, f, idx: (layer, e, 0, f)),
            pl.BlockSpec((1, 1, tf, D), lambda e, f, idx: (layer, e, f, 0)),
        ],
        out_specs=pl.BlockSpec((1, rows, D), lambda e, f, idx: (e, 0, 0)),
        scratch_shapes=[
            pltpu.VMEM((rows * SUBLANES, LANES), F32),
            pltpu.VMEM((rows, D), BF16),
            pltpu.SemaphoreType.DMA,
        ],
    )
    return pl.pallas_call(
        functools.partial(_moe_kernel, rows=rows, nf=nf),
        grid_spec=grid_spec,
        out_shape=jax.ShapeDtypeStruct((E, rows, D), F32),
        compiler_params=_cparams(("arbitrary", "arbitrary")),
        name="moe",
    )(idx.reshape(-1), h2.reshape(h2.shape[0], SUBLANES, LANES), gate, w_gate, w_up, w_down)


def _combine_kernel(idx_ref, y_hbm, x_ref, g2_ref, fg_ref, o_ref, slab, win_ref, sems, *, tc, cap, set_col0, row0,
                    tiles_per_set, win, chunk, final):
    t_id = pl.program_id(0)
    e = pl.program_id(1)
    n_t = pl.num_programs(0)
    n_e = pl.num_programs(1)
    rows = y_hbm.shape[1]
    step = t_id * n_e + e
    slot = step % 2

    def window(tile, expert):
        col0 = expert * rows + set_col0 + (tile // tiles_per_set) * cap

        def lower_bound(v):
            def body(_, lh):
                lo, hi = lh
                mid = (lo + hi) // 2
                less = idx_ref[col0 + jnp.minimum(mid, cap - 1)] < v
                active = lo < hi
                return (jnp.where(active & less, mid + 1, lo), jnp.where(active & jnp.logical_not(less), mid, hi))
            return lax.fori_loop(0, cap.bit_length() + 1, body, (jnp.int32(0), jnp.int32(cap)))[0]

        first = col0 - expert * rows
        r0 = first + lower_bound(row0 + tile * tc)
        r1 = first + lower_bound(row0 + (tile + 1) * tc)
        w0 = jnp.minimum((r0 // SUBLANES) * SUBLANES, rows - win)
        return r0, r1, w0

    def chunk_copies(expert, r0, r1, w0, sl):
        w0 = pl.multiple_of(w0, SUBLANES)
        return [(pltpu.make_async_copy(y_hbm.at[expert, pl.ds(w0 + k * chunk, chunk), :],
                                       slab.at[sl, pl.ds(k * chunk, chunk), :], sems.at[sl]),
                 (w0 + k * chunk < r1) & (w0 + (k + 1) * chunk > r0)) for k in range(win // chunk)]

    def fetch(tile, expert, sl):
        r0, r1, w0 = window(tile, expert)
        win_ref[sl, 0] = r0
        win_ref[sl, 1] = r1
        win_ref[sl, 2] = w0
        for copy, needed in chunk_copies(expert, r0, r1, w0, sl):
            @pl.when(needed)
            def _start(copy=copy):
                copy.start()

    @pl.when(step == 0)
    def _first():
        fetch(t_id, e, slot)

    r0 = win_ref[slot, 0]
    r1 = win_ref[slot, 1]
    w0 = win_ref[slot, 2]
    for copy, needed in chunk_copies(e, r0, r1, w0, slot):
        @pl.when(needed)
        def _wait(copy=copy):
            copy.wait()

    @pl.when(step + 1 < n_t * n_e)
    def _prefetch():
        wrap = e == n_e - 1
        fetch(jnp.where(wrap, t_id + 1, t_id), jnp.where(wrap, 0, e + 1), 1 - slot)

    @pl.when(e == 0)
    def _init():
        o_ref[...] = jnp.zeros(o_ref.shape, F32)

    row_base = e * rows
    tile_start = row0 + t_id * tc

    def add_rows(j0, n):
        ts = [idx_ref[row_base + j0 + r] - tile_start for r in range(n)]
        cur = [o_ref[pl.ds(t, 1), :] for t in ts]
        new = [slab[slot, pl.ds(j0 + r - w0, 1), :] for r in range(n)]
        for r in range(n):
            o_ref[pl.ds(ts[r], 1), :] = cur[r] + new[r]

    n_groups = (r1 - r0) // COMBINE_UNROLL

    def add_group(g, carry):
        add_rows(r0 + g * COMBINE_UNROLL, COMBINE_UNROLL)
        return carry
    lax.fori_loop(0, n_groups, add_group, 0)

    def add_tail(j, carry):
        add_rows(j, 1)
        return carry
    lax.fori_loop(r0 + n_groups * COMBINE_UNROLL, r1, add_tail, 0)

    @pl.when(e == n_e - 1)
    def _finish():
        out = x_ref[...] + g2_ref[0] * o_ref[...]
        if final:
            out = _rms(out, fg_ref[...])
        o_ref[...] = out


def _combine(idx, y, x_all, mod, final_g, *, n_sets, n, cap, set_col0, row0, n_batch, latent, final):
    E, rows, D = y.shape
    tc = min(2048, n)
    tiles_per_set = n // tc
    chunk = min(256, tc, cap)
    win = min(tc, cap) + chunk
    assert rows >= win and (rows - win) % SUBLANES == 0
    blk0 = row0 // tc
    if latent:
        seg = lambda t: t // tiles_per_set
    else:
        seg = lambda t: n_batch
    grid_spec = pltpu.PrefetchScalarGridSpec(
        num_scalar_prefetch=1,
        grid=(n_sets * tiles_per_set, E),
        in_specs=[
            pl.BlockSpec(memory_space=pl.ANY),
            pl.BlockSpec((tc, D), lambda t, e, idx: (blk0 + t, 0)),
            pl.BlockSpec((1, 1, D), lambda t, e, idx: (seg(t), 0, 5)),
            pl.BlockSpec((1, D), lambda t, e, idx: (0, 0)),
        ],
        out_specs=pl.BlockSpec((tc, D), lambda t, e, idx: (t, 0)),
        scratch_shapes=[pltpu.VMEM((2, win, D), F32), pltpu.SMEM((2, 3), I32), pltpu.SemaphoreType.DMA((2,))],
    )
    return pl.pallas_call(
        functools.partial(_combine_kernel, tc=tc, cap=cap, set_col0=set_col0, row0=row0,
                          tiles_per_set=tiles_per_set, win=win, chunk=chunk, final=final),
        grid_spec=grid_spec,
        out_shape=jax.ShapeDtypeStruct((n_sets * n, D), F32),
        compiler_params=_cparams(("arbitrary", "arbitrary")),
        name="combine_lat" if latent else "combine_ctx",
    )(idx.reshape(-1), y, x_all, mod, final_g)


def _rope_tables(n_lat, n_ctx):
    half = MLA_ROPE // 2
    quarter = half // 2
    inv = ROPE_BASE ** (-jnp.arange(quarter, dtype=F32) / quarter)
    t = jnp.arange(n_lat)
    ang_r = (t // GRID_W).astype(F32)[:, None] * inv
    ang_c = (t % GRID_W).astype(F32)[:, None] * inv
    cos32 = jnp.concatenate([jnp.cos(ang_r), jnp.cos(ang_r), jnp.cos(ang_c), jnp.cos(ang_c)], axis=1)
    sin32 = jnp.concatenate([-jnp.sin(ang_r), jnp.sin(ang_r), -jnp.sin(ang_c), jnp.sin(ang_c)], axis=1)
    pad = HEAD_PAD - MLA_NOPE - MLA_ROPE
    cos_l = jnp.concatenate([jnp.ones((n_lat, MLA_NOPE), F32), cos32, jnp.zeros((n_lat, pad), F32)], axis=1)
    sin_l = jnp.concatenate([jnp.zeros((n_lat, MLA_NOPE), F32), sin32, jnp.zeros((n_lat, pad), F32)], axis=1)
    cos_c = jnp.concatenate([jnp.ones((ROW_TILE, MLA_NOPE + MLA_ROPE), F32), jnp.zeros((ROW_TILE, pad), F32)], axis=1)
    sin_c = jnp.zeros((ROW_TILE, HEAD_PAD), F32)
    return jnp.concatenate([cos_l, cos_c]), jnp.concatenate([sin_l, sin_c])


def _rope_partner():
    half = MLA_ROPE // 2
    quarter = half // 2
    p = jnp.arange(MLA_ROPE)
    return (p // half) * half + (p % half + quarter) % half


def _layer_weights(l, w_in, mla_q_norm, mla_kv_norm, w_uq, w_ukv, w_pool, w_br_mla, w_br_na, w_br_pool, w_out,
                   w_router):
    D = w_in.shape[1]
    offs = [0]
    for s in IN_SIZES:
        offs.append(offs[-1] + s)
    wi = w_in[l]
    w_cq, w_ckv, w_kr, w_na, w_pl = [wi[:, offs[k]:offs[k + 1]] for k in range(5)]
    w_gt = wi[:, offs[5]:]
    partner = _rope_partner()
    zl = jnp.zeros((D, MLA_NOPE), F32)
    zr = jnp.zeros((D, HEAD_PAD - MLA_NOPE - MLA_ROPE), F32)
    w_kr_p = jnp.concatenate([zl, w_kr, zr], axis=1)
    w_krs_p = jnp.concatenate([zl, w_kr[:, partner], zr], axis=1)
    wb = jnp.concatenate([w_cq, w_ckv, w_kr_p, w_krs_p, w_na, w_pl], axis=1).astype(BF16)

    dq = MLA_NOPE + MLA_ROPE
    uq = w_uq[l].reshape(MLA_Q_RANK, MLA_HEADS, dq)
    zq = jnp.zeros((MLA_Q_RANK, MLA_HEADS, HEAD_PAD - dq), F32)
    wqm = jnp.concatenate([uq, zq], axis=2).reshape(MLA_Q_RANK, MLA_HEADS * HEAD_PAD).astype(BF16)
    wqs = jnp.concatenate([jnp.zeros((MLA_Q_RANK, MLA_HEADS, MLA_NOPE), F32), uq[:, :, MLA_NOPE:][:, :, partner], zq],
                          axis=2).reshape(MLA_Q_RANK, MLA_HEADS * HEAD_PAD).astype(BF16)
    ukv = w_ukv[l].reshape(MLA_KV_RANK, MLA_HEADS, MLA_NOPE + MLA_V)
    wk = jnp.concatenate([ukv[:, :, :MLA_NOPE], jnp.zeros((MLA_KV_RANK, MLA_HEADS, HEAD_PAD - MLA_NOPE), F32)],
                         axis=2).reshape(MLA_KV_RANK, MLA_HEADS * HEAD_PAD).astype(BF16)
    wv = jnp.concatenate([ukv[:, :, MLA_NOPE:], jnp.zeros((MLA_KV_RANK, MLA_HEADS, V_ROWS - MLA_V), F32)],
                         axis=2).reshape(MLA_KV_RANK, MLA_HEADS * V_ROWS).astype(BF16)
    vone = jnp.tile(jnp.concatenate([jnp.zeros((MLA_V,), F32), jnp.ones((V_ROWS - MLA_V,), F32)]), MLA_HEADS)[None]

    w_bd = jnp.zeros((POOL_WIDTH, POOL_WIDTH), F32)
    for g in range(len(POOL_WINDOWS)):
        w_bd = w_bd.at[g * POOL_GROUP:(g + 1) * POOL_GROUP, g * POOL_GROUP:(g + 1) * POOL_GROUP].set(w_pool[l, g])
    return dict(
        wb=wb, gq=mla_q_norm[l][None], gkv=mla_kv_norm[l][None], wqm=wqm, wqs=wqs, wk=wk, wv=wv, vone=vone,
        w_bd=w_bd.astype(BF16), wg=w_gt.astype(BF16), wa=w_br_mla[l].astype(BF16), wbn=w_br_na[l].astype(BF16),
        wp=w_br_pool[l].astype(BF16), wo=w_out[l].astype(BF16), wrt=w_router[l].T.astype(BF16))


def kernel(x, c, ctx, c_ctx, norm1_g, norm2_g, w_ada, b_ada, w_in, mla_q_norm, mla_kv_norm, w_uq, w_ukv, na_rpb, w_pool, pool_scale, w_br_mla, w_br_na, w_br_pool, w_out, w_router, w_gate, w_up, w_down, final_g):
    B, N, D = x.shape
    Cx = ctx.shape[1]
    L = w_in.shape[0]
    E = N_EXPERTS
    T = ROW_TILE
    assert Cx == T and N % (NA_KROWS * GRID_W) == 0 and B + 1 <= 8
    tiles_per_batch = N // T
    n_lat_tiles = B * tiles_per_batch
    cap_l = EC_CAPACITY * N // E
    cap_c = EC_CAPACITY * Cx // E

    x_all = jnp.concatenate([x.reshape(B * N, D), ctx.reshape(B * Cx, D)], axis=0)
    cc = jnp.concatenate([c, c_ctx[None], jnp.zeros((8 - B - 1, D), F32)], axis=0)
    mod_all = _ada(cc, w_ada, b_ada).reshape(L, 8, 1, 6 * D)
    cos_t, sin_t = _rope_tables(N, Cx)
    dims = dict(n_batch=B, n_lat=N, n_ctx=Cx)

    out = None
    for l in range(L):
        last = l == L - 1
        wts = _layer_weights(l, w_in, mla_q_norm, mla_kv_norm, w_uq, w_ukv, w_pool, w_br_mla, w_br_na, w_br_pool,
                             w_out, w_router)
        mod = mod_all[l]
        g1 = norm1_g[l][None]
        g2 = norm2_g[l][None]
        q_all, k_all, vt_all, na_all, pool_u = _inproj(x_all, mod, g1, wts, cos_t, sin_t, n_lat_tiles=n_lat_tiles,
                                                      tiles_per_batch=tiles_per_batch, n_batch=B)
        a_l = _flash(q_all, k_all, vt_all,latent=True, **dims)
        b_l = _na_lat(na_all, _na_bias(na_rpb[l], N // GRID_W), **dims)
        p_all = _pool(pool_u, wts["w_bd"], pool_scale[l][None], **dims)
        if not last:
            a_c = _flash(q_all, k_all, vt_all,latent=False, **dims)
            b_c = _na_ctx(na_all, **dims)
        else:
            a_c = jnp.zeros((B * Cx, MLA_W), BF16)
            b_c = jnp.zeros((B * Cx, NA_W), BF16)
        a_all = jnp.concatenate([a_l, a_c], axis=0)
        b_all = jnp.concatenate([b_l, b_c], axis=0)
        xn_all, h2_all, aff_t = _merge(x_all, mod, g1, g2, a_all, b_all, p_all, wts, tiles_per_batch=tiles_per_batch,
                                       n_batch=B)

        idx, gate = _topk(aff_t, n_sets=B, n=N, cap=cap_l, row0=0)
        if not last:
            idx_c, gate_c = _topk(aff_t, n_sets=B, n=Cx, cap=cap_c, row0=B * N)
            idx = jnp.concatenate([idx, idx_c], axis=1)
            gate = jnp.concatenate([gate, gate_c], axis=1)
        y = _moe(idx, gate, h2_all, w_gate, w_up, w_down, l)
        fg = final_g[None]
        xl_new = _combine(idx, y, xn_all, mod, fg, n_sets=B, n=N, cap=cap_l, set_col0=0, row0=0, n_batch=B,
                          latent=True, final=last)
        if last:
            out = xl_new.reshape(B, N, D)
        else:
            xc_new = _combine(idx, y, xn_all, mod, fg, n_sets=B, n=Cx, cap=cap_c, set_col0=B * cap_l, row0=B * N,
                              n_batch=B, latent=False, final=False)
            x_all = jnp.concatenate([xl_new, xc_new], axis=0)
    return out
```

```python
import functools

import numpy as np
import jax
import jax.numpy as jnp
from jax import lax
from jax.experimental import pallas as pl
from jax.experimental.pallas import tpu as pltpu

F32 = jnp.float32
BF16 = jnp.bfloat16
I32 = jnp.int32

GRID_W = 64
MLA_HEADS = 8
MLA_Q_RANK = 384
MLA_KV_RANK = 256
MLA_NOPE = 64
MLA_ROPE = 32
MLA_V = 64
MLA_W = MLA_HEADS * MLA_V
NA_HEADS = 4
NA_HEAD_DIM = 64
NA_W = NA_HEADS * NA_HEAD_DIM
NA_WIN_R = 8
NA_WIN_C = 16
POOL_WINDOWS = (2, 4, 8, 16)
POOL_GROUP = 64
POOL_WIDTH = POOL_GROUP * len(POOL_WINDOWS)
N_EXPERTS = 16
EC_CAPACITY = 2
N_BRANCH = 3
ROPE_BASE = 10000.0
EPS = 1e-6
IN_SIZES = (MLA_Q_RANK, MLA_KV_RANK, MLA_ROPE, 3 * NA_W, POOL_WIDTH)

LANES = 128
SUBLANES = 8
HEAD_PAD = 128
V_ROWS = MLA_V + 16
QK_EXP2_SCALE = (MLA_NOPE + MLA_ROPE) ** -0.5 * 1.4426950408889634
ROW_TILE = 256
POOL_HALO = 8
GATHER_UNROLL = 8
COMBINE_UNROLL = 4
MOE_ROW_BLOCKS = 4
FLASH_PIECE = 512
MASK_VALUE = -1e30
VMEM_LIMIT = 56 * 1024 * 1024


def _cparams(sem, vmem=VMEM_LIMIT):
    return pltpu.CompilerParams(dimension_semantics=sem, vmem_limit_bytes=vmem)


def _rms(x, g):
    return x * lax.rsqrt(jnp.mean(x * x, axis=-1, keepdims=True) + EPS) * g


def _dot(a, b):
    return jnp.dot(a, b, preferred_element_type=F32)


def _dot_nt(a, b):
    return lax.dot_general(a, b, (((1,), (1,)), ((), ())), preferred_element_type=F32)


def _ada_kernel(c_ref, w_ref, b_ref, o_ref):
    c = c_ref[...]
    s = c * jax.nn.sigmoid(c)
    o_ref[0] = _dot(s.astype(BF16), w_ref[0].astype(BF16)) + b_ref[0]


def _ada(cc, w_ada, b_ada):
    L, D, D6 = w_ada.shape
    tn = 1536
    return pl.pallas_call(
        _ada_kernel,
        grid=(L, D6 // tn),
        in_specs=[
            pl.BlockSpec((8, D), lambda l, j: (0, 0)),
            pl.BlockSpec((1, D, tn), lambda l, j: (l, 0, j)),
            pl.BlockSpec((1, 1, tn), lambda l, j: (l, 0, j)),
        ],
        out_specs=pl.BlockSpec((1, 8, tn), lambda l, j: (l, 0, j)),
        out_shape=jax.ShapeDtypeStruct((L, 8, D6), F32),
        compiler_params=_cparams(("parallel", "parallel")),
        name="ada",
    )(cc, w_ada, b_ada.reshape(L, 1, D6))


def _inproj_kernel(x_ref, sh_ref, sc_ref, g1_ref, wb_ref, gq_ref, gkv_ref, wqm_ref, wqs_ref, wk_ref, wv_ref,
                   vone_ref, cos_ref, sin_ref, q_ref, k_ref, vt_ref, na_ref, pool_ref):
    x = x_ref[...]
    h = _rms(x, g1_ref[...]) * (1.0 + sc_ref[0]) + sh_ref[0]
    z = _dot(h.astype(BF16), wb_ref[...])
    o = 0
    cq = z[:, o:o + MLA_Q_RANK]; o += MLA_Q_RANK
    ckv = z[:, o:o + MLA_KV_RANK]; o += MLA_KV_RANK
    kr = z[:, o:o + HEAD_PAD]; o += HEAD_PAD
    krs = z[:, o:o + HEAD_PAD]; o += HEAD_PAD
    na_ref[...] = z[:, o:o + 3 * NA_W].astype(BF16); o += 3 * NA_W
    pool_ref[...] = z[:, o:o + POOL_WIDTH]
    cqn = _rms(cq, gq_ref[...]).astype(BF16)
    ckvn = _rms(ckv, gkv_ref[...]).astype(BF16)
    cos = cos_ref[...]
    sin = sin_ref[...]
    cos8 = jnp.concatenate([cos] * MLA_HEADS, axis=1)
    sin8 = jnp.concatenate([sin] * MLA_HEADS, axis=1)
    q = _dot(cqn, wqm_ref[...]) * cos8 + _dot(cqn, wqs_ref[...]) * sin8
    q_ref[...] = (q * QK_EXP2_SCALE).astype(BF16)
    krot = kr * cos + krs * sin
    k = _dot(ckvn, wk_ref[...]) + jnp.concatenate([krot] * MLA_HEADS, axis=1)
    k_ref[...] = k.astype(BF16)
    vt_ref[...] = (_dot(ckvn, wv_ref[...]) + vone_ref[...]).T.astype(BF16)


def _inproj(x_all, mod, g1, wts, cos_t, sin_t, *, n_lat_tiles, tiles_per_batch, n_batch):
    R, D = x_all.shape
    T = ROW_TILE
    nb = wts["wb"].shape[1]

    def seg(i):
        return jnp.minimum(i // tiles_per_batch, n_batch)

    def tab(i):
        return jnp.where(i < n_lat_tiles, i % tiles_per_batch, tiles_per_batch)

    full = lambda a: pl.BlockSpec(a.shape, lambda i: (0,) * a.ndim)
    outs = [
        jax.ShapeDtypeStruct((R, MLA_HEADS * HEAD_PAD), BF16),
        jax.ShapeDtypeStruct((R, MLA_HEADS * HEAD_PAD), BF16),
        jax.ShapeDtypeStruct((MLA_HEADS * V_ROWS, R), BF16),
        jax.ShapeDtypeStruct((R, 3 * NA_W), BF16),
        jax.ShapeDtypeStruct((R, POOL_WIDTH), F32),
    ]
    row_major = lambda s: pl.BlockSpec((T, s.shape[1]), lambda i: (i, 0))
    out_specs = [row_major(outs[0]), row_major(outs[1]), pl.BlockSpec((MLA_HEADS * V_ROWS, T), lambda i: (0, i)),
                 row_major(outs[3]), row_major(outs[4])]
    return pl.pallas_call(
        _inproj_kernel,
        grid=(R // T,),
        in_specs=[
            pl.BlockSpec((T, D), lambda i: (i, 0)),
            pl.BlockSpec((1, 1, D), lambda i: (seg(i), 0, 0)),
            pl.BlockSpec((1, 1, D), lambda i: (seg(i), 0, 1)),
            full(g1), full(wts["wb"]), full(wts["gq"]), full(wts["gkv"]), full(wts["wqm"]), full(wts["wqs"]),
            full(wts["wk"]), full(wts["wv"]), full(wts["vone"]),
            pl.BlockSpec((T, HEAD_PAD), lambda i: (tab(i), 0)),
            pl.BlockSpec((T, HEAD_PAD), lambda i: (tab(i), 0)),
        ],
        out_specs=out_specs,
        out_shape=outs,
        compiler_params=_cparams(("parallel",)),
        name="inproj",
    )(x_all, mod, mod, g1, wts["wb"], wts["gq"], wts["gkv"], wts["wqm"], wts["wqs"], wts["wk"], wts["wv"],
      wts["vone"], cos_t, sin_t)


def _flash_kernel(*refs, tk, n_chunks):
    if n_chunks:
        q_ref, kc_ref, vtc_ref, kl_ref, vtl_ref, o_ref, m_ref, acc_ref, s_ref = refs
    else:
        q_ref, kc_ref, vtc_ref, o_ref, m_ref, acc_ref = refs
    tq = q_ref.shape[0]
    pw = min(FLASH_PIECE, tq)
    pieces = [(h, slice(pc * pw, (pc + 1) * pw)) for h in range(2) for pc in range(tq // pw)]
    m_ref[...] = jnp.full(m_ref.shape, -jnp.inf, F32)
    acc_ref[...] = jnp.zeros(acc_ref.shape, F32)

    def scores(h, k2, lanes):
        return _dot_nt(k2[:, h * HEAD_PAD:(h + 1) * HEAD_PAD], q_ref[lanes, h * HEAD_PAD:(h + 1) * HEAD_PAD])

    def softmax_pv(h, s, vt, lanes):
        m_old = m_ref[h, :, lanes]
        m_new = jnp.maximum(m_old, jnp.max(s, axis=0, keepdims=True))
        p = jnp.exp2(s - m_new)
        acc_ref[h, :, lanes] = (jnp.exp2(m_old - m_new) * acc_ref[h, :, lanes]
                                + _dot(vt[h * V_ROWS:(h + 1) * V_ROWS, :], p.astype(BF16)))
        m_ref[h, :, lanes] = m_new

    kc = kc_ref[...]
    vtc = vtc_ref[...]
    if not n_chunks:
        for h, lanes in pieces:
            softmax_pv(h, scores(h, kc, lanes), vtc, lanes)
    else:
        def kchunk(i):
            return kl_ref[pl.ds(pl.multiple_of(i * tk, tk), tk), :]

        def vchunk(i):
            return vtl_ref[:, pl.ds(pl.multiple_of(i * tk, tk), tk)]

        k0 = kchunk(0)
        for h, lanes in pieces:
            s_ref[0, h, :, lanes] = scores(h, k0, lanes)
            softmax_pv(h, scores(h, kc, lanes), vtc, lanes)

        def stage(slot_next, k_next, slot_cur, vt_cur):
            for h, lanes in pieces:
                s_ref[slot_next, h, :, lanes] = scores(h, k_next, lanes)
                softmax_pv(h, s_ref[slot_cur, h, :, lanes], vt_cur, lanes)

        def body(i2, carry):
            c0 = 2 * i2
            stage(1, kchunk(c0 + 1), 0, vchunk(c0))
            stage(0, kchunk(jnp.minimum(c0 + 2, n_chunks - 1)), 1, vchunk(c0 + 1))
            return carry
        lax.fori_loop(0, n_chunks // 2, body, 0)
    a0 = acc_ref[0]
    a1 = acc_ref[1]
    o_t = jnp.concatenate([a0[0:MLA_V] / a0[MLA_V:MLA_V + 1], a1[0:MLA_V] / a1[MLA_V:MLA_V + 1]], axis=0)
    o_ref[...] = o_t.T.astype(BF16)


def _flash(q_all, k_all, vt_all, *, n_batch, n_lat, n_ctx, latent):
    ctx_blk0 = (n_batch * n_lat) // n_ctx
    n_pairs = MLA_HEADS // 2
    if latent:
        tq = min(2048, n_lat)
        tk = min(512, n_lat // 2)
        nq = n_lat // tq
        q_map = lambda b, p, i: (b * nq + i, p)
        n_rows = n_batch * n_lat
        n_chunks = n_lat // tk
        assert n_chunks % 2 == 0
    else:
        tq, nq, tk, n_chunks = n_ctx, 1, n_ctx, 0
        q_map = lambda b, p, i: (ctx_blk0 + b, p)
        n_rows = n_batch * n_ctx
    in_specs = [
        pl.BlockSpec((tq, 2 * HEAD_PAD), q_map),
        pl.BlockSpec((n_ctx, 2 * HEAD_PAD), lambda b, p, i: (ctx_blk0 + b, p)),
        pl.BlockSpec((2 * V_ROWS, n_ctx), lambda b, p, i: (p, ctx_blk0 + b)),
    ]
    args = [q_all, k_all, vt_all]
    scratch = [pltpu.VMEM((2, 1, tq), F32), pltpu.VMEM((2, V_ROWS, tq), F32)]
    if latent:
        in_specs += [
            pl.BlockSpec((n_lat, 2 * HEAD_PAD), lambda b, p, i: (b, p)),
            pl.BlockSpec((2 * V_ROWS, n_lat), lambda b, p, i: (p, b)),
        ]
        args += [k_all, vt_all]
        scratch.append(pltpu.VMEM((2, 2, tk, tq), F32))
    return pl.pallas_call(
        functools.partial(_flash_kernel, tk=tk, n_chunks=n_chunks),
        grid=(n_batch, n_pairs, nq),
        in_specs=in_specs,
        out_specs=pl.BlockSpec((tq, 2 * MLA_V), lambda b, p, i: (b * nq + i, p)),
        out_shape=jax.ShapeDtypeStruct((n_rows, MLA_W), BF16),
        scratch_shapes=scratch,
        compiler_params=_cparams(("parallel", "parallel", "arbitrary")),
        name="flash_lat" if latent else "flash_ctx",
    )(*args)


NA_QROWS = 8
NA_KROWS = 16


def _na_softmax_pv(parts):
    m = None
    for s, _ in parts:
        mi = jnp.max(s, axis=1, keepdims=True)
        m = mi if m is None else jnp.maximum(m, mi)
    l = None
    o = None
    for s, v in parts:
        p = jnp.exp(s - m)
        li = jnp.sum(p, axis=1, keepdims=True)
        oi = _dot(p.astype(BF16), v)
        l = li if l is None else l + li
        o = oi if o is None else o + oi
    return o / l


def _na_lat_kernel(q_ref, k_ref, v_ref, kc_ref, vc_ref, bias_ref, o_ref, *, rows, scale):
    i = pl.program_id(1)
    s0 = jnp.clip(NA_QROWS * i - NA_WIN_R // 2, 0, rows - NA_KROWS) * GRID_W
    s0 = pl.multiple_of(s0, 256)
    nk = NA_KROWS * GRID_W
    lo = lax.broadcasted_iota(I32, (1, LANES), 1) < NA_HEAD_DIM
    outs = []
    for pr in range(NA_HEADS // 2):
        cs = slice(pr * LANES, (pr + 1) * LANES)
        qp = q_ref[:, cs]
        kw = k_ref[pl.ds(s0, nk), cs]
        vw = v_ref[pl.ds(s0, nk), cs]
        kc = kc_ref[:, cs]
        vc = vc_ref[:, cs]
        o_pair = None
        for hh in range(2):
            msk = lo if hh == 0 else jnp.logical_not(lo)
            qh = jnp.where(msk, qp, jnp.zeros_like(qp))
            s_lat = _dot_nt(qh, kw) * scale + bias_ref[0, 2 * pr + hh]
            s_ctx = _dot_nt(qh, kc) * scale
            o = _na_softmax_pv([(s_lat, vw), (s_ctx, vc)])
            o_pair = o if hh == 0 else jnp.where(lo, o_pair, o)
        outs.append(o_pair)
    o_ref[...] = jnp.concatenate(outs, axis=1).astype(BF16)


def _na_lat(na_all, bias, *, n_batch, n_lat, n_ctx):
    rows = n_lat // GRID_W
    nblk = rows // NA_QROWS
    tq = NA_QROWS * GRID_W
    ctx_blk0 = (n_batch * n_lat) // n_ctx
    scale = NA_HEAD_DIM ** -0.5

    def variant(i):
        return jnp.where(i == 0, 0, jnp.where(i == nblk - 1, 2, 1))

    return pl.pallas_call(
        functools.partial(_na_lat_kernel, rows=rows, scale=scale),
        grid=(n_batch, nblk),
        in_specs=[
            pl.BlockSpec((tq, NA_W), lambda b, i: (b * nblk + i, 0)),
            pl.BlockSpec((n_lat, NA_W), lambda b, i: (b, 1)),
            pl.BlockSpec((n_lat, NA_W), lambda b, i: (b, 2)),
            pl.BlockSpec((n_ctx, NA_W), lambda b, i: (ctx_blk0 + b, 1)),
            pl.BlockSpec((n_ctx, NA_W), lambda b, i: (ctx_blk0 + b, 2)),
            pl.BlockSpec((1, NA_HEADS, tq, NA_KROWS * GRID_W), lambda b, i: (variant(i), 0, 0, 0)),
        ],
        out_specs=pl.BlockSpec((tq, NA_W), lambda b, i: (b * nblk + i, 0)),
        out_shape=jax.ShapeDtypeStruct((n_batch * n_lat, NA_W), BF16),
        compiler_params=_cparams(("parallel", "arbitrary")),
        name="na_lat",
    )(na_all, na_all, na_all, na_all, na_all, bias)


def _na_ctx_kernel(q_ref, k_ref, v_ref, o_ref, *, scale):
    lo = lax.broadcasted_iota(I32, (1, LANES), 1) < NA_HEAD_DIM
    outs = []
    for pr in range(NA_HEADS // 2):
        cs = slice(pr * LANES, (pr + 1) * LANES)
        qp = q_ref[:, cs]
        kp = k_ref[:, cs]
        vp = v_ref[:, cs]
        o_pair = None
        for hh in range(2):
            msk = lo if hh == 0 else jnp.logical_not(lo)
            qh = jnp.where(msk, qp, jnp.zeros_like(qp))
            o = _na_softmax_pv([(_dot_nt(qh, kp) * scale, vp)])
            o_pair = o if hh == 0 else jnp.where(lo, o_pair, o)
        outs.append(o_pair)
    o_ref[...] = jnp.concatenate(outs, axis=1).astype(BF16)


def _na_ctx(na_all, *, n_batch, n_lat, n_ctx):
    ctx_blk0 = (n_batch * n_lat) // n_ctx
    return pl.pallas_call(
        functools.partial(_na_ctx_kernel, scale=NA_HEAD_DIM ** -0.5),
        grid=(n_batch,),
        in_specs=[pl.BlockSpec((n_ctx, NA_W), lambda b, j=j: (ctx_blk0 + b, j)) for j in range(3)],
        out_specs=pl.BlockSpec((n_ctx, NA_W), lambda b: (b, 0)),
        out_shape=jax.ShapeDtypeStruct((n_batch * n_ctx, NA_W), BF16),
        compiler_params=_cparams(("parallel",)),
        name="na_ctx",
    )(na_all, na_all, na_all)


def _na_bias(rpb, rows):
    nblk = rows // NA_QROWS
    qc = np.arange(GRID_W)
    kc = np.arange(GRID_W)
    start_c = np.clip(qc - NA_WIN_C // 2, 0, GRID_W - NA_WIN_C)
    valid_c = (kc[None, :] >= start_c[:, None]) & (kc[None, :] < start_c[:, None] + NA_WIN_C)
    dc = np.clip(kc[None, :] - qc[:, None], -(NA_WIN_C - 1), NA_WIN_C - 1) + (NA_WIN_C - 1)
    onehot = (dc.reshape(1, -1) == np.arange(2 * NA_WIN_C - 1)[:, None]).astype(np.float32)
    by_dr = jnp.einsum("hab,bn->han", rpb.astype(F32), jnp.asarray(onehot), precision=lax.Precision.HIGHEST)
    by_dr = by_dr.reshape(NA_HEADS, 2 * NA_WIN_R - 1, GRID_W, GRID_W)
    tables = []
    for i in (0, min(1, nblk - 1), nblk - 1):
        s0 = min(max(NA_QROWS * i - NA_WIN_R // 2, 0), rows - NA_KROWS)
        qrow = NA_QROWS * i + np.arange(NA_QROWS)
        krow = s0 + np.arange(NA_KROWS)
        start_r = np.clip(qrow - NA_WIN_R // 2, 0, rows - NA_WIN_R)
        valid_r = (krow[None, :] >= start_r[:, None]) & (krow[None, :] < start_r[:, None] + NA_WIN_R)
        dr = np.clip(krow[None, :] - qrow[:, None] + (NA_WIN_R - 1), 0, 2 * NA_WIN_R - 2)
        b = jnp.transpose(by_dr[:, dr], (0, 1, 3, 2, 4))
        valid = valid_r[:, None, :, None] & valid_c[None, :, None, :]
        b = jnp.where(jnp.asarray(valid)[None], b, MASK_VALUE)
        tables.append(b.reshape(NA_HEADS, NA_QROWS * GRID_W, NA_KROWS * GRID_W))
    return jnp.stack(tables)


def _pool_kernel(prev_ref, cur_ref, next_ref, w_ref, ps_ref, o_ref, scr, *, tiles_per_batch, n_lat_tiles, n_lat, n_ctx):
    i = pl.program_id(0)
    T = ROW_TILE
    H = POOL_HALO
    is_lat = i < n_lat_tiles
    tile_in_seq = jnp.where(is_lat, i % tiles_per_batch, 0)
    n_seq = jnp.where(is_lat, n_lat, n_ctx)
    first = tile_in_seq == 0
    last = jnp.where(is_lat, tile_in_seq == tiles_per_batch - 1, True)
    u = cur_ref[...]
    scr[0:H, :] = jnp.where(first, 0.0, prev_ref[...])
    scr[H:H + T, :] = u
    scr[H + T:H + T + H, :] = jnp.where(last, 0.0, next_ref[...])

    def win(lo, hi):
        acc = None
        for j in range(lo, hi):
            piece = scr[H + j:H + j + T, :]
            acc = piece if acc is None else acc + piece
        return acc

    t = (tile_in_seq * T + lax.broadcasted_iota(I32, (T, 1), 0))
    lane = lax.broadcasted_iota(I32, (1, POOL_WIDTH), 1)
    sums = {}
    acc = win(-1, 1)
    sums[2] = acc
    acc = acc + win(-2, -1) + win(1, 2)
    sums[4] = acc
    acc = acc + win(-4, -2) + win(2, 4)
    sums[8] = acc
    acc = acc + win(-8, -4) + win(4, 8)
    sums[16] = acc
    mean = None
    for gi, w in enumerate(POOL_WINDOWS):
        cnt = (jnp.minimum(t + w // 2, n_seq) - jnp.maximum(t - w // 2, 0)).astype(F32)
        mg = sums[w] / cnt
        mean = mg if mean is None else jnp.where(lane < gi * POOL_GROUP, mean, mg)
    dlt = mean - u
    o_ref[...] = (_dot(dlt.astype(BF16), w_ref[...]) * ps_ref[...]).astype(BF16)


def _pool(u_all, w_bd, pool_scale, *, n_batch, n_lat, n_ctx):
    R = u_all.shape[0]
    T = ROW_TILE
    H = POOL_HALO
    assert n_ctx == T
    tiles_per_batch = n_lat // T
    n_lat_tiles = n_batch * tiles_per_batch
    nb8 = R // H
    return pl.pallas_call(
        functools.partial(_pool_kernel, tiles_per_batch=tiles_per_batch, n_lat_tiles=n_lat_tiles, n_lat=n_lat,
                          n_ctx=n_ctx),
        grid=(R // T,),
        in_specs=[
            pl.BlockSpec((H, POOL_WIDTH), lambda i: (jnp.maximum(i * (T // H) - 1, 0), 0)),
            pl.BlockSpec((T, POOL_WIDTH), lambda i: (i, 0)),
            pl.BlockSpec((H, POOL_WIDTH), lambda i: (jnp.minimum((i + 1) * (T // H), nb8 - 1), 0)),
            pl.BlockSpec((POOL_WIDTH, POOL_WIDTH), lambda i: (0, 0)),
            pl.BlockSpec((1, POOL_WIDTH), lambda i: (0, 0)),
        ],
        out_specs=pl.BlockSpec((T, POOL_WIDTH), lambda i: (i, 0)),
        out_shape=jax.ShapeDtypeStruct((R, POOL_WIDTH), BF16),
        scratch_shapes=[pltpu.VMEM((T + 2 * H, POOL_WIDTH), F32)],
        compiler_params=_cparams(("parallel",)),
        name="pool",
    )(u_all, u_all, u_all, w_bd, pool_scale)


def _merge_kernel(x_ref, sh1_ref, sc1_ref, gt1_ref, sh2_ref, sc2_ref, g1_ref, g2_ref, a_ref, b_ref, p_ref,
                  wg_ref, wa_ref, wb_ref, wp_ref, wo_ref, wrt_ref, xn_ref, h2_ref, afft_ref):
    D = x_ref.shape[1]
    x = x_ref[...]
    h = _rms(x, g1_ref[...]) * (1.0 + sc1_ref[0]) + sh1_ref[0]
    g = jax.nn.sigmoid(_dot(h.astype(BF16), wg_ref[...]))
    m = (g[:, 0:D] * _dot(a_ref[...], wa_ref[...]) + g[:, D:2 * D] * _dot(b_ref[...], wb_ref[...])
         + g[:, 2 * D:3 * D] * _dot(p_ref[...], wp_ref[...]))
    xn = x + gt1_ref[0] * _dot(m.astype(BF16), wo_ref[...])
    xn_ref[...] = xn
    h2 = _rms(xn, g2_ref[...]) * (1.0 + sc2_ref[0]) + sh2_ref[0]
    h2_ref[...] = h2
    logit_t = _dot_nt(wrt_ref[...], h2.astype(BF16))
    ex_t = jnp.exp(logit_t - jnp.max(logit_t, axis=0, keepdims=True))
    afft_ref[...] = ex_t / jnp.sum(ex_t, axis=0, keepdims=True)


def _merge(x_all, mod, g1, g2, a_all, b_all, p_all, wts, *, tiles_per_batch, n_batch):
    R, D = x_all.shape
    T = ROW_TILE

    def seg(i):
        return jnp.minimum(i // tiles_per_batch, n_batch)

    modspec = lambda k: pl.BlockSpec((1, 1, D), lambda i: (seg(i), 0, k))
    full = lambda a: pl.BlockSpec(a.shape, lambda i: (0,) * a.ndim)
    row = lambda w: pl.BlockSpec((T, w), lambda i: (i, 0))
    outs = [
        jax.ShapeDtypeStruct((R, D), F32),
        jax.ShapeDtypeStruct((R, D), F32),
        jax.ShapeDtypeStruct((N_EXPERTS, R), F32),
    ]
    wnames = ("wg", "wa", "wbn", "wp", "wo", "wrt")
    return pl.pallas_call(
        _merge_kernel,
        grid=(R // T,),
        in_specs=[row(D), modspec(0), modspec(1), modspec(2), modspec(3), modspec(4), full(g1), full(g2),
                  row(MLA_W), row(NA_W), row(POOL_WIDTH)] + [full(wts[n]) for n in wnames],
        out_specs=[row(D), row(D), pl.BlockSpec((N_EXPERTS, T), lambda i: (0, i))],
        out_shape=outs,
        compiler_params=_cparams(("parallel",)),
        name="merge",
    )(x_all, mod, mod, mod, mod, mod, g1, g2, a_all, b_all, p_all, *[wts[n] for n in wnames])


def _onehot(mask):
    return jnp.where(mask, 1.0, 0.0).astype(BF16)


def _col_to_row(col):
    m = col.shape[0]
    wide = jnp.broadcast_to(col, (m, LANES))
    if m < LANES:
        wide = jnp.concatenate([wide, jnp.zeros((LANES - m, LANES), col.dtype)], axis=0)
    return wide.T[0:1, 0:m]


def _topk_kernel(aff_ref, idx_ref, gate_ref, wc_ref, affc_ref, cend_ref, off_ref, *, n, cap, row0, jt):
    s = pl.program_id(0)
    nc = n // LANES
    aff = aff_ref[...]
    bits = lax.bitcast_convert_type(aff, I32)

    def search(it, thr):
        cand = thr | jnp.left_shift(jnp.int32(1), 30 - it)
        cnt = jnp.sum((bits >= cand).astype(I32), axis=1, keepdims=True)
        return jnp.where(cnt >= cap, cand, thr)

    thr = lax.fori_loop(0, 31, search, jnp.zeros((N_EXPERTS, 1), I32))
    gt = bits > thr
    eq = bits == thr
    need = (cap - jnp.sum(gt.astype(I32), axis=1, keepdims=True)).astype(F32)
    lane = lax.broadcasted_iota(I32, (1, LANES), 1)
    upper = _onehot(lax.broadcasted_iota(I32, (LANES, LANES), 0) <= lax.broadcasted_iota(I32, (LANES, LANES), 1))

    wc_ref[:, nc:, :] = jnp.zeros((N_EXPERTS, LANES - nc, LANES), F32)
    affc_ref[:, nc:, :] = jnp.zeros((N_EXPERTS, LANES - nc, LANES), F32)
    ties_before = jnp.zeros((N_EXPERTS, 1), F32)
    tot = jnp.zeros((N_EXPERTS, LANES), F32)
    for ch in range(nc):
        sl = slice(ch * LANES, (ch + 1) * LANES)
        tie_rank = _dot(_onehot(eq[:, sl]), upper) + ties_before
        ties_before = tie_rank[:, LANES - 1:LANES]
        sel = gt[:, sl] | (eq[:, sl] & (tie_rank <= need))
        w = _dot(_onehot(sel), upper)
        wc_ref[:, ch, :] = w
        affc_ref[:, ch, :] = aff[:, sl]
        tot = jnp.where(lane == ch, w[:, LANES - 1:LANES], tot)
    cend = _dot(tot.astype(BF16), upper)
    cend_ref[...] = cend
    off_ref[...] = cend - tot
    base = row0 + s * n

    def per_expert(e, carry):
        wc = wc_ref[e].astype(BF16)
        a = affc_ref[e]
        hi = a.astype(BF16)
        rest = a - hi.astype(F32)
        mid = rest.astype(BF16)
        lo = (rest - mid.astype(F32)).astype(BF16)
        cend_e = cend_ref[pl.ds(e, 1), :]
        off_e = off_ref[pl.ds(e, 1), :]
        idx_row = []
        gate_row = []
        for t in range(cap // jt):
            jcol = (t * jt + lax.broadcasted_iota(I32, (jt, 1), 0)).astype(F32)
            cj = jnp.sum((cend_e <= jcol).astype(I32), axis=1, keepdims=True)
            in_chunk = lane == cj
            g = _onehot(in_chunk)
            rank = jcol - jnp.sum(jnp.where(in_chunk, off_e, 0.0), axis=1, keepdims=True)
            pos = jnp.sum((_dot(g, wc) <= rank).astype(I32), axis=1, keepdims=True)
            aff_rows = _dot(g, hi) + _dot(g, mid) + _dot(g, lo)
            gate = jnp.sum(jnp.where(lane == pos, aff_rows, 0.0), axis=1, keepdims=True)
            idx_row.append(_col_to_row(cj * LANES + pos + base))
            gate_row.append(_col_to_row(gate))
        mine = lax.broadcasted_iota(I32, (N_EXPERTS, 1), 0) == e
        idx_all, gate_all = carry
        return (jnp.where(mine, jnp.concatenate(idx_row, axis=1), idx_all),
                jnp.where(mine, jnp.concatenate(gate_row, axis=1), gate_all))

    idx_all, gate_all = lax.fori_loop(0, N_EXPERTS, per_expert, (jnp.zeros((N_EXPERTS, cap), I32),
                                                                 jnp.zeros((N_EXPERTS, cap), F32)))
    idx_ref[0] = idx_all
    gate_ref[0] = gate_all


def _topk(aff_t, *, n_sets, n, cap, row0):
    jt = min(128, cap)
    blk0 = row0 // n
    spec = pl.BlockSpec((1, N_EXPERTS, cap), lambda s: (s, 0, 0))
    idx, gate = pl.pallas_call(
        functools.partial(_topk_kernel, n=n, cap=cap, row0=row0, jt=jt),
        grid=(n_sets,),
        in_specs=[pl.BlockSpec((N_EXPERTS, n), lambda s: (0, blk0 + s))],
        out_specs=[spec, spec],
        out_shape=[jax.ShapeDtypeStruct((n_sets, N_EXPERTS, cap), I32),
                   jax.ShapeDtypeStruct((n_sets, N_EXPERTS, cap), F32)],
        scratch_shapes=[pltpu.VMEM((N_EXPERTS, LANES, LANES), F32), pltpu.VMEM((N_EXPERTS, LANES, LANES), F32),
                        pltpu.VMEM((N_EXPERTS, LANES), F32), pltpu.VMEM((N_EXPERTS, LANES), F32)],
        compiler_params=_cparams(("parallel",)),
        name="topk",
    )(aff_t)
    idx = jnp.transpose(idx, (1, 0, 2)).reshape(N_EXPERTS, n_sets * cap)
    gate = jnp.transpose(gate, (1, 0, 2)).reshape(N_EXPERTS, n_sets * cap)
    return idx, gate


def _moe_kernel(idx_ref, h_hbm, gate_ref, wg_ref, wu_ref, wd_ref, y_ref, xs, xb, sem, *, rows, nf):
    e = pl.program_id(0)
    f = pl.program_id(1)

    def row_copy(j, t):
        return pltpu.make_async_copy(h_hbm.at[t], xs.at[pl.ds(pl.multiple_of(j * SUBLANES, SUBLANES), SUBLANES), :],
                                     sem)

    def issue_gather(expert):
        base = expert * rows

        def issue(j8, carry):
            for r in range(GATHER_UNROLL):
                j = j8 * GATHER_UNROLL + r
                row_copy(j, idx_ref[base + j]).start()
            return carry
        lax.fori_loop(0, rows // GATHER_UNROLL, issue, 0)

    @pl.when(f == 0)
    def _gather():
        @pl.when(e == 0)
        def _first():
            issue_gather(0)

        def wait(j8, carry):
            for r in range(GATHER_UNROLL):
                row_copy(j8 * GATHER_UNROLL + r, 0).wait()
            return carry
        lax.fori_loop(0, rows // GATHER_UNROLL, wait, 0)
        for s in range(SUBLANES):
            xb[:, s * LANES:(s + 1) * LANES] = xs[pl.ds(s, rows, stride=SUBLANES), :].astype(BF16)

        @pl.when(e + 1 < pl.num_programs(0))
        def _next():
            issue_gather(e + 1)
        y_ref[0] = jnp.zeros(y_ref.shape[1:], F32)

    wg = wg_ref[0, 0].astype(BF16)
    wu = wu_ref[0, 0].astype(BF16)
    wd = wd_ref[0, 0].astype(BF16)
    rb = rows // MOE_ROW_BLOCKS
    for b in range(MOE_ROW_BLOCKS):
        xv = xb[b * rb:(b + 1) * rb, :]
        a = _dot(xv, wg)
        u = _dot(xv, wu)
        hmid = (a * jax.nn.sigmoid(a)) * u
        y_ref[0, b * rb:(b + 1) * rb, :] += _dot(hmid.astype(BF16), wd)

    @pl.when(f == nf - 1)
    def _scale():
        y_ref[0] = y_ref[0] * gate_ref[0]


def _moe(idx, gate, h2, w_gate, w_up, w_down, layer):
    E, rows = idx.shape
    _, _, D, F = w_gate.shape
    assert D == SUBLANES * LANES and rows % GATHER_UNROLL == 0 and rows % (16 * MOE_ROW_BLOCKS) == 0
    tf = 256
    nf = F // tf
    grid_spec = pltpu.PrefetchScalarGridSpec(
        num_scalar_prefetch=1,
        grid=(E, nf),
        in_specs=[
            pl.BlockSpec(memory_space=pl.ANY),
            pl.BlockSpec((1, rows, 1), lambda e, f, idx: (e, 0, 0)),
            pl.BlockSpec((1, 1, D, tf), lambda e, f, idx: (layer, e, 0, f)),
            pl.BlockSpec((1, 1, D, tf), lambda e, f, idx: (layer, e, 0, f)),
            pl.BlockSpec((1, 1, tf, D), lambda e, f, idx: (layer, e, f, 0)),
        ],
        out_specs=pl.BlockSpec((1, rows, D), lambda e, f, idx: (e, 0, 0)),
        scratch_shapes=[
            pltpu.VMEM((rows * SUBLANES, LANES), F32),
            pltpu.VMEM((rows, D), BF16),
            pltpu.SemaphoreType.DMA,
        ],
    )
    return pl.pallas_call(
        functools.partial(_moe_kernel, rows=rows, nf=nf),
        grid_spec=grid_spec,
        out_shape=jax.ShapeDtypeStruct((E, rows, D), F32),
        compiler_params=_cparams(("arbitrary", "arbitrary")),
        name="moe",
    )(idx.reshape(-1), h2.reshape(h2.shape[0], SUBLANES, LANES), gate[:, :, None], w_gate, w_up, w_down)


def _combine_kernel(idx_ref, y_hbm, x_ref, g2_ref, fg_ref, o_ref, slab, win_ref, sems, *, tc, cap, set_col0, row0,
                    tiles_per_set, win, chunk, final):
    t_id = pl.program_id(0)
    e = pl.program_id(1)
    n_t = pl.num_programs(0)
    n_e = pl.num_programs(1)
    rows = y_hbm.shape[1]
    step = t_id * n_e + e
    slot = step % 2

    def window(tile, expert):
        col0 = expert * rows + set_col0 + (tile // tiles_per_set) * cap

        def lower_bound(v):
            def body(_, lh):
                lo, hi = lh
                mid = (lo + hi) // 2
                less = idx_ref[col0 + jnp.minimum(mid, cap - 1)] < v
                active = lo < hi
                return (jnp.where(active & less, mid + 1, lo), jnp.where(active & jnp.logical_not(less), mid, hi))
            return lax.fori_loop(0, cap.bit_length() + 1, body, (jnp.int32(0), jnp.int32(cap)))[0]

        first = col0 - expert * rows
        r0 = first + lower_bound(row0 + tile * tc)
        r1 = first + lower_bound(row0 + (tile + 1) * tc)
        w0 = jnp.minimum((r0 // SUBLANES) * SUBLANES, rows - win)
        return r0, r1, w0

    def chunk_copies(expert, r0, r1, w0, sl):
        w0 = pl.multiple_of(w0, SUBLANES)
        return [(pltpu.make_async_copy(y_hbm.at[expert, pl.ds(w0 + k * chunk, chunk), :],
                                       slab.at[sl, pl.ds(k * chunk, chunk), :], sems.at[sl]),
                 (w0 + k * chunk < r1) & (w0 + (k + 1) * chunk > r0)) for k in range(win // chunk)]

    def fetch(tile, expert, sl):
        r0, r1, w0 = window(tile, expert)
        win_ref[sl, 0] = r0
        win_ref[sl, 1] = r1
        win_ref[sl, 2] = w0
        for copy, needed in chunk_copies(expert, r0, r1, w0, sl):
            @pl.when(needed)
            def _start(copy=copy):
                copy.start()

    @pl.when(step == 0)
    def _first():
        fetch(t_id, e, slot)

    r0 = win_ref[slot, 0]
    r1 = win_ref[slot, 1]
    w0 = win_ref[slot, 2]
    for copy, needed in chunk_copies(e, r0, r1, w0, slot):
        @pl.when(needed)
        def _wait(copy=copy):
            copy.wait()

    @pl.when(step + 1 < n_t * n_e)
    def _prefetch():
        wrap = e == n_e - 1
        fetch(jnp.where(wrap, t_id + 1, t_id), jnp.where(wrap, 0, e + 1), 1 - slot)

    @pl.when(e == 0)
    def _init():
        o_ref[...] = jnp.zeros(o_ref.shape, F32)

    row_base = e * rows
    tile_start = row0 + t_id * tc

    def add_rows(j0, n):
        ts = [idx_ref[row_base + j0 + r] - tile_start for r in range(n)]
        cur = [o_ref[pl.ds(t, 1), :] for t in ts]
        new = [slab[slot, pl.ds(j0 + r - w0, 1), :] for r in range(n)]
        for r in range(n):
            o_ref[pl.ds(ts[r], 1), :] = cur[r] + new[r]

    n_groups = (r1 - r0) // COMBINE_UNROLL

    def add_group(g, carry):
        add_rows(r0 + g * COMBINE_UNROLL, COMBINE_UNROLL)
        return carry
    lax.fori_loop(0, n_groups, add_group, 0)

    def add_tail(j, carry):
        add_rows(j, 1)
        return carry
    lax.fori_loop(r0 + n_groups * COMBINE_UNROLL, r1, add_tail, 0)

    @pl.when(e == n_e - 1)
    def _finish():
        out = x_ref[...] + g2_ref[0] * o_ref[...]
        if final:
            out = _rms(out, fg_ref[...])
        o_ref[...] = out


def _combine(idx, y, x_all, mod, final_g, *, n_sets, n, cap, set_col0, row0, n_batch, latent, final):
    E, rows, D = y.shape
    tc = min(2048, n)
    tiles_per_set = n // tc
    chunk = min(256, tc, cap)
    win = min(tc, cap) + chunk
    assert rows >= win and (rows - win) % SUBLANES == 0
    blk0 = row0 // tc
    if latent:
        seg = lambda t: t // tiles_per_set
    else:
        seg = lambda t: n_batch
    grid_spec = pltpu.PrefetchScalarGridSpec(
        num_scalar_prefetch=1,
        grid=(n_sets * tiles_per_set, E),
        in_specs=[
            pl.BlockSpec(memory_space=pl.ANY),
            pl.BlockSpec((tc, D), lambda t, e, idx: (blk0 + t, 0)),
            pl.BlockSpec((1, 1, D), lambda t, e, idx: (seg(t), 0, 5)),
            pl.BlockSpec((1, D), lambda t, e, idx: (0, 0)),
        ],
        out_specs=pl.BlockSpec((tc, D), lambda t, e, idx: (t, 0)),
        scratch_shapes=[pltpu.VMEM((2, win, D), F32), pltpu.SMEM((2, 3), I32), pltpu.SemaphoreType.DMA((2,))],
    )
    return pl.pallas_call(
        functools.partial(_combine_kernel, tc=tc, cap=cap, set_col0=set_col0, row0=row0,
                          tiles_per_set=tiles_per_set, win=win, chunk=chunk, final=final),
        grid_spec=grid_spec,
        out_shape=jax.ShapeDtypeStruct((n_sets * n, D), F32),
        compiler_params=_cparams(("arbitrary", "arbitrary")),
        name="combine_lat" if latent else "combine_ctx",
    )(idx.reshape(-1), y, x_all, mod, final_g)


def _rope_tables(n_lat, n_ctx):
    half = MLA_ROPE // 2
    quarter = half // 2
    inv = ROPE_BASE ** (-jnp.arange(quarter, dtype=F32) / quarter)
    t = jnp.arange(n_lat)
    ang_r = (t // GRID_W).astype(F32)[:, None] * inv
    ang_c = (t % GRID_W).astype(F32)[:, None] * inv
    cos32 = jnp.concatenate([jnp.cos(ang_r), jnp.cos(ang_r), jnp.cos(ang_c), jnp.cos(ang_c)], axis=1)
    sin32 = jnp.concatenate([-jnp.sin(ang_r), jnp.sin(ang_r), -jnp.sin(ang_c), jnp.sin(ang_c)], axis=1)
    pad = HEAD_PAD - MLA_NOPE - MLA_ROPE
    cos_l = jnp.concatenate([jnp.ones((n_lat, MLA_NOPE), F32), cos32, jnp.zeros((n_lat, pad), F32)], axis=1)
    sin_l = jnp.concatenate([jnp.zeros((n_lat, MLA_NOPE), F32), sin32, jnp.zeros((n_lat, pad), F32)], axis=1)
    cos_c = jnp.concatenate([jnp.ones((ROW_TILE, MLA_NOPE + MLA_ROPE), F32), jnp.zeros((ROW_TILE, pad), F32)], axis=1)
    sin_c = jnp.zeros((ROW_TILE, HEAD_PAD), F32)
    return jnp.concatenate([cos_l, cos_c]), jnp.concatenate([sin_l, sin_c])


def _rope_partner():
    half = MLA_ROPE // 2
    quarter = half // 2
    p = jnp.arange(MLA_ROPE)
    return (p // half) * half + (p % half + quarter) % half


def _layer_weights(l, w_in, mla_q_norm, mla_kv_norm, w_uq, w_ukv, w_pool, w_br_mla, w_br_na, w_br_pool, w_out,
                   w_router):
    D = w_in.shape[1]
    offs = [0]
    for s in IN_SIZES:
        offs.append(offs[-1] + s)
    wi = w_in[l]
    w_cq, w_ckv, w_kr, w_na, w_pl = [wi[:, offs[k]:offs[k + 1]] for k in range(5)]
    w_gt = wi[:, offs[5]:]
    partner = _rope_partner()
    zl = jnp.zeros((D, MLA_NOPE), F32)
    zr = jnp.zeros((D, HEAD_PAD - MLA_NOPE - MLA_ROPE), F32)
    w_kr_p = jnp.concatenate([zl, w_kr, zr], axis=1)
    w_krs_p = jnp.concatenate([zl, w_kr[:, partner], zr], axis=1)
    wb = jnp.concatenate([w_cq, w_ckv, w_kr_p, w_krs_p, w_na, w_pl], axis=1).astype(BF16)

    dq = MLA_NOPE + MLA_ROPE
    uq = w_uq[l].reshape(MLA_Q_RANK, MLA_HEADS, dq)
    zq = jnp.zeros((MLA_Q_RANK, MLA_HEADS, HEAD_PAD - dq), F32)
    wqm = jnp.concatenate([uq, zq], axis=2).reshape(MLA_Q_RANK, MLA_HEADS * HEAD_PAD).astype(BF16)
    wqs = jnp.concatenate([jnp.zeros((MLA_Q_RANK, MLA_HEADS, MLA_NOPE), F32), uq[:, :, MLA_NOPE:][:, :, partner], zq],
                          axis=2).reshape(MLA_Q_RANK, MLA_HEADS * HEAD_PAD).astype(BF16)
    ukv = w_ukv[l].reshape(MLA_KV_RANK, MLA_HEADS, MLA_NOPE + MLA_V)
    wk = jnp.concatenate([ukv[:, :, :MLA_NOPE], jnp.zeros((MLA_KV_RANK, MLA_HEADS, HEAD_PAD - MLA_NOPE), F32)],
                         axis=2).reshape(MLA_KV_RANK, MLA_HEADS * HEAD_PAD).astype(BF16)
    wv = jnp.concatenate([ukv[:, :, MLA_NOPE:], jnp.zeros((MLA_KV_RANK, MLA_HEADS, V_ROWS - MLA_V), F32)],
                         axis=2).reshape(MLA_KV_RANK, MLA_HEADS * V_ROWS).astype(BF16)
    vone = jnp.tile(jnp.concatenate([jnp.zeros((MLA_V,), F32), jnp.ones((V_ROWS - MLA_V,), F32)]), MLA_HEADS)[None]

    w_bd = jnp.zeros((POOL_WIDTH, POOL_WIDTH), F32)
    for g in range(len(POOL_WINDOWS)):
        w_bd = w_bd.at[g * POOL_GROUP:(g + 1) * POOL_GROUP, g * POOL_GROUP:(g + 1) * POOL_GROUP].set(w_pool[l, g])
    return dict(
        wb=wb, gq=mla_q_norm[l][None], gkv=mla_kv_norm[l][None], wqm=wqm, wqs=wqs, wk=wk, wv=wv, vone=vone,
        w_bd=w_bd.astype(BF16), wg=w_gt.astype(BF16), wa=w_br_mla[l].astype(BF16), wbn=w_br_na[l].astype(BF16),
        wp=w_br_pool[l].astype(BF16), wo=w_out[l].astype(BF16), wrt=w_router[l].T.astype(BF16))


def kernel(x, c, ctx, c_ctx, norm1_g, norm2_g, w_ada, b_ada, w_in, mla_q_norm, mla_kv_norm, w_uq, w_ukv, na_rpb, w_pool, pool_scale, w_br_mla, w_br_na, w_br_pool, w_out, w_router, w_gate, w_up, w_down, final_g):
    B, N, D = x.shape
    Cx = ctx.shape[1]
    L = w_in.shape[0]
    E = N_EXPERTS
    T = ROW_TILE
    assert Cx == T and N % (NA_KROWS * GRID_W) == 0 and B + 1 <= 8
    tiles_per_batch = N // T
    n_lat_tiles = B * tiles_per_batch
    cap_l = EC_CAPACITY * N // E
    cap_c = EC_CAPACITY * Cx // E

    x_all = jnp.concatenate([x.reshape(B * N, D), ctx.reshape(B * Cx, D)], axis=0)
    cc = jnp.concatenate([c, c_ctx[None], jnp.zeros((8 - B - 1, D), F32)], axis=0)
    mod_all = _ada(cc, w_ada, b_ada).reshape(L, 8, 1, 6 * D)
    cos_t, sin_t = _rope_tables(N, Cx)
    dims = dict(n_batch=B, n_lat=N, n_ctx=Cx)

    out = None
    for l in range(L):
        last = l == L - 1
        wts = _layer_weights(l, w_in, mla_q_norm, mla_kv_norm, w_uq, w_ukv, w_pool, w_br_mla, w_br_na, w_br_pool,
                             w_out, w_router)
        mod = mod_all[l]
        g1 = norm1_g[l][None]
        g2 = norm2_g[l][None]
        q_all, k_all, vt_all, na_all, pool_u = _inproj(x_all, mod, g1, wts, cos_t, sin_t, n_lat_tiles=n_lat_tiles,
                                                      tiles_per_batch=tiles_per_batch, n_batch=B)
        a_l = _flash(q_all, k_all, vt_all,latent=True, **dims)
        b_l = _na_lat(na_all, _na_bias(na_rpb[l], N // GRID_W), **dims)
        p_all = _pool(pool_u, wts["w_bd"], pool_scale[l][None], **dims)
        if not last:
            a_c = _flash(q_all, k_all, vt_all,latent=False, **dims)
            b_c = _na_ctx(na_all, **dims)
        else:
            a_c = jnp.zeros((B * Cx, MLA_W), BF16)
            b_c = jnp.zeros((B * Cx, NA_W), BF16)
        a_all = jnp.concatenate([a_l, a_c], axis=0)
        b_all = jnp.concatenate([b_l, b_c], axis=0)
        xn_all, h2_all, aff_t = _merge(x_all, mod, g1, g2, a_all, b_all, p_all, wts, tiles_per_batch=tiles_per_batch,
                                       n_batch=B)

        idx, gate = _topk(aff_t, n_sets=B, n=N, cap=cap_l, row0=0)
        if not last:
            idx_c, gate_c = _topk(aff_t, n_sets=B, n=Cx, cap=cap_c, row0=B * N)
            idx = jnp.concatenate([idx, idx_c], axis=1)
            gate = jnp.concatenate([gate, gate_c], axis=1)
        y = _moe(idx, gate, h2_all, w_gate, w_up, w_down, l)
        fg = final_g[None]
        xl_new = _combine(idx, y, xn_all, mod, fg, n_sets=B, n=N, cap=cap_l, set_col0=0, row0=0, n_batch=B,
                          latent=True, final=last)
        if last:
            out = xl_new.reshape(B, N, D)
        else:
            xc_new = _combine(idx, y, xn_all, mod, fg, n_sets=B, n=Cx, cap=cap_c, set_col0=B * cap_l, row0=B * N,
                              n_batch=B, latent=False, final=False)
            x_all = jnp.concatenate([xl_new, xc_new], axis=0)
    return out
```

```python
import functools

import numpy as np
import jax
import jax.numpy as jnp
from jax import lax
from jax.experimental import pallas as pl
from jax.experimental.pallas import tpu as pltpu

F32 = jnp.float32
BF16 = jnp.bfloat16
I32 = jnp.int32

GRID_W = 64
MLA_HEADS = 8
MLA_Q_RANK = 384
MLA_KV_RANK = 256
MLA_NOPE = 64
MLA_ROPE = 32
MLA_V = 64
MLA_W = MLA_HEADS * MLA_V
NA_HEADS = 4
NA_HEAD_DIM = 64
NA_W = NA_HEADS * NA_HEAD_DIM
NA_WIN_R = 8
NA_WIN_C = 16
POOL_WINDOWS = (2, 4, 8, 16)
POOL_GROUP = 64
POOL_WIDTH = POOL_GROUP * len(POOL_WINDOWS)
N_EXPERTS = 16
EC_CAPACITY = 2
N_BRANCH = 3
ROPE_BASE = 10000.0
EPS = 1e-6
IN_SIZES = (MLA_Q_RANK, MLA_KV_RANK, MLA_ROPE, 3 * NA_W, POOL_WIDTH)

LANES = 128
SUBLANES = 8
HEAD_PAD = 128
V_ROWS = MLA_V + 16
QK_EXP2_SCALE = (MLA_NOPE + MLA_ROPE) ** -0.5 * 1.4426950408889634
ROW_TILE = 256
POOL_HALO = 8
GATHER_UNROLL = 8
COMBINE_UNROLL = 4
MOE_ROW_BLOCKS = 4
FLASH_PIECE = 512
MASK_VALUE = -1e30
VMEM_LIMIT = 56 * 1024 * 1024


def _cparams(sem, vmem=VMEM_LIMIT):
    return pltpu.CompilerParams(dimension_semantics=sem, vmem_limit_bytes=vmem)


def _rms(x, g):
    return x * lax.rsqrt(jnp.mean(x * x, axis=-1, keepdims=True) + EPS) * g


def _dot(a, b):
    return jnp.dot(a, b, preferred_element_type=F32)


def _dot_nt(a, b):
    return lax.dot_general(a, b, (((1,), (1,)), ((), ())), preferred_element_type=F32)


def _ada_kernel(c_ref, w_ref, b_ref, o_ref):
    c = c_ref[...]
    s = c * jax.nn.sigmoid(c)
    o_ref[0] = _dot(s.astype(BF16), w_ref[0].astype(BF16)) + b_ref[0]


def _ada(cc, w_ada, b_ada):
    L, D, D6 = w_ada.shape
    tn = 1536
    return pl.pallas_call(
        _ada_kernel,
        grid=(L, D6 // tn),
        in_specs=[
            pl.BlockSpec((8, D), lambda l, j: (0, 0)),
            pl.BlockSpec((1, D, tn), lambda l, j: (l, 0, j)),
            pl.BlockSpec((1, 1, tn), lambda l, j: (l, 0, j)),
        ],
        out_specs=pl.BlockSpec((1, 8, tn), lambda l, j: (l, 0, j)),
        out_shape=jax.ShapeDtypeStruct((L, 8, D6), F32),
        compiler_params=_cparams(("parallel", "parallel")),
        name="ada",
    )(cc, w_ada, b_ada.reshape(L, 1, D6))


def _inproj_kernel(x_ref, sh_ref, sc_ref, g1_ref, wb_ref, gq_ref, gkv_ref, wqm_ref, wqs_ref, wk_ref, wv_ref,
                   vone_ref, cos_ref, sin_ref, qt_ref, k_ref, vt_ref, na_ref, pool_ref):
    x = x_ref[...]
    h = _rms(x, g1_ref[...]) * (1.0 + sc_ref[0]) + sh_ref[0]
    z = _dot(h.astype(BF16), wb_ref[...])
    o = 0
    cq = z[:, o:o + MLA_Q_RANK]; o += MLA_Q_RANK
    ckv = z[:, o:o + MLA_KV_RANK]; o += MLA_KV_RANK
    kr = z[:, o:o + HEAD_PAD]; o += HEAD_PAD
    krs = z[:, o:o + HEAD_PAD]; o += HEAD_PAD
    na_ref[...] = z[:, o:o + 3 * NA_W].astype(BF16); o += 3 * NA_W
    pool_ref[...] = z[:, o:o + POOL_WIDTH]
    cqn = _rms(cq, gq_ref[...]).astype(BF16)
    ckvn = _rms(ckv, gkv_ref[...]).astype(BF16)
    cos = cos_ref[...]
    sin = sin_ref[...]
    cos8 = jnp.concatenate([cos] * MLA_HEADS, axis=1)
    sin8 = jnp.concatenate([sin] * MLA_HEADS, axis=1)
    q = _dot(cqn, wqm_ref[...]) * cos8 + _dot(cqn, wqs_ref[...]) * sin8
    qt_ref[...] = (q * QK_EXP2_SCALE).T.astype(BF16)
    krot = kr * cos + krs * sin
    k = _dot(ckvn, wk_ref[...]) + jnp.concatenate([krot] * MLA_HEADS, axis=1)
    k_ref[...] = k.astype(BF16)
    vt_ref[...] = (_dot(ckvn, wv_ref[...]) + vone_ref[...]).T.astype(BF16)


def _inproj(x_all, mod, g1, wts, cos_t, sin_t, *, n_lat_tiles, tiles_per_batch, n_batch):
    R, D = x_all.shape
    T = ROW_TILE
    nb = wts["wb"].shape[1]

    def seg(i):
        return jnp.minimum(i // tiles_per_batch, n_batch)

    def tab(i):
        return jnp.where(i < n_lat_tiles, i % tiles_per_batch, tiles_per_batch)

    full = lambda a: pl.BlockSpec(a.shape, lambda i: (0,) * a.ndim)
    outs = [
        jax.ShapeDtypeStruct((MLA_HEADS * HEAD_PAD, R), BF16),
        jax.ShapeDtypeStruct((R, MLA_HEADS * HEAD_PAD), BF16),
        jax.ShapeDtypeStruct((MLA_HEADS * V_ROWS, R), BF16),
        jax.ShapeDtypeStruct((R, 3 * NA_W), BF16),
        jax.ShapeDtypeStruct((R, POOL_WIDTH), F32),
    ]
    row_major = lambda s: pl.BlockSpec((T, s.shape[1]), lambda i: (i, 0))
    out_specs = [pl.BlockSpec((MLA_HEADS * HEAD_PAD, T), lambda i: (0, i)), row_major(outs[1]),
                 pl.BlockSpec((MLA_HEADS * V_ROWS, T), lambda i: (0, i)),
                 row_major(outs[3]), row_major(outs[4])]
    return pl.pallas_call(
        _inproj_kernel,
        grid=(R // T,),
        in_specs=[
            pl.BlockSpec((T, D), lambda i: (i, 0)),
            pl.BlockSpec((1, 1, D), lambda i: (seg(i), 0, 0)),
            pl.BlockSpec((1, 1, D), lambda i: (seg(i), 0, 1)),
            full(g1), full(wts["wb"]), full(wts["gq"]), full(wts["gkv"]), full(wts["wqm"]), full(wts["wqs"]),
            full(wts["wk"]), full(wts["wv"]), full(wts["vone"]),
            pl.BlockSpec((T, HEAD_PAD), lambda i: (tab(i), 0)),
            pl.BlockSpec((T, HEAD_PAD), lambda i: (tab(i), 0)),
        ],
        out_specs=out_specs,
        out_shape=outs,
        compiler_params=_cparams(("parallel",)),
        name="inproj",
    )(x_all, mod, mod, g1, wts["wb"], wts["gq"], wts["gkv"], wts["wqm"], wts["wqs"], wts["wk"], wts["wv"],
      wts["vone"], cos_t, sin_t)


def _flash_kernel(*refs, tk, n_chunks):
    if n_chunks:
        q_ref, kc_ref, vtc_ref, kl_ref, vtl_ref, o_ref, m_ref, acc_ref, s_ref = refs
    else:
        q_ref, kc_ref, vtc_ref, o_ref, m_ref, acc_ref = refs
    tq = q_ref.shape[1]
    pw = min(FLASH_PIECE, tq)
    pieces = [(h, slice(pc * pw, (pc + 1) * pw)) for h in range(2) for pc in range(tq // pw)]
    m_ref[...] = jnp.full(m_ref.shape, -jnp.inf, F32)
    acc_ref[...] = jnp.zeros(acc_ref.shape, F32)

    def scores(h, k2, lanes):
        return _dot(k2[:, h * HEAD_PAD:(h + 1) * HEAD_PAD], q_ref[h * HEAD_PAD:(h + 1) * HEAD_PAD, lanes])

    def softmax_pv(h, s, vt, lanes):
        m_old = m_ref[h, :, lanes]
        m_new = jnp.maximum(m_old, jnp.max(s, axis=0, keepdims=True))
        p = jnp.exp2(s - m_new)
        acc_ref[h, :, lanes] = (jnp.exp2(m_old - m_new) * acc_ref[h, :, lanes]
                                + _dot(vt[h * V_ROWS:(h + 1) * V_ROWS, :], p.astype(BF16)))
        m_ref[h, :, lanes] = m_new

    kc = kc_ref[...]
    vtc = vtc_ref[...]
    if not n_chunks:
        for h, lanes in pieces:
            softmax_pv(h, scores(h, kc, lanes), vtc, lanes)
    else:
        def kchunk(i):
            return kl_ref[pl.ds(pl.multiple_of(i * tk, tk), tk), :]

        def vchunk(i):
            return vtl_ref[:, pl.ds(pl.multiple_of(i * tk, tk), tk)]

        k0 = kchunk(0)
        for h, lanes in pieces:
            s_ref[0, h, :, lanes] = scores(h, k0, lanes)
            softmax_pv(h, scores(h, kc, lanes), vtc, lanes)

        def stage(slot_next, k_next, slot_cur, vt_cur):
            for h, lanes in pieces:
                s_ref[slot_next, h, :, lanes] = scores(h, k_next, lanes)
                softmax_pv(h, s_ref[slot_cur, h, :, lanes], vt_cur, lanes)

        def body(i2, carry):
            c0 = 2 * i2
            stage(1, kchunk(c0 + 1), 0, vchunk(c0))
            stage(0, kchunk(jnp.minimum(c0 + 2, n_chunks - 1)), 1, vchunk(c0 + 1))
            return carry
        lax.fori_loop(0, n_chunks // 2, body, 0)
    a0 = acc_ref[0]
    a1 = acc_ref[1]
    o_t = jnp.concatenate([a0[0:MLA_V] / a0[MLA_V:MLA_V + 1], a1[0:MLA_V] / a1[MLA_V:MLA_V + 1]], axis=0)
    o_ref[...] = o_t.T.astype(BF16)


def _flash(q_all, k_all, vt_all, *, n_batch, n_lat, n_ctx, latent):
    ctx_blk0 = (n_batch * n_lat) // n_ctx
    n_pairs = MLA_HEADS // 2
    if latent:
        tq = min(2048, n_lat)
        tk = min(512, n_lat // 2)
        nq = n_lat // tq
        q_map = lambda b, p, i: (p, b * nq + i)
        n_rows = n_batch * n_lat
        n_chunks = n_lat // tk
        assert n_chunks % 2 == 0
    else:
        tq, nq, tk, n_chunks = n_ctx, 1, n_ctx, 0
        q_map = lambda b, p, i: (p, ctx_blk0 + b)
        n_rows = n_batch * n_ctx
    in_specs = [
        pl.BlockSpec((2 * HEAD_PAD, tq), q_map),
        pl.BlockSpec((n_ctx, 2 * HEAD_PAD), lambda b, p, i: (ctx_blk0 + b, p)),
        pl.BlockSpec((2 * V_ROWS, n_ctx), lambda b, p, i: (p, ctx_blk0 + b)),
    ]
    args = [q_all, k_all, vt_all]
    scratch = [pltpu.VMEM((2, 1, tq), F32), pltpu.VMEM((2, V_ROWS, tq), F32)]
    if latent:
        in_specs += [
            pl.BlockSpec((n_lat, 2 * HEAD_PAD), lambda b, p, i: (b, p)),
            pl.BlockSpec((2 * V_ROWS, n_lat), lambda b, p, i: (p, b)),
        ]
        args += [k_all, vt_all]
        scratch.append(pltpu.VMEM((2, 2, tk, tq), F32))
    return pl.pallas_call(
        functools.partial(_flash_kernel, tk=tk, n_chunks=n_chunks),
        grid=(n_batch, n_pairs, nq),
        in_specs=in_specs,
        out_specs=pl.BlockSpec((tq, 2 * MLA_V), lambda b, p, i: (b * nq + i, p)),
        out_shape=jax.ShapeDtypeStruct((n_rows, MLA_W), BF16),
        scratch_shapes=scratch,
        compiler_params=_cparams(("parallel", "parallel", "arbitrary")),
        name="flash_lat" if latent else "flash_ctx",
    )(*args)


NA_QROWS = 8
NA_KROWS = 16


def _na_softmax_pv(parts):
    m = None
    for s, _ in parts:
        mi = jnp.max(s, axis=1, keepdims=True)
        m = mi if m is None else jnp.maximum(m, mi)
    l = None
    o = None
    for s, v in parts:
        p = jnp.exp(s - m)
        li = jnp.sum(p, axis=1, keepdims=True)
        oi = _dot(p.astype(BF16), v)
        l = li if l is None else l + li
        o = oi if o is None else o + oi
    return o / l


def _na_lat_kernel(q_ref, k_ref, v_ref, kc_ref, vc_ref, pairs_ref, o_ref, bias_ref, *, rows, scale, layouts):
    i = pl.program_id(1)
    nblk = rows // NA_QROWS

    def build(variant):
        lay = layouts[variant]
        for h in range(NA_HEADS):
            for qr in range(NA_QROWS):
                for kp in range(NA_KROWS // 2):
                    bias_ref[h, qr * GRID_W:(qr + 1) * GRID_W, kp * LANES:(kp + 1) * LANES] = pairs_ref[h, int(lay[qr, kp])]

    @pl.when(i == 0)
    def _first():
        build(0)

    if nblk > 2:
        @pl.when(i == 1)
        def _interior():
            build(1)

    @pl.when(i == nblk - 1)
    def _last():
        build(2)

    s0 = jnp.clip(NA_QROWS * i - NA_WIN_R // 2, 0, rows - NA_KROWS) * GRID_W
    s0 = pl.multiple_of(s0, 256)
    nk = NA_KROWS * GRID_W
    lo = lax.broadcasted_iota(I32, (1, LANES), 1) < NA_HEAD_DIM
    outs = []
    for pr in range(NA_HEADS // 2):
        cs = slice(pr * LANES, (pr + 1) * LANES)
        qp = q_ref[:, cs]
        kw = k_ref[pl.ds(s0, nk), cs]
        vw = v_ref[pl.ds(s0, nk), cs]
        kc = kc_ref[:, cs]
        vc = vc_ref[:, cs]
        o_pair = None
        for hh in range(2):
            msk = lo if hh == 0 else jnp.logical_not(lo)
            qh = jnp.where(msk, qp, jnp.zeros_like(qp))
            s_lat = _dot_nt(qh, kw) * scale + bias_ref[2 * pr + hh]
            s_ctx = _dot_nt(qh, kc) * scale
            o = _na_softmax_pv([(s_lat, vw), (s_ctx, vc)])
            o_pair = o if hh == 0 else jnp.where(lo, o_pair, o)
        outs.append(o_pair)
    o_ref[...] = jnp.concatenate(outs, axis=1).astype(BF16)


def _na_lat(na_all, rpb, *, n_batch, n_lat, n_ctx):
    rows = n_lat // GRID_W
    nblk = rows // NA_QROWS
    tq = NA_QROWS * GRID_W
    ctx_blk0 = (n_batch * n_lat) // n_ctx
    scale = NA_HEAD_DIM ** -0.5
    layouts, pair_ids = _na_layouts(rows)
    pairs = _na_pair_tables(rpb, pair_ids)

    return pl.pallas_call(
        functools.partial(_na_lat_kernel, rows=rows, scale=scale, layouts=layouts),
        grid=(n_batch, nblk),
        in_specs=[
            pl.BlockSpec((tq, NA_W), lambda b, i: (b * nblk + i, 0)),
            pl.BlockSpec((n_lat, NA_W), lambda b, i: (b, 1)),
            pl.BlockSpec((n_lat, NA_W), lambda b, i: (b, 2)),
            pl.BlockSpec((n_ctx, NA_W), lambda b, i: (ctx_blk0 + b, 1)),
            pl.BlockSpec((n_ctx, NA_W), lambda b, i: (ctx_blk0 + b, 2)),
            pl.BlockSpec(pairs.shape, lambda b, i: (0, 0, 0, 0)),
        ],
        out_specs=pl.BlockSpec((tq, NA_W), lambda b, i: (b * nblk + i, 0)),
        out_shape=jax.ShapeDtypeStruct((n_batch * n_lat, NA_W), BF16),
        scratch_shapes=[pltpu.VMEM((NA_HEADS, tq, NA_KROWS * GRID_W), F32)],
        compiler_params=_cparams(("arbitrary", "arbitrary")),
        name="na_lat",
    )(na_all, na_all, na_all, na_all, na_all, pairs)


def _na_ctx_kernel(q_ref, k_ref, v_ref, o_ref, *, scale):
    lo = lax.broadcasted_iota(I32, (1, LANES), 1) < NA_HEAD_DIM
    outs = []
    for pr in range(NA_HEADS // 2):
        cs = slice(pr * LANES, (pr + 1) * LANES)
        qp = q_ref[:, cs]
        kp = k_ref[:, cs]
        vp = v_ref[:, cs]
        o_pair = None
        for hh in range(2):
            msk = lo if hh == 0 else jnp.logical_not(lo)
            qh = jnp.where(msk, qp, jnp.zeros_like(qp))
            o = _na_softmax_pv([(_dot_nt(qh, kp) * scale, vp)])
            o_pair = o if hh == 0 else jnp.where(lo, o_pair, o)
        outs.append(o_pair)
    o_ref[...] = jnp.concatenate(outs, axis=1).astype(BF16)


def _na_ctx(na_all, *, n_batch, n_lat, n_ctx):
    ctx_blk0 = (n_batch * n_lat) // n_ctx
    return pl.pallas_call(
        functools.partial(_na_ctx_kernel, scale=NA_HEAD_DIM ** -0.5),
        grid=(n_batch,),
        in_specs=[pl.BlockSpec((n_ctx, NA_W), lambda b, j=j: (ctx_blk0 + b, j)) for j in range(3)],
        out_specs=pl.BlockSpec((n_ctx, NA_W), lambda b: (b, 0)),
        out_shape=jax.ShapeDtypeStruct((n_batch * n_ctx, NA_W), BF16),
        compiler_params=_cparams(("parallel",)),
        name="na_ctx",
    )(na_all, na_all, na_all)


NA_MASKED = 2 * NA_WIN_R - 1


def _na_layouts(rows):
    nblk = rows // NA_QROWS
    pair_ids = {}
    layouts = []
    for i in (0, min(1, nblk - 1), nblk - 1):
        s0 = min(max(NA_QROWS * i - NA_WIN_R // 2, 0), rows - NA_KROWS)
        qrow = NA_QROWS * i + np.arange(NA_QROWS)
        krow = s0 + np.arange(NA_KROWS)
        start_r = np.clip(qrow - NA_WIN_R // 2, 0, rows - NA_WIN_R)
        valid_r = (krow[None, :] >= start_r[:, None]) & (krow[None, :] < start_r[:, None] + NA_WIN_R)
        dr = np.where(valid_r, krow[None, :] - qrow[:, None] + (NA_WIN_R - 1), NA_MASKED)
        lay = np.zeros((NA_QROWS, NA_KROWS // 2), np.int64)
        for qr in range(NA_QROWS):
            for kp in range(NA_KROWS // 2):
                key = (int(dr[qr, 2 * kp]), int(dr[qr, 2 * kp + 1]))
                lay[qr, kp] = pair_ids.setdefault(key, len(pair_ids))
        layouts.append(lay)
    return layouts, list(pair_ids)


def _na_pair_tables(rpb, pair_ids):
    qc = np.arange(GRID_W)
    kc = np.arange(GRID_W)
    start_c = np.clip(qc - NA_WIN_C // 2, 0, GRID_W - NA_WIN_C)
    valid_c = (kc[None, :] >= start_c[:, None]) & (kc[None, :] < start_c[:, None] + NA_WIN_C)
    dc = np.clip(kc[None, :] - qc[:, None], -(NA_WIN_C - 1), NA_WIN_C - 1) + (NA_WIN_C - 1)
    onehot = (dc.reshape(1, -1) == np.arange(2 * NA_WIN_C - 1)[:, None]).astype(np.float32)
    by_dr = jnp.einsum("hab,bn->han", rpb.astype(F32), jnp.asarray(onehot), precision=lax.Precision.HIGHEST)
    by_dr = by_dr.reshape(NA_HEADS, NA_MASKED, GRID_W, GRID_W)
    by_dr = jnp.where(jnp.asarray(valid_c)[None, None], by_dr, MASK_VALUE)
    ext = jnp.concatenate([by_dr, jnp.full((NA_HEADS, 1, GRID_W, GRID_W), MASK_VALUE, F32)], axis=1)
    left = np.array([p[0] for p in pair_ids])
    right = np.array([p[1] for p in pair_ids])
    return jnp.concatenate([ext[:, left], ext[:, right]], axis=-1)


def _pool_kernel(prev_ref, cur_ref, next_ref, w_ref, ps_ref, o_ref, scr, *, tiles_per_batch, n_lat_tiles, n_lat, n_ctx):
    i = pl.program_id(0)
    T = ROW_TILE
    H = POOL_HALO
    is_lat = i < n_lat_tiles
    tile_in_seq = jnp.where(is_lat, i % tiles_per_batch, 0)
    n_seq = jnp.where(is_lat, n_lat, n_ctx)
    first = tile_in_seq == 0
    last = jnp.where(is_lat, tile_in_seq == tiles_per_batch - 1, True)
    u = cur_ref[...]
    scr[0:H, :] = jnp.where(first, 0.0, prev_ref[...])
    scr[H:H + T, :] = u
    scr[H + T:H + T + H, :] = jnp.where(last, 0.0, next_ref[...])

    def win(lo, hi):
        acc = None
        for j in range(lo, hi):
            piece = scr[H + j:H + j + T, :]
            acc = piece if acc is None else acc + piece
        return acc

    t = (tile_in_seq * T + lax.broadcasted_iota(I32, (T, 1), 0))
    lane = lax.broadcasted_iota(I32, (1, POOL_WIDTH), 1)
    sums = {}
    acc = win(-1, 1)
    sums[2] = acc
    acc = acc + win(-2, -1) + win(1, 2)
    sums[4] = acc
    acc = acc + win(-4, -2) + win(2, 4)
    sums[8] = acc
    acc = acc + win(-8, -4) + win(4, 8)
    sums[16] = acc
    mean = None
    for gi, w in enumerate(POOL_WINDOWS):
        cnt = (jnp.minimum(t + w // 2, n_seq) - jnp.maximum(t - w // 2, 0)).astype(F32)
        mg = sums[w] / cnt
        mean = mg if mean is None else jnp.where(lane < gi * POOL_GROUP, mean, mg)
    dlt = mean - u
    o_ref[...] = (_dot(dlt.astype(BF16), w_ref[...]) * ps_ref[...]).astype(BF16)


def _pool(u_all, w_bd, pool_scale, *, n_batch, n_lat, n_ctx):
    R = u_all.shape[0]
    T = ROW_TILE
    H = POOL_HALO
    assert n_ctx == T
    tiles_per_batch = n_lat // T
    n_lat_tiles = n_batch * tiles_per_batch
    nb8 = R // H
    return pl.pallas_call(
        functools.partial(_pool_kernel, tiles_per_batch=tiles_per_batch, n_lat_tiles=n_lat_tiles, n_lat=n_lat,
                          n_ctx=n_ctx),
        grid=(R // T,),
        in_specs=[
            pl.BlockSpec((H, POOL_WIDTH), lambda i: (jnp.maximum(i * (T // H) - 1, 0), 0)),
            pl.BlockSpec((T, POOL_WIDTH), lambda i: (i, 0)),
            pl.BlockSpec((H, POOL_WIDTH), lambda i: (jnp.minimum((i + 1) * (T // H), nb8 - 1), 0)),
            pl.BlockSpec((POOL_WIDTH, POOL_WIDTH), lambda i: (0, 0)),
            pl.BlockSpec((1, POOL_WIDTH), lambda i: (0, 0)),
        ],
        out_specs=pl.BlockSpec((T, POOL_WIDTH), lambda i: (i, 0)),
        out_shape=jax.ShapeDtypeStruct((R, POOL_WIDTH), BF16),
        scratch_shapes=[pltpu.VMEM((T + 2 * H, POOL_WIDTH), F32)],
        compiler_params=_cparams(("parallel",)),
        name="pool",
    )(u_all, u_all, u_all, w_bd, pool_scale)


def _merge_kernel(x_ref, sh1_ref, sc1_ref, gt1_ref, sh2_ref, sc2_ref, g1_ref, g2_ref, a_ref, b_ref, p_ref,
                  wg_ref, wa_ref, wb_ref, wp_ref, wo_ref, wrt_ref, xn_ref, h2_ref, afft_ref):
    D = x_ref.shape[1]
    x = x_ref[...]
    h = _rms(x, g1_ref[...]) * (1.0 + sc1_ref[0]) + sh1_ref[0]
    g = jax.nn.sigmoid(_dot(h.astype(BF16), wg_ref[...]))
    m = (g[:, 0:D] * _dot(a_ref[...], wa_ref[...]) + g[:, D:2 * D] * _dot(b_ref[...], wb_ref[...])
         + g[:, 2 * D:3 * D] * _dot(p_ref[...], wp_ref[...]))
    xn = x + gt1_ref[0] * _dot(m.astype(BF16), wo_ref[...])
    xn_ref[...] = xn
    h2 = _rms(xn, g2_ref[...]) * (1.0 + sc2_ref[0]) + sh2_ref[0]
    h2_ref[...] = h2
    logit_t = _dot_nt(wrt_ref[...], h2.astype(BF16))
    ex_t = jnp.exp(logit_t - jnp.max(logit_t, axis=0, keepdims=True))
    afft_ref[...] = ex_t / jnp.sum(ex_t, axis=0, keepdims=True)


def _merge(x_all, mod, g1, g2, a_all, b_all, p_all, wts, *, tiles_per_batch, n_batch):
    R, D = x_all.shape
    T = ROW_TILE

    def seg(i):
        return jnp.minimum(i // tiles_per_batch, n_batch)

    modspec = lambda k: pl.BlockSpec((1, 1, D), lambda i: (seg(i), 0, k))
    full = lambda a: pl.BlockSpec(a.shape, lambda i: (0,) * a.ndim)
    row = lambda w: pl.BlockSpec((T, w), lambda i: (i, 0))
    outs = [
        jax.ShapeDtypeStruct((R, D), F32),
        jax.ShapeDtypeStruct((R, D), F32),
        jax.ShapeDtypeStruct((N_EXPERTS, R), F32),
    ]
    wnames = ("wg", "wa", "wbn", "wp", "wo", "wrt")
    return pl.pallas_call(
        _merge_kernel,
        grid=(R // T,),
        in_specs=[row(D), modspec(0), modspec(1), modspec(2), modspec(3), modspec(4), full(g1), full(g2),
                  row(MLA_W), row(NA_W), row(POOL_WIDTH)] + [full(wts[n]) for n in wnames],
        out_specs=[row(D), row(D), pl.BlockSpec((N_EXPERTS, T), lambda i: (0, i))],
        out_shape=outs,
        compiler_params=_cparams(("parallel",)),
        name="merge",
    )(x_all, mod, mod, mod, mod, mod, g1, g2, a_all, b_all, p_all, *[wts[n] for n in wnames])


def _onehot(mask):
    return jnp.where(mask, 1.0, 0.0).astype(BF16)


def _col_to_row(col):
    m = col.shape[0]
    wide = jnp.broadcast_to(col, (m, LANES))
    if m < LANES:
        wide = jnp.concatenate([wide, jnp.zeros((LANES - m, LANES), col.dtype)], axis=0)
    return wide.T[0:1, 0:m]


def _topk_kernel(aff_ref, idx_ref, gate_ref, wc_ref, affc_ref, cend_ref, off_ref, *, n, cap, row0, jt):
    s = pl.program_id(0)
    nc = n // LANES
    aff = aff_ref[...]
    bits = lax.bitcast_convert_type(aff, I32)

    def search(it, thr):
        cand = thr | jnp.left_shift(jnp.int32(1), 30 - it)
        cnt = jnp.sum((bits >= cand).astype(I32), axis=1, keepdims=True)
        return jnp.where(cnt >= cap, cand, thr)

    thr = lax.fori_loop(0, 31, search, jnp.zeros((N_EXPERTS, 1), I32))
    gt = bits > thr
    eq = bits == thr
    need = (cap - jnp.sum(gt.astype(I32), axis=1, keepdims=True)).astype(F32)
    lane = lax.broadcasted_iota(I32, (1, LANES), 1)
    upper = _onehot(lax.broadcasted_iota(I32, (LANES, LANES), 0) <= lax.broadcasted_iota(I32, (LANES, LANES), 1))

    wc_ref[:, nc:, :] = jnp.zeros((N_EXPERTS, LANES - nc, LANES), F32)
    affc_ref[:, nc:, :] = jnp.zeros((N_EXPERTS, LANES - nc, LANES), F32)
    ties_before = jnp.zeros((N_EXPERTS, 1), F32)
    tot = jnp.zeros((N_EXPERTS, LANES), F32)
    for ch in range(nc):
        sl = slice(ch * LANES, (ch + 1) * LANES)
        tie_rank = _dot(_onehot(eq[:, sl]), upper) + ties_before
        ties_before = tie_rank[:, LANES - 1:LANES]
        sel = gt[:, sl] | (eq[:, sl] & (tie_rank <= need))
        w = _dot(_onehot(sel), upper)
        wc_ref[:, ch, :] = w
        affc_ref[:, ch, :] = aff[:, sl]
        tot = jnp.where(lane == ch, w[:, LANES - 1:LANES], tot)
    cend = _dot(tot.astype(BF16), upper)
    cend_ref[...] = cend
    off_ref[...] = cend - tot
    base = row0 + s * n

    def per_expert(e, carry):
        wc = wc_ref[e].astype(BF16)
        a = affc_ref[e]
        hi = a.astype(BF16)
        rest = a - hi.astype(F32)
        mid = rest.astype(BF16)
        lo = (rest - mid.astype(F32)).astype(BF16)
        cend_e = cend_ref[pl.ds(e, 1), :]
        off_e = off_ref[pl.ds(e, 1), :]
        idx_row = []
        gate_row = []
        for t in range(cap // jt):
            jcol = (t * jt + lax.broadcasted_iota(I32, (jt, 1), 0)).astype(F32)
            cj = jnp.sum((cend_e <= jcol).astype(I32), axis=1, keepdims=True)
            in_chunk = lane == cj
            g = _onehot(in_chunk)
            rank = jcol - jnp.sum(jnp.where(in_chunk, off_e, 0.0), axis=1, keepdims=True)
            pos = jnp.sum((_dot(g, wc) <= rank).astype(I32), axis=1, keepdims=True)
            aff_rows = _dot(g, hi) + _dot(g, mid) + _dot(g, lo)
            gate = jnp.sum(jnp.where(lane == pos, aff_rows, 0.0), axis=1, keepdims=True)
            idx_row.append(_col_to_row(cj * LANES + pos + base))
            gate_row.append(_col_to_row(gate))
        mine = lax.broadcasted_iota(I32, (N_EXPERTS, 1), 0) == e
        idx_all, gate_all = carry
        return (jnp.where(mine, jnp.concatenate(idx_row, axis=1), idx_all),
                jnp.where(mine, jnp.concatenate(gate_row, axis=1), gate_all))

    idx_all, gate_all = lax.fori_loop(0, N_EXPERTS, per_expert, (jnp.zeros((N_EXPERTS, cap), I32),
                                                                 jnp.zeros((N_EXPERTS, cap), F32)))
    idx_ref[0] = idx_all
    gate_ref[0] = gate_all


def _topk(aff_t, *, n_sets, n, cap, row0):
    jt = min(128, cap)
    blk0 = row0 // n
    spec = pl.BlockSpec((1, N_EXPERTS, cap), lambda s: (s, 0, 0))
    idx, gate = pl.pallas_call(
        functools.partial(_topk_kernel, n=n, cap=cap, row0=row0, jt=jt),
        grid=(n_sets,),
        in_specs=[pl.BlockSpec((N_EXPERTS, n), lambda s: (0, blk0 + s))],
        out_specs=[spec, spec],
        out_shape=[jax.ShapeDtypeStruct((n_sets, N_EXPERTS, cap), I32),
                   jax.ShapeDtypeStruct((n_sets, N_EXPERTS, cap), F32)],
        scratch_shapes=[pltpu.VMEM((N_EXPERTS, LANES, LANES), F32), pltpu.VMEM((N_EXPERTS, LANES, LANES), F32),
                        pltpu.VMEM((N_EXPERTS, LANES), F32), pltpu.VMEM((N_EXPERTS, LANES), F32)],
        compiler_params=_cparams(("parallel",)),
        name="topk",
    )(aff_t)
    idx = jnp.transpose(idx, (1, 0, 2)).reshape(N_EXPERTS, n_sets * cap)
    gate = jnp.transpose(gate, (1, 0, 2)).reshape(N_EXPERTS, n_sets * cap)
    return idx, gate


def _moe_kernel(idx_ref, h_hbm, gate_ref, wg_ref, wu_ref, wd_ref, y_ref, xs, xb, sem, *, rows, nf):
    e = pl.program_id(0)
    f = pl.program_id(1)

    def row_copy(j, t):
        return pltpu.make_async_copy(h_hbm.at[t], xs.at[pl.ds(pl.multiple_of(j * SUBLANES, SUBLANES), SUBLANES), :],
                                     sem)

    def issue_gather(expert):
        base = expert * rows

        def issue(j8, carry):
            for r in range(GATHER_UNROLL):
                j = j8 * GATHER_UNROLL + r
                row_copy(j, idx_ref[base + j]).start()
            return carry
        lax.fori_loop(0, rows // GATHER_UNROLL, issue, 0)

    @pl.when(f == 0)
    def _gather():
        @pl.when(e == 0)
        def _first():
            issue_gather(0)

        def wait(j8, carry):
            for r in range(GATHER_UNROLL):
                row_copy(j8 * GATHER_UNROLL + r, 0).wait()
            return carry
        lax.fori_loop(0, rows // GATHER_UNROLL, wait, 0)
        for s in range(SUBLANES):
            xb[:, s * LANES:(s + 1) * LANES] = xs[pl.ds(s, rows, stride=SUBLANES), :].astype(BF16)

        @pl.when(e + 1 < pl.num_programs(0))
        def _next():
            issue_gather(e + 1)
        y_ref[0] = jnp.zeros(y_ref.shape[1:], F32)

    wg = wg_ref[0, 0].astype(BF16)
    wu = wu_ref[0, 0].astype(BF16)
    wd = wd_ref[0, 0].astype(BF16)
    rb = rows // MOE_ROW_BLOCKS
    for b in range(MOE_ROW_BLOCKS):
        xv = xb[b * rb:(b + 1) * rb, :]
        a = _dot(xv, wg)
        u = _dot(xv, wu)
        hmid = (a * jax.nn.sigmoid(a)) * u
        y_ref[0, b * rb:(b + 1) * rb, :] += _dot(hmid.astype(BF16), wd)

    @pl.when(f == nf - 1)
    def _scale():
        y_ref[0] = y_ref[0] * gate_ref[0]


def _moe(idx, gate, h2, w_gate, w_up, w_down, layer):
    E, rows = idx.shape
    _, _, D, F = w_gate.shape
    assert D == SUBLANES * LANES and rows % GATHER_UNROLL == 0 and rows % (16 * MOE_ROW_BLOCKS) == 0
    tf = 256
    nf = F // tf
    grid_spec = pltpu.PrefetchScalarGridSpec(
        num_scalar_prefetch=1,
        grid=(E, nf),
        in_specs=[
            pl.BlockSpec(memory_space=pl.ANY),
            pl.BlockSpec((1, rows, 1), lambda e, f, idx: (e, 0, 0)),
            pl.BlockSpec((1, 1, D, tf), lambda e, f, idx: (layer, e, 0, f)),
            pl.BlockSpec((1, 1, D, tf), lambda e, f, idx: (layer, e, 0, f)),
            pl.BlockSpec((1, 1, tf, D), lambda e, f, idx: (layer, e, f, 0)),
        ],
        out_specs=pl.BlockSpec((1, rows, D), lambda e, f, idx: (e, 0, 0)),
        scratch_shapes=[
            pltpu.VMEM((rows * SUBLANES, LANES), F32),
            pltpu.VMEM((rows, D), BF16),
            pltpu.SemaphoreType.DMA,
        ],
    )
    return pl.pallas_call(
        functools.partial(_moe_kernel, rows=rows, nf=nf),
        grid_spec=grid_spec,
        out_shape=jax.ShapeDtypeStruct((E, rows, D), F32),
        compiler_params=_cparams(("arbitrary", "arbitrary")),
        name="moe",
    )(idx.reshape(-1), h2.reshape(h2.shape[0], SUBLANES, LANES), gate[:, :, None], w_gate, w_up, w_down)


def _combine_kernel(idx_ref, y_hbm, x_ref, g2_ref, fg_ref, o_ref, slab, win_ref, sems, *, tc, cap, set_col0, row0,
                    tiles_per_set, win, chunk, final):
    t_id = pl.program_id(0)
    e = pl.program_id(1)
    n_t = pl.num_programs(0)
    n_e = pl.num_programs(1)
    rows = y_hbm.shape[1]
    step = t_id * n_e + e
    slot = step % 2

    def window(tile, expert):
        col0 = expert * rows + set_col0 + (tile // tiles_per_set) * cap

        def lower_bound(v):
            def body(_, lh):
                lo, hi = lh
                mid = (lo + hi) // 2
                less = idx_ref[col0 + jnp.minimum(mid, cap - 1)] < v
                active = lo < hi
                return (jnp.where(active & less, mid + 1, lo), jnp.where(active & jnp.logical_not(less), mid, hi))
            return lax.fori_loop(0, cap.bit_length() + 1, body, (jnp.int32(0), jnp.int32(cap)))[0]

        first = col0 - expert * rows
        r0 = first + lower_bound(row0 + tile * tc)
        r1 = first + lower_bound(row0 + (tile + 1) * tc)
        w0 = jnp.minimum((r0 // SUBLANES) * SUBLANES, rows - win)
        return r0, r1, w0

    def chunk_copies(expert, r0, r1, w0, sl):
        w0 = pl.multiple_of(w0, SUBLANES)
        return [(pltpu.make_async_copy(y_hbm.at[expert, pl.ds(w0 + k * chunk, chunk), :],
                                       slab.at[sl, pl.ds(k * chunk, chunk), :], sems.at[sl]),
                 (w0 + k * chunk < r1) & (w0 + (k + 1) * chunk > r0)) for k in range(win // chunk)]

    def fetch(tile, expert, sl):
        r0, r1, w0 = window(tile, expert)
        win_ref[sl, 0] = r0
        win_ref[sl, 1] = r1
        win_ref[sl, 2] = w0
        for copy, needed in chunk_copies(expert, r0, r1, w0, sl):
            @pl.when(needed)
            def _start(copy=copy):
                copy.start()

    @pl.when(step == 0)
    def _first():
        fetch(t_id, e, slot)

    r0 = win_ref[slot, 0]
    r1 = win_ref[slot, 1]
    w0 = win_ref[slot, 2]
    for copy, needed in chunk_copies(e, r0, r1, w0, slot):
        @pl.when(needed)
        def _wait(copy=copy):
            copy.wait()

    @pl.when(step + 1 < n_t * n_e)
    def _prefetch():
        wrap = e == n_e - 1
        fetch(jnp.where(wrap, t_id + 1, t_id), jnp.where(wrap, 0, e + 1), 1 - slot)

    @pl.when(e == 0)
    def _init():
        o_ref[...] = jnp.zeros(o_ref.shape, F32)

    row_base = e * rows
    tile_start = row0 + t_id * tc

    def add_rows(j0, n):
        ts = [idx_ref[row_base + j0 + r] - tile_start for r in range(n)]
        cur = [o_ref[pl.ds(t, 1), :] for t in ts]
        new = [slab[slot, pl.ds(j0 + r - w0, 1), :] for r in range(n)]
        for r in range(n):
            o_ref[pl.ds(ts[r], 1), :] = cur[r] + new[r]

    n_groups = (r1 - r0) // COMBINE_UNROLL

    def add_group(g, carry):
        add_rows(r0 + g * COMBINE_UNROLL, COMBINE_UNROLL)
        return carry
    lax.fori_loop(0, n_groups, add_group, 0)

    def add_tail(j, carry):
        add_rows(j, 1)
        return carry
    lax.fori_loop(r0 + n_groups * COMBINE_UNROLL, r1, add_tail, 0)

    @pl.when(e == n_e - 1)
    def _finish():
        out = x_ref[...] + g2_ref[0] * o_ref[...]
        if final:
            out = _rms(out, fg_ref[...])
        o_ref[...] = out


def _combine(idx, y, x_all, mod, final_g, *, n_sets, n, cap, set_col0, row0, n_batch, latent, final):
    E, rows, D = y.shape
    tc = min(2048, n)
    tiles_per_set = n // tc
    chunk = min(256, tc, cap)
    win = min(tc, cap) + chunk
    assert rows >= win and (rows - win) % SUBLANES == 0
    blk0 = row0 // tc
    if latent:
        seg = lambda t: t // tiles_per_set
    else:
        seg = lambda t: n_batch
    grid_spec = pltpu.PrefetchScalarGridSpec(
        num_scalar_prefetch=1,
        grid=(n_sets * tiles_per_set, E),
        in_specs=[
            pl.BlockSpec(memory_space=pl.ANY),
            pl.BlockSpec((tc, D), lambda t, e, idx: (blk0 + t, 0)),
            pl.BlockSpec((1, 1, D), lambda t, e, idx: (seg(t), 0, 5)),
            pl.BlockSpec((1, D), lambda t, e, idx: (0, 0)),
        ],
        out_specs=pl.BlockSpec((tc, D), lambda t, e, idx: (t, 0)),
        scratch_shapes=[pltpu.VMEM((2, win, D), F32), pltpu.SMEM((2, 3), I32), pltpu.SemaphoreType.DMA((2,))],
    )
    return pl.pallas_call(
        functools.partial(_combine_kernel, tc=tc, cap=cap, set_col0=set_col0, row0=row0,
                          tiles_per_set=tiles_per_set, win=win, chunk=chunk, final=final),
        grid_spec=grid_spec,
        out_shape=jax.ShapeDtypeStruct((n_sets * n, D), F32),
        compiler_params=_cparams(("arbitrary", "arbitrary")),
        name="combine_lat" if latent else "combine_ctx",
    )(idx.reshape(-1), y, x_all, mod, final_g)


def _rope_tables(n_lat, n_ctx):
    half = MLA_ROPE // 2
    quarter = half // 2
    inv = ROPE_BASE ** (-jnp.arange(quarter, dtype=F32) / quarter)
    t = jnp.arange(n_lat)
    ang_r = (t // GRID_W).astype(F32)[:, None] * inv
    ang_c = (t % GRID_W).astype(F32)[:, None] * inv
    cos32 = jnp.concatenate([jnp.cos(ang_r), jnp.cos(ang_r), jnp.cos(ang_c), jnp.cos(ang_c)], axis=1)
    sin32 = jnp.concatenate([-jnp.sin(ang_r), jnp.sin(ang_r), -jnp.sin(ang_c), jnp.sin(ang_c)], axis=1)
    pad = HEAD_PAD - MLA_NOPE - MLA_ROPE
    cos_l = jnp.concatenate([jnp.ones((n_lat, MLA_NOPE), F32), cos32, jnp.zeros((n_lat, pad), F32)], axis=1)
    sin_l = jnp.concatenate([jnp.zeros((n_lat, MLA_NOPE), F32), sin32, jnp.zeros((n_lat, pad), F32)], axis=1)
    cos_c = jnp.concatenate([jnp.ones((ROW_TILE, MLA_NOPE + MLA_ROPE), F32), jnp.zeros((ROW_TILE, pad), F32)], axis=1)
    sin_c = jnp.zeros((ROW_TILE, HEAD_PAD), F32)
    return jnp.concatenate([cos_l, cos_c]), jnp.concatenate([sin_l, sin_c])


def _rope_partner():
    half = MLA_ROPE // 2
    quarter = half // 2
    p = jnp.arange(MLA_ROPE)
    return (p // half) * half + (p % half + quarter) % half


def _layer_weights(l, w_in, mla_q_norm, mla_kv_norm, w_uq, w_ukv, w_pool, w_br_mla, w_br_na, w_br_pool, w_out,
                   w_router):
    D = w_in.shape[1]
    offs = [0]
    for s in IN_SIZES:
        offs.append(offs[-1] + s)
    wi = w_in[l]
    w_cq, w_ckv, w_kr, w_na, w_pl = [wi[:, offs[k]:offs[k + 1]] for k in range(5)]
    w_gt = wi[:, offs[5]:]
    partner = _rope_partner()
    zl = jnp.zeros((D, MLA_NOPE), F32)
    zr = jnp.zeros((D, HEAD_PAD - MLA_NOPE - MLA_ROPE), F32)
    w_kr_p = jnp.concatenate([zl, w_kr, zr], axis=1)
    w_krs_p = jnp.concatenate([zl, w_kr[:, partner], zr], axis=1)
    wb = jnp.concatenate([w_cq, w_ckv, w_kr_p, w_krs_p, w_na, w_pl], axis=1).astype(BF16)

    dq = MLA_NOPE + MLA_ROPE
    uq = w_uq[l].reshape(MLA_Q_RANK, MLA_HEADS, dq)
    zq = jnp.zeros((MLA_Q_RANK, MLA_HEADS, HEAD_PAD - dq), F32)
    wqm = jnp.concatenate([uq, zq], axis=2).reshape(MLA_Q_RANK, MLA_HEADS * HEAD_PAD).astype(BF16)
    wqs = jnp.concatenate([jnp.zeros((MLA_Q_RANK, MLA_HEADS, MLA_NOPE), F32), uq[:, :, MLA_NOPE:][:, :, partner], zq],
                          axis=2).reshape(MLA_Q_RANK, MLA_HEADS * HEAD_PAD).astype(BF16)
    ukv = w_ukv[l].reshape(MLA_KV_RANK, MLA_HEADS, MLA_NOPE + MLA_V)
    wk = jnp.concatenate([ukv[:, :, :MLA_NOPE], jnp.zeros((MLA_KV_RANK, MLA_HEADS, HEAD_PAD - MLA_NOPE), F32)],
                         axis=2).reshape(MLA_KV_RANK, MLA_HEADS * HEAD_PAD).astype(BF16)
    wv = jnp.concatenate([ukv[:, :, MLA_NOPE:], jnp.zeros((MLA_KV_RANK, MLA_HEADS, V_ROWS - MLA_V), F32)],
                         axis=2).reshape(MLA_KV_RANK, MLA_HEADS * V_ROWS).astype(BF16)
    vone = jnp.tile(jnp.concatenate([jnp.zeros((MLA_V,), F32), jnp.ones((V_ROWS - MLA_V,), F32)]), MLA_HEADS)[None]

    w_bd = jnp.zeros((POOL_WIDTH, POOL_WIDTH), F32)
    for g in range(len(POOL_WINDOWS)):
        w_bd = w_bd.at[g * POOL_GROUP:(g + 1) * POOL_GROUP, g * POOL_GROUP:(g + 1) * POOL_GROUP].set(w_pool[l, g])
    return dict(
        wb=wb, gq=mla_q_norm[l][None], gkv=mla_kv_norm[l][None], wqm=wqm, wqs=wqs, wk=wk, wv=wv, vone=vone,
        w_bd=w_bd.astype(BF16), wg=w_gt.astype(BF16), wa=w_br_mla[l].astype(BF16), wbn=w_br_na[l].astype(BF16),
        wp=w_br_pool[l].astype(BF16), wo=w_out[l].astype(BF16), wrt=w_router[l].T.astype(BF16))


def kernel(x, c, ctx, c_ctx, norm1_g, norm2_g, w_ada, b_ada, w_in, mla_q_norm, mla_kv_norm, w_uq, w_ukv, na_rpb, w_pool, pool_scale, w_br_mla, w_br_na, w_br_pool, w_out, w_router, w_gate, w_up, w_down, final_g):
    B, N, D = x.shape
    Cx = ctx.shape[1]
    L = w_in.shape[0]
    E = N_EXPERTS
    T = ROW_TILE
    assert Cx == T and N % (NA_KROWS * GRID_W) == 0 and B + 1 <= 8
    tiles_per_batch = N // T
    n_lat_tiles = B * tiles_per_batch
    cap_l = EC_CAPACITY * N // E
    cap_c = EC_CAPACITY * Cx // E

    x_all = jnp.concatenate([x.reshape(B * N, D), ctx.reshape(B * Cx, D)], axis=0)
    cc = jnp.concatenate([c, c_ctx[None], jnp.zeros((8 - B - 1, D), F32)], axis=0)
    mod_all = _ada(cc, w_ada, b_ada).reshape(L, 8, 1, 6 * D)
    cos_t, sin_t = _rope_tables(N, Cx)
    dims = dict(n_batch=B, n_lat=N, n_ctx=Cx)

    out = None
    for l in range(L):
        last = l == L - 1
        wts = _layer_weights(l, w_in, mla_q_norm, mla_kv_norm, w_uq, w_ukv, w_pool, w_br_mla, w_br_na, w_br_pool,
                             w_out, w_router)
        mod = mod_all[l]
        g1 = norm1_g[l][None]
        g2 = norm2_g[l][None]
        q_all, k_all, vt_all, na_all, pool_u = _inproj(x_all, mod, g1, wts, cos_t, sin_t, n_lat_tiles=n_lat_tiles,
                                                      tiles_per_batch=tiles_per_batch, n_batch=B)
        a_l = _flash(q_all, k_all, vt_all,latent=True, **dims)
        b_l = _na_lat(na_all, na_rpb[l], **dims)
        p_all = _pool(pool_u, wts["w_bd"], pool_scale[l][None], **dims)
        if not last:
            a_c = _flash(q_all, k_all, vt_all,latent=False, **dims)
            b_c = _na_ctx(na_all, **dims)
        else:
            a_c = jnp.zeros((B * Cx, MLA_W), BF16)
            b_c = jnp.zeros((B * Cx, NA_W), BF16)
        a_all = jnp.concatenate([a_l, a_c], axis=0)
        b_all = jnp.concatenate([b_l, b_c], axis=0)
        xn_all, h2_all, aff_t = _merge(x_all, mod, g1, g2, a_all, b_all, p_all, wts, tiles_per_batch=tiles_per_batch,
                                       n_batch=B)

        idx, gate = _topk(aff_t, n_sets=B, n=N, cap=cap_l, row0=0)
        if not last:
            idx_c, gate_c = _topk(aff_t, n_sets=B, n=Cx, cap=cap_c, row0=B * N)
            idx = jnp.concatenate([idx, idx_c], axis=1)
            gate = jnp.concatenate([gate, gate_c], axis=1)
        y = _moe(idx, gate, h2_all, w_gate, w_up, w_down, l)
        fg = final_g[None]
        xl_new = _combine(idx, y, xn_all, mod, fg, n_sets=B, n=N, cap=cap_l, set_col0=0, row0=0, n_batch=B,
                          latent=True, final=last)
        if last:
            out = xl_new.reshape(B, N, D)
        else:
            xc_new = _combine(idx, y, xn_all, mod, fg, n_sets=B, n=Cx, cap=cap_c, set_col0=B * cap_l, row0=B * N,
                              n_batch=B, latent=False, final=False)
            x_all = jnp.concatenate([xl_new, xc_new], axis=0)
    return out
```

```python
import functools

import numpy as np
import jax
import jax.numpy as jnp
from jax import lax
from jax.experimental import pallas as pl
from jax.experimental.pallas import tpu as pltpu

F32 = jnp.float32
BF16 = jnp.bfloat16
I32 = jnp.int32

GRID_W = 64
MLA_HEADS = 8
MLA_Q_RANK = 384
MLA_KV_RANK = 256
MLA_NOPE = 64
MLA_ROPE = 32
MLA_V = 64
MLA_W = MLA_HEADS * MLA_V
NA_HEADS = 4
NA_HEAD_DIM = 64
NA_W = NA_HEADS * NA_HEAD_DIM
NA_WIN_R = 8
NA_WIN_C = 16
POOL_WINDOWS = (2, 4, 8, 16)
POOL_GROUP = 64
POOL_WIDTH = POOL_GROUP * len(POOL_WINDOWS)
N_EXPERTS = 16
EC_CAPACITY = 2
N_BRANCH = 3
ROPE_BASE = 10000.0
EPS = 1e-6
IN_SIZES = (MLA_Q_RANK, MLA_KV_RANK, MLA_ROPE, 3 * NA_W, POOL_WIDTH)

LANES = 128
SUBLANES = 8
HEAD_PAD = 128
V_ROWS = MLA_V + 16
QK_EXP2_SCALE = (MLA_NOPE + MLA_ROPE) ** -0.5 * 1.4426950408889634
ROW_TILE = 256
POOL_HALO = 8
GATHER_UNROLL = 8
COMBINE_UNROLL = 4
MOE_ROW_BLOCKS = 4
FLASH_PIECE = 512
MASK_VALUE = -1e30
VMEM_LIMIT = 56 * 1024 * 1024


def _cparams(sem, vmem=VMEM_LIMIT):
    return pltpu.CompilerParams(dimension_semantics=sem, vmem_limit_bytes=vmem)


def _rms(x, g):
    return x * lax.rsqrt(jnp.mean(x * x, axis=-1, keepdims=True) + EPS) * g


def _dot(a, b):
    return jnp.dot(a, b, preferred_element_type=F32)


def _dot_nt(a, b):
    return lax.dot_general(a, b, (((1,), (1,)), ((), ())), preferred_element_type=F32)


def _ada_kernel(c_ref, w_ref, b_ref, o_ref):
    c = c_ref[...]
    s = c * jax.nn.sigmoid(c)
    o_ref[0] = _dot(s.astype(BF16), w_ref[0].astype(BF16)) + b_ref[0]


def _ada(cc, w_ada, b_ada):
    L, D, D6 = w_ada.shape
    tn = 1536
    return pl.pallas_call(
        _ada_kernel,
        grid=(L, D6 // tn),
        in_specs=[
            pl.BlockSpec((8, D), lambda l, j: (0, 0)),
            pl.BlockSpec((1, D, tn), lambda l, j: (l, 0, j)),
            pl.BlockSpec((1, 1, tn), lambda l, j: (l, 0, j)),
        ],
        out_specs=pl.BlockSpec((1, 8, tn), lambda l, j: (l, 0, j)),
        out_shape=jax.ShapeDtypeStruct((L, 8, D6), F32),
        compiler_params=_cparams(("parallel", "parallel")),
        name="ada",
    )(cc, w_ada, b_ada.reshape(L, 1, D6))


def _inproj_kernel(x_ref, sh_ref, sc_ref, g1_ref, wb_ref, gq_ref, gkv_ref, wqm_ref, wqs_ref, wk_ref, wv_ref,
                   vone_ref, cos_ref, sin_ref, qt_ref, k_ref, vt_ref, na_ref, pool_ref):
    x = x_ref[...]
    h = _rms(x, g1_ref[...]) * (1.0 + sc_ref[0]) + sh_ref[0]
    z = _dot(h.astype(BF16), wb_ref[...])
    o = 0
    cq = z[:, o:o + MLA_Q_RANK]; o += MLA_Q_RANK
    ckv = z[:, o:o + MLA_KV_RANK]; o += MLA_KV_RANK
    kr = z[:, o:o + HEAD_PAD]; o += HEAD_PAD
    krs = z[:, o:o + HEAD_PAD]; o += HEAD_PAD
    na_ref[...] = z[:, o:o + 3 * NA_W].astype(BF16); o += 3 * NA_W
    pool_ref[...] = z[:, o:o + POOL_WIDTH]
    cqn = _rms(cq, gq_ref[...]).astype(BF16)
    ckvn = _rms(ckv, gkv_ref[...]).astype(BF16)
    cos = cos_ref[...]
    sin = sin_ref[...]
    cos8 = jnp.concatenate([cos] * MLA_HEADS, axis=1)
    sin8 = jnp.concatenate([sin] * MLA_HEADS, axis=1)
    q = _dot(cqn, wqm_ref[...]) * cos8 + _dot(cqn, wqs_ref[...]) * sin8
    qt_ref[...] = (q * QK_EXP2_SCALE).T.astype(BF16)
    krot = kr * cos + krs * sin
    k = _dot(ckvn, wk_ref[...]) + jnp.concatenate([krot] * MLA_HEADS, axis=1)
    k_ref[...] = k.astype(BF16)
    vt_ref[...] = (_dot(ckvn, wv_ref[...]) + vone_ref[...]).T.astype(BF16)


def _inproj(x_all, mod, g1, wts, cos_t, sin_t, *, n_lat_tiles, tiles_per_batch, n_batch):
    R, D = x_all.shape
    T = ROW_TILE
    nb = wts["wb"].shape[1]

    def seg(i):
        return jnp.minimum(i // tiles_per_batch, n_batch)

    def tab(i):
        return jnp.where(i < n_lat_tiles, i % tiles_per_batch, tiles_per_batch)

    full = lambda a: pl.BlockSpec(a.shape, lambda i: (0,) * a.ndim)
    outs = [
        jax.ShapeDtypeStruct((MLA_HEADS * HEAD_PAD, R), BF16),
        jax.ShapeDtypeStruct((R, MLA_HEADS * HEAD_PAD), BF16),
        jax.ShapeDtypeStruct((MLA_HEADS * V_ROWS, R), BF16),
        jax.ShapeDtypeStruct((R, 3 * NA_W), BF16),
        jax.ShapeDtypeStruct((R, POOL_WIDTH), F32),
    ]
    row_major = lambda s: pl.BlockSpec((T, s.shape[1]), lambda i: (i, 0))
    out_specs = [pl.BlockSpec((MLA_HEADS * HEAD_PAD, T), lambda i: (0, i)), row_major(outs[1]),
                 pl.BlockSpec((MLA_HEADS * V_ROWS, T), lambda i: (0, i)),
                 row_major(outs[3]), row_major(outs[4])]
    return pl.pallas_call(
        _inproj_kernel,
        grid=(R // T,),
        in_specs=[
            pl.BlockSpec((T, D), lambda i: (i, 0)),
            pl.BlockSpec((1, 1, D), lambda i: (seg(i), 0, 0)),
            pl.BlockSpec((1, 1, D), lambda i: (seg(i), 0, 1)),
            full(g1), full(wts["wb"]), full(wts["gq"]), full(wts["gkv"]), full(wts["wqm"]), full(wts["wqs"]),
            full(wts["wk"]), full(wts["wv"]), full(wts["vone"]),
            pl.BlockSpec((T, HEAD_PAD), lambda i: (tab(i), 0)),
            pl.BlockSpec((T, HEAD_PAD), lambda i: (tab(i), 0)),
        ],
        out_specs=out_specs,
        out_shape=outs,
        compiler_params=_cparams(("parallel",)),
        name="inproj",
    )(x_all, mod, mod, g1, wts["wb"], wts["gq"], wts["gkv"], wts["wqm"], wts["wqs"], wts["wk"], wts["wv"],
      wts["vone"], cos_t, sin_t)


def _flash_kernel(*refs, tk, n_chunks):
    if n_chunks:
        q_ref, kc_ref, vtc_ref, kl_ref, vtl_ref, o_ref, m_ref, acc_ref, s_ref = refs
    else:
        q_ref, kc_ref, vtc_ref, o_ref, m_ref, acc_ref = refs
    tq = q_ref.shape[1]
    pw = min(FLASH_PIECE, tq)
    pieces = [(h, slice(pc * pw, (pc + 1) * pw)) for h in range(2) for pc in range(tq // pw)]
    m_ref[...] = jnp.full(m_ref.shape, -jnp.inf, F32)
    acc_ref[...] = jnp.zeros(acc_ref.shape, F32)

    def scores(h, k2, lanes):
        return _dot(k2[:, h * HEAD_PAD:(h + 1) * HEAD_PAD], q_ref[h * HEAD_PAD:(h + 1) * HEAD_PAD, lanes])

    def softmax_pv(h, s, vt, lanes):
        m_old = m_ref[h, :, lanes]
        m_new = jnp.maximum(m_old, jnp.max(s, axis=0, keepdims=True))
        p = jnp.exp2(s - m_new)
        acc_ref[h, :, lanes] = (jnp.exp2(m_old - m_new) * acc_ref[h, :, lanes]
                                + _dot(vt[h * V_ROWS:(h + 1) * V_ROWS, :], p.astype(BF16)))
        m_ref[h, :, lanes] = m_new

    kc = kc_ref[...]
    vtc = vtc_ref[...]
    if not n_chunks:
        for h, lanes in pieces:
            softmax_pv(h, scores(h, kc, lanes), vtc, lanes)
    else:
        def kchunk(i):
            return kl_ref[pl.ds(pl.multiple_of(i * tk, tk), tk), :]

        def vchunk(i):
            return vtl_ref[:, pl.ds(pl.multiple_of(i * tk, tk), tk)]

        k0 = kchunk(0)
        for h, lanes in pieces:
            s_ref[0, h, :, lanes] = scores(h, k0, lanes)
            softmax_pv(h, scores(h, kc, lanes), vtc, lanes)

        def stage(slot_next, k_next, slot_cur, vt_cur):
            for h, lanes in pieces:
                s_ref[slot_next, h, :, lanes] = scores(h, k_next, lanes)
                softmax_pv(h, s_ref[slot_cur, h, :, lanes], vt_cur, lanes)

        def body(i2, carry):
            c0 = 2 * i2
            stage(1, kchunk(c0 + 1), 0, vchunk(c0))
            stage(0, kchunk(jnp.minimum(c0 + 2, n_chunks - 1)), 1, vchunk(c0 + 1))
            return carry
        lax.fori_loop(0, n_chunks // 2, body, 0)
    a0 = acc_ref[0]
    a1 = acc_ref[1]
    o_t = jnp.concatenate([a0[0:MLA_V] / a0[MLA_V:MLA_V + 1], a1[0:MLA_V] / a1[MLA_V:MLA_V + 1]], axis=0)
    o_ref[...] = o_t.T.astype(BF16)


def _flash(q_all, k_all, vt_all, *, n_batch, n_lat, n_ctx, latent):
    ctx_blk0 = (n_batch * n_lat) // n_ctx
    n_pairs = MLA_HEADS // 2
    if latent:
        tq = min(2048, n_lat)
        tk = min(512, n_lat // 2)
        nq = n_lat // tq
        q_map = lambda b, p, i: (p, b * nq + i)
        n_rows = n_batch * n_lat
        n_chunks = n_lat // tk
        assert n_chunks % 2 == 0
    else:
        tq, nq, tk, n_chunks = n_ctx, 1, n_ctx, 0
        q_map = lambda b, p, i: (p, ctx_blk0 + b)
        n_rows = n_batch * n_ctx
    in_specs = [
        pl.BlockSpec((2 * HEAD_PAD, tq), q_map),
        pl.BlockSpec((n_ctx, 2 * HEAD_PAD), lambda b, p, i: (ctx_blk0 + b, p)),
        pl.BlockSpec((2 * V_ROWS, n_ctx), lambda b, p, i: (p, ctx_blk0 + b)),
    ]
    args = [q_all, k_all, vt_all]
    scratch = [pltpu.VMEM((2, 1, tq), F32), pltpu.VMEM((2, V_ROWS, tq), F32)]
    if latent:
        in_specs += [
            pl.BlockSpec((n_lat, 2 * HEAD_PAD), lambda b, p, i: (b, p)),
            pl.BlockSpec((2 * V_ROWS, n_lat), lambda b, p, i: (p, b)),
        ]
        args += [k_all, vt_all]
        scratch.append(pltpu.VMEM((2, 2, tk, tq), F32))
    return pl.pallas_call(
        functools.partial(_flash_kernel, tk=tk, n_chunks=n_chunks),
        grid=(n_batch, n_pairs, nq),
        in_specs=in_specs,
        out_specs=pl.BlockSpec((tq, 2 * MLA_V), lambda b, p, i: (b * nq + i, p)),
        out_shape=jax.ShapeDtypeStruct((n_rows, MLA_W), BF16),
        scratch_shapes=scratch,
        compiler_params=_cparams(("parallel", "parallel", "arbitrary")),
        name="flash_lat" if latent else "flash_ctx",
    )(*args)


NA_QROWS = 8
NA_KROWS = 16


def _na_softmax_pv(parts):
    m = None
    for s, _ in parts:
        mi = jnp.max(s, axis=1, keepdims=True)
        m = mi if m is None else jnp.maximum(m, mi)
    l = None
    o = None
    for s, v in parts:
        p = jnp.exp(s - m)
        li = jnp.sum(p, axis=1, keepdims=True)
        oi = _dot(p.astype(BF16), v)
        l = li if l is None else l + li
        o = oi if o is None else o + oi
    return o / l


def _na_lat_kernel(q_ref, k_ref, v_ref, kc_ref, vc_ref, pairs_ref, o_ref, bias_ref, *, rows, scale, layouts):
    i = pl.program_id(1)
    nblk = rows // NA_QROWS

    def build(variant):
        lay = layouts[variant]
        for h in range(NA_HEADS):
            for qr in range(NA_QROWS):
                for kp in range(NA_KROWS // 2):
                    bias_ref[h, qr * GRID_W:(qr + 1) * GRID_W, kp * LANES:(kp + 1) * LANES] = pairs_ref[h, int(lay[qr, kp])]

    @pl.when(i == 0)
    def _first():
        build(0)

    if nblk > 2:
        @pl.when(i == 1)
        def _interior():
            build(1)

    @pl.when(i == nblk - 1)
    def _last():
        build(2)

    s0 = jnp.clip(NA_QROWS * i - NA_WIN_R // 2, 0, rows - NA_KROWS) * GRID_W
    s0 = pl.multiple_of(s0, 256)
    nk = NA_KROWS * GRID_W
    lo = lax.broadcasted_iota(I32, (1, LANES), 1) < NA_HEAD_DIM
    outs = []
    for pr in range(NA_HEADS // 2):
        cs = slice(pr * LANES, (pr + 1) * LANES)
        qp = q_ref[:, cs]
        kw = k_ref[pl.ds(s0, nk), cs]
        vw = v_ref[pl.ds(s0, nk), cs]
        kc = kc_ref[:, cs]
        vc = vc_ref[:, cs]
        o_pair = None
        for hh in range(2):
            msk = lo if hh == 0 else jnp.logical_not(lo)
            qh = jnp.where(msk, qp, jnp.zeros_like(qp))
            s_lat = _dot_nt(qh, kw) * scale + bias_ref[2 * pr + hh]
            s_ctx = _dot_nt(qh, kc) * scale
            o = _na_softmax_pv([(s_lat, vw), (s_ctx, vc)])
            o_pair = o if hh == 0 else jnp.where(lo, o_pair, o)
        outs.append(o_pair)
    o_ref[...] = jnp.concatenate(outs, axis=1).astype(BF16)


def _na_lat(na_all, rpb, *, n_batch, n_lat, n_ctx):
    rows = n_lat // GRID_W
    nblk = rows // NA_QROWS
    tq = NA_QROWS * GRID_W
    ctx_blk0 = (n_batch * n_lat) // n_ctx
    scale = NA_HEAD_DIM ** -0.5
    layouts, pair_ids = _na_layouts(rows)
    pairs = _na_pair_tables(rpb, pair_ids)

    return pl.pallas_call(
        functools.partial(_na_lat_kernel, rows=rows, scale=scale, layouts=layouts),
        grid=(n_batch, nblk),
        in_specs=[
            pl.BlockSpec((tq, NA_W), lambda b, i: (b * nblk + i, 0)),
            pl.BlockSpec((n_lat, NA_W), lambda b, i: (b, 1)),
            pl.BlockSpec((n_lat, NA_W), lambda b, i: (b, 2)),
            pl.BlockSpec((n_ctx, NA_W), lambda b, i: (ctx_blk0 + b, 1)),
            pl.BlockSpec((n_ctx, NA_W), lambda b, i: (ctx_blk0 + b, 2)),
            pl.BlockSpec(pairs.shape, lambda b, i: (0, 0, 0, 0)),
        ],
        out_specs=pl.BlockSpec((tq, NA_W), lambda b, i: (b * nblk + i, 0)),
        out_shape=jax.ShapeDtypeStruct((n_batch * n_lat, NA_W), BF16),
        scratch_shapes=[pltpu.VMEM((NA_HEADS, tq, NA_KROWS * GRID_W), F32)],
        compiler_params=_cparams(("arbitrary", "arbitrary")),
        name="na_lat",
    )(na_all, na_all, na_all, na_all, na_all, pairs)


def _na_ctx_kernel(q_ref, k_ref, v_ref, o_ref, *, scale):
    lo = lax.broadcasted_iota(I32, (1, LANES), 1) < NA_HEAD_DIM
    outs = []
    for pr in range(NA_HEADS // 2):
        cs = slice(pr * LANES, (pr + 1) * LANES)
        qp = q_ref[:, cs]
        kp = k_ref[:, cs]
        vp = v_ref[:, cs]
        o_pair = None
        for hh in range(2):
            msk = lo if hh == 0 else jnp.logical_not(lo)
            qh = jnp.where(msk, qp, jnp.zeros_like(qp))
            o = _na_softmax_pv([(_dot_nt(qh, kp) * scale, vp)])
            o_pair = o if hh == 0 else jnp.where(lo, o_pair, o)
        outs.append(o_pair)
    o_ref[...] = jnp.concatenate(outs, axis=1).astype(BF16)


def _na_ctx(na_all, *, n_batch, n_lat, n_ctx):
    ctx_blk0 = (n_batch * n_lat) // n_ctx
    return pl.pallas_call(
        functools.partial(_na_ctx_kernel, scale=NA_HEAD_DIM ** -0.5),
        grid=(n_batch,),
        in_specs=[pl.BlockSpec((n_ctx, NA_W), lambda b, j=j: (ctx_blk0 + b, j)) for j in range(3)],
        out_specs=pl.BlockSpec((n_ctx, NA_W), lambda b: (b, 0)),
        out_shape=jax.ShapeDtypeStruct((n_batch * n_ctx, NA_W), BF16),
        compiler_params=_cparams(("parallel",)),
        name="na_ctx",
    )(na_all, na_all, na_all)


NA_MASKED = 2 * NA_WIN_R - 1


def _na_layouts(rows):
    nblk = rows // NA_QROWS
    pair_ids = {}
    layouts = []
    for i in (0, min(1, nblk - 1), nblk - 1):
        s0 = min(max(NA_QROWS * i - NA_WIN_R // 2, 0), rows - NA_KROWS)
        qrow = NA_QROWS * i + np.arange(NA_QROWS)
        krow = s0 + np.arange(NA_KROWS)
        start_r = np.clip(qrow - NA_WIN_R // 2, 0, rows - NA_WIN_R)
        valid_r = (krow[None, :] >= start_r[:, None]) & (krow[None, :] < start_r[:, None] + NA_WIN_R)
        dr = np.where(valid_r, krow[None, :] - qrow[:, None] + (NA_WIN_R - 1), NA_MASKED)
        lay = np.zeros((NA_QROWS, NA_KROWS // 2), np.int64)
        for qr in range(NA_QROWS):
            for kp in range(NA_KROWS // 2):
                key = (int(dr[qr, 2 * kp]), int(dr[qr, 2 * kp + 1]))
                lay[qr, kp] = pair_ids.setdefault(key, len(pair_ids))
        layouts.append(lay)
    return layouts, list(pair_ids)


def _na_pair_tables(rpb, pair_ids):
    qc = np.arange(GRID_W)
    kc = np.arange(GRID_W)
    start_c = np.clip(qc - NA_WIN_C // 2, 0, GRID_W - NA_WIN_C)
    valid_c = (kc[None, :] >= start_c[:, None]) & (kc[None, :] < start_c[:, None] + NA_WIN_C)
    dc = np.clip(kc[None, :] - qc[:, None], -(NA_WIN_C - 1), NA_WIN_C - 1) + (NA_WIN_C - 1)
    onehot = (dc.reshape(1, -1) == np.arange(2 * NA_WIN_C - 1)[:, None]).astype(np.float32)
    by_dr = jnp.einsum("hab,bn->han", rpb.astype(F32), jnp.asarray(onehot), precision=lax.Precision.HIGHEST)
    by_dr = by_dr.reshape(NA_HEADS, NA_MASKED, GRID_W, GRID_W)
    by_dr = jnp.where(jnp.asarray(valid_c)[None, None], by_dr, MASK_VALUE)
    ext = jnp.concatenate([by_dr, jnp.full((NA_HEADS, 1, GRID_W, GRID_W), MASK_VALUE, F32)], axis=1)
    left = np.array([p[0] for p in pair_ids])
    right = np.array([p[1] for p in pair_ids])
    return jnp.concatenate([ext[:, left], ext[:, right]], axis=-1)


def _pool_kernel(prev_ref, cur_ref, next_ref, w_ref, ps_ref, o_ref, scr, *, tiles_per_batch, n_lat_tiles, n_lat, n_ctx):
    i = pl.program_id(0)
    T = ROW_TILE
    H = POOL_HALO
    is_lat = i < n_lat_tiles
    tile_in_seq = jnp.where(is_lat, i % tiles_per_batch, 0)
    n_seq = jnp.where(is_lat, n_lat, n_ctx)
    first = tile_in_seq == 0
    last = jnp.where(is_lat, tile_in_seq == tiles_per_batch - 1, True)
    u = cur_ref[...]
    scr[0:H, :] = jnp.where(first, 0.0, prev_ref[...])
    scr[H:H + T, :] = u
    scr[H + T:H + T + H, :] = jnp.where(last, 0.0, next_ref[...])

    def win(lo, hi):
        acc = None
        for j in range(lo, hi):
            piece = scr[H + j:H + j + T, :]
            acc = piece if acc is None else acc + piece
        return acc

    t = (tile_in_seq * T + lax.broadcasted_iota(I32, (T, 1), 0))
    lane = lax.broadcasted_iota(I32, (1, POOL_WIDTH), 1)
    sums = {}
    acc = win(-1, 1)
    sums[2] = acc
    acc = acc + win(-2, -1) + win(1, 2)
    sums[4] = acc
    acc = acc + win(-4, -2) + win(2, 4)
    sums[8] = acc
    acc = acc + win(-8, -4) + win(4, 8)
    sums[16] = acc
    mean = None
    for gi, w in enumerate(POOL_WINDOWS):
        cnt = (jnp.minimum(t + w // 2, n_seq) - jnp.maximum(t - w // 2, 0)).astype(F32)
        mg = sums[w] / cnt
        mean = mg if mean is None else jnp.where(lane < gi * POOL_GROUP, mean, mg)
    dlt = mean - u
    o_ref[...] = (_dot(dlt.astype(BF16), w_ref[...]) * ps_ref[...]).astype(BF16)


def _pool(u_all, w_bd, pool_scale, *, n_batch, n_lat, n_ctx):
    R = u_all.shape[0]
    T = ROW_TILE
    H = POOL_HALO
    assert n_ctx == T
    tiles_per_batch = n_lat // T
    n_lat_tiles = n_batch * tiles_per_batch
    nb8 = R // H
    return pl.pallas_call(
        functools.partial(_pool_kernel, tiles_per_batch=tiles_per_batch, n_lat_tiles=n_lat_tiles, n_lat=n_lat,
                          n_ctx=n_ctx),
        grid=(R // T,),
        in_specs=[
            pl.BlockSpec((H, POOL_WIDTH), lambda i: (jnp.maximum(i * (T // H) - 1, 0), 0)),
            pl.BlockSpec((T, POOL_WIDTH), lambda i: (i, 0)),
            pl.BlockSpec((H, POOL_WIDTH), lambda i: (jnp.minimum((i + 1) * (T // H), nb8 - 1), 0)),
            pl.BlockSpec((POOL_WIDTH, POOL_WIDTH), lambda i: (0, 0)),
            pl.BlockSpec((1, POOL_WIDTH), lambda i: (0, 0)),
        ],
        out_specs=pl.BlockSpec((T, POOL_WIDTH), lambda i: (i, 0)),
        out_shape=jax.ShapeDtypeStruct((R, POOL_WIDTH), BF16),
        scratch_shapes=[pltpu.VMEM((T + 2 * H, POOL_WIDTH), F32)],
        compiler_params=_cparams(("parallel",)),
        name="pool",
    )(u_all, u_all, u_all, w_bd, pool_scale)


def _merge_kernel(x_ref, sh1_ref, sc1_ref, gt1_ref, sh2_ref, sc2_ref, g1_ref, g2_ref, a_ref, b_ref, p_ref,
                  wg_ref, wa_ref, wb_ref, wp_ref, wo_ref, wrt_ref, xn_ref, h2_ref, afft_ref):
    D = x_ref.shape[1]
    x = x_ref[...]
    h = _rms(x, g1_ref[...]) * (1.0 + sc1_ref[0]) + sh1_ref[0]
    g = jax.nn.sigmoid(_dot(h.astype(BF16), wg_ref[...]))
    m = (g[:, 0:D] * _dot(a_ref[...], wa_ref[...]) + g[:, D:2 * D] * _dot(b_ref[...], wb_ref[...])
         + g[:, 2 * D:3 * D] * _dot(p_ref[...], wp_ref[...]))
    xn = x + gt1_ref[0] * _dot(m.astype(BF16), wo_ref[...])
    xn_ref[...] = xn
    h2 = _rms(xn, g2_ref[...]) * (1.0 + sc2_ref[0]) + sh2_ref[0]
    h2_ref[...] = h2
    logit_t = _dot_nt(wrt_ref[...], h2.astype(BF16))
    ex_t = jnp.exp(logit_t - jnp.max(logit_t, axis=0, keepdims=True))
    afft_ref[...] = ex_t / jnp.sum(ex_t, axis=0, keepdims=True)


def _merge(x_all, mod, g1, g2, a_all, b_all, p_all, wts, *, tiles_per_batch, n_batch):
    R, D = x_all.shape
    T = ROW_TILE

    def seg(i):
        return jnp.minimum(i // tiles_per_batch, n_batch)

    modspec = lambda k: pl.BlockSpec((1, 1, D), lambda i: (seg(i), 0, k))
    full = lambda a: pl.BlockSpec(a.shape, lambda i: (0,) * a.ndim)
    row = lambda w: pl.BlockSpec((T, w), lambda i: (i, 0))
    outs = [
        jax.ShapeDtypeStruct((R, D), F32),
        jax.ShapeDtypeStruct((R, D), F32),
        jax.ShapeDtypeStruct((N_EXPERTS, R), F32),
    ]
    wnames = ("wg", "wa", "wbn", "wp", "wo", "wrt")
    return pl.pallas_call(
        _merge_kernel,
        grid=(R // T,),
        in_specs=[row(D), modspec(0), modspec(1), modspec(2), modspec(3), modspec(4), full(g1), full(g2),
                  row(MLA_W), row(NA_W), row(POOL_WIDTH)] + [full(wts[n]) for n in wnames],
        out_specs=[row(D), row(D), pl.BlockSpec((N_EXPERTS, T), lambda i: (0, i))],
        out_shape=outs,
        compiler_params=_cparams(("parallel",)),
        name="merge",
    )(x_all, mod, mod, mod, mod, mod, g1, g2, a_all, b_all, p_all, *[wts[n] for n in wnames])


def _onehot(mask):
    return jnp.where(mask, 1.0, 0.0).astype(BF16)


def _col_to_row(col):
    m = col.shape[0]
    wide = jnp.broadcast_to(col, (m, LANES))
    if m < LANES:
        wide = jnp.concatenate([wide, jnp.zeros((LANES - m, LANES), col.dtype)], axis=0)
    return wide.T[0:1, 0:m]


def _topk_kernel(aff_ref, idx_ref, gate_ref, wc_ref, affc_ref, cend_ref, off_ref, *, n, cap, row0, jt):
    s = pl.program_id(0)
    nc = n // LANES
    aff = aff_ref[...]
    bits = lax.bitcast_convert_type(aff, I32)

    def search(it, thr):
        cand = thr | jnp.left_shift(jnp.int32(1), 30 - it)
        cnt = jnp.sum((bits >= cand).astype(I32), axis=1, keepdims=True)
        return jnp.where(cnt >= cap, cand, thr)

    thr = lax.fori_loop(0, 31, search, jnp.zeros((N_EXPERTS, 1), I32))
    gt = bits > thr
    eq = bits == thr
    need = (cap - jnp.sum(gt.astype(I32), axis=1, keepdims=True)).astype(F32)
    lane = lax.broadcasted_iota(I32, (1, LANES), 1)
    upper = _onehot(lax.broadcasted_iota(I32, (LANES, LANES), 0) <= lax.broadcasted_iota(I32, (LANES, LANES), 1))

    wc_ref[:, nc:, :] = jnp.zeros((N_EXPERTS, LANES - nc, LANES), F32)
    affc_ref[:, nc:, :] = jnp.zeros((N_EXPERTS, LANES - nc, LANES), F32)
    ties_before = jnp.zeros((N_EXPERTS, 1), F32)
    tot = jnp.zeros((N_EXPERTS, LANES), F32)
    for ch in range(nc):
        sl = slice(ch * LANES, (ch + 1) * LANES)
        tie_rank = _dot(_onehot(eq[:, sl]), upper) + ties_before
        ties_before = tie_rank[:, LANES - 1:LANES]
        sel = gt[:, sl] | (eq[:, sl] & (tie_rank <= need))
        w = _dot(_onehot(sel), upper)
        wc_ref[:, ch, :] = w
        affc_ref[:, ch, :] = aff[:, sl]
        tot = jnp.where(lane == ch, w[:, LANES - 1:LANES], tot)
    cend = _dot(tot.astype(BF16), upper)
    cend_ref[...] = cend
    off_ref[...] = cend - tot
    base = row0 + s * n

    def per_expert(e, carry):
        wc = wc_ref[e].astype(BF16)
        a = affc_ref[e]
        hi = a.astype(BF16)
        rest = a - hi.astype(F32)
        mid = rest.astype(BF16)
        lo = (rest - mid.astype(F32)).astype(BF16)
        cend_e = cend_ref[pl.ds(e, 1), :]
        off_e = off_ref[pl.ds(e, 1), :]
        idx_row = []
        gate_row = []
        for t in range(cap // jt):
            jcol = (t * jt + lax.broadcasted_iota(I32, (jt, 1), 0)).astype(F32)
            cj = jnp.sum((cend_e <= jcol).astype(I32), axis=1, keepdims=True)
            in_chunk = lane == cj
            g = _onehot(in_chunk)
            rank = jcol - jnp.sum(jnp.where(in_chunk, off_e, 0.0), axis=1, keepdims=True)
            pos = jnp.sum((_dot(g, wc) <= rank).astype(I32), axis=1, keepdims=True)
            aff_rows = _dot(g, hi) + _dot(g, mid) + _dot(g, lo)
            gate = jnp.sum(jnp.where(lane == pos, aff_rows, 0.0), axis=1, keepdims=True)
            idx_row.append(_col_to_row(cj * LANES + pos + base))
            gate_row.append(_col_to_row(gate))
        mine = lax.broadcasted_iota(I32, (N_EXPERTS, 1), 0) == e
        idx_all, gate_all = carry
        return (jnp.where(mine, jnp.concatenate(idx_row, axis=1), idx_all),
                jnp.where(mine, jnp.concatenate(gate_row, axis=1), gate_all))

    idx_all, gate_all = lax.fori_loop(0, N_EXPERTS, per_expert, (jnp.zeros((N_EXPERTS, cap), I32),
                                                                 jnp.zeros((N_EXPERTS, cap), F32)))
    idx_ref[0] = idx_all
    gate_ref[0] = gate_all


def _topk(aff_t, *, n_sets, n, cap, row0):
    jt = min(128, cap)
    blk0 = row0 // n
    spec = pl.BlockSpec((1, N_EXPERTS, cap), lambda s: (s, 0, 0))
    idx, gate = pl.pallas_call(
        functools.partial(_topk_kernel, n=n, cap=cap, row0=row0, jt=jt),
        grid=(n_sets,),
        in_specs=[pl.BlockSpec((N_EXPERTS, n), lambda s: (0, blk0 + s))],
        out_specs=[spec, spec],
        out_shape=[jax.ShapeDtypeStruct((n_sets, N_EXPERTS, cap), I32),
                   jax.ShapeDtypeStruct((n_sets, N_EXPERTS, cap), F32)],
        scratch_shapes=[pltpu.VMEM((N_EXPERTS, LANES, LANES), F32), pltpu.VMEM((N_EXPERTS, LANES, LANES), F32),
                        pltpu.VMEM((N_EXPERTS, LANES), F32), pltpu.VMEM((N_EXPERTS, LANES), F32)],
        compiler_params=_cparams(("parallel",)),
        name="topk",
    )(aff_t)
    idx = jnp.transpose(idx, (1, 0, 2)).reshape(N_EXPERTS, n_sets * cap)
    gate = jnp.transpose(gate, (1, 0, 2)).reshape(N_EXPERTS, n_sets * cap)
    return idx, gate


def _moe_kernel(idx_ref, h_hbm, gate_ref, wg_ref, wu_ref, wd_ref, y_ref, xs, xb, yacc, sem, *, rows, nf):
    e = pl.program_id(0)
    f = pl.program_id(1)

    def row_copy(j, t):
        return pltpu.make_async_copy(h_hbm.at[t], xs.at[pl.ds(pl.multiple_of(j * SUBLANES, SUBLANES), SUBLANES), :],
                                     sem)

    def issue_gather(expert):
        base = expert * rows

        def issue(j8, carry):
            for r in range(GATHER_UNROLL):
                j = j8 * GATHER_UNROLL + r
                row_copy(j, idx_ref[base + j]).start()
            return carry
        lax.fori_loop(0, rows // GATHER_UNROLL, issue, 0)

    @pl.when(f == 0)
    def _gather():
        @pl.when(e == 0)
        def _first():
            issue_gather(0)

        def wait(j8, carry):
            for r in range(GATHER_UNROLL):
                row_copy(j8 * GATHER_UNROLL + r, 0).wait()
            return carry
        lax.fori_loop(0, rows // GATHER_UNROLL, wait, 0)
        for s in range(SUBLANES):
            xb[:, s * LANES:(s + 1) * LANES] = xs[pl.ds(s, rows, stride=SUBLANES), :].astype(BF16)

        @pl.when(e + 1 < pl.num_programs(0))
        def _next():
            issue_gather(e + 1)
        yacc[...] = jnp.zeros(yacc.shape, F32)

    wg = wg_ref[0, 0].astype(BF16)
    wu = wu_ref[0, 0].astype(BF16)
    wd = wd_ref[0, 0].astype(BF16)
    rb = rows // MOE_ROW_BLOCKS
    for b in range(MOE_ROW_BLOCKS):
        xv = xb[b * rb:(b + 1) * rb, :]
        a = _dot(xv, wg)
        u = _dot(xv, wu)
        hmid = (a * jax.nn.sigmoid(a)) * u
        yacc[b * rb:(b + 1) * rb, :] += _dot(hmid.astype(BF16), wd)

    @pl.when(f == nf - 1)
    def _scale():
        g = gate_ref[0]
        for s in range(SUBLANES):
            y_ref[0, pl.ds(s, rows, stride=SUBLANES), :] = yacc[:, s * LANES:(s + 1) * LANES] * g


def _moe(idx, gate, h2, w_gate, w_up, w_down, layer):
    E, rows = idx.shape
    _, _, D, F = w_gate.shape
    assert D == SUBLANES * LANES and rows % GATHER_UNROLL == 0 and rows % (16 * MOE_ROW_BLOCKS) == 0
    tf = 256
    nf = F // tf
    grid_spec = pltpu.PrefetchScalarGridSpec(
        num_scalar_prefetch=1,
        grid=(E, nf),
        in_specs=[
            pl.BlockSpec(memory_space=pl.ANY),
            pl.BlockSpec((1, rows, 1), lambda e, f, idx: (e, 0, 0)),
            pl.BlockSpec((1, 1, D, tf), lambda e, f, idx: (layer, e, 0, f)),
            pl.BlockSpec((1, 1, D, tf), lambda e, f, idx: (layer, e, 0, f)),
            pl.BlockSpec((1, 1, tf, D), lambda e, f, idx: (layer, e, f, 0)),
        ],
        out_specs=pl.BlockSpec((1, rows * SUBLANES, LANES), lambda e, f, idx: (e, 0, 0)),
        scratch_shapes=[
            pltpu.VMEM((rows * SUBLANES, LANES), F32),
            pltpu.VMEM((rows, D), BF16),
            pltpu.VMEM((rows, D), F32),
            pltpu.SemaphoreType.DMA,
        ],
    )
    return pl.pallas_call(
        functools.partial(_moe_kernel, rows=rows, nf=nf),
        grid_spec=grid_spec,
        out_shape=jax.ShapeDtypeStruct((E, rows * SUBLANES, LANES), F32),
        compiler_params=_cparams(("arbitrary", "arbitrary")),
        name="moe",
    )(idx.reshape(-1), h2.reshape(h2.shape[0], SUBLANES, LANES), gate[:, :, None], w_gate, w_up, w_down)


def _combine_kernel(idx_ref, y_hbm, x_ref, g2_ref, fg_ref, o_ref, slab, oacc, win_ref, sems, *, tc, cap, set_col0, row0,
                    tiles_per_set, win, chunk, final):
    t_id = pl.program_id(0)
    e = pl.program_id(1)
    n_t = pl.num_programs(0)
    n_e = pl.num_programs(1)
    rows = y_hbm.shape[1] // SUBLANES
    step = t_id * n_e + e
    slot = step % 2

    def window(tile, expert):
        col0 = expert * rows + set_col0 + (tile // tiles_per_set) * cap

        def lower_bound(v):
            def body(_, lh):
                lo, hi = lh
                mid = (lo + hi) // 2
                less = idx_ref[col0 + jnp.minimum(mid, cap - 1)] < v
                active = lo < hi
                return (jnp.where(active & less, mid + 1, lo), jnp.where(active & jnp.logical_not(less), mid, hi))
            return lax.fori_loop(0, cap.bit_length() + 1, body, (jnp.int32(0), jnp.int32(cap)))[0]

        first = col0 - expert * rows
        r0 = first + lower_bound(row0 + tile * tc)
        r1 = first + lower_bound(row0 + (tile + 1) * tc)
        w0 = jnp.minimum(r0, rows - win)
        return r0, r1, w0

    def chunk_copies(expert, r0, r1, w0, sl):
        src0 = pl.multiple_of(w0 * SUBLANES, SUBLANES)
        return [(pltpu.make_async_copy(y_hbm.at[expert, pl.ds(src0 + k * chunk * SUBLANES, chunk * SUBLANES), :],
                                       slab.at[sl, pl.ds(k * chunk * SUBLANES, chunk * SUBLANES), :], sems.at[sl]),
                 (w0 + k * chunk < r1) & (w0 + (k + 1) * chunk > r0)) for k in range(win // chunk)]

    def fetch(tile, expert, sl):
        r0, r1, w0 = window(tile, expert)
        win_ref[sl, 0] = r0
        win_ref[sl, 1] = r1
        win_ref[sl, 2] = w0
        for copy, needed in chunk_copies(expert, r0, r1, w0, sl):
            @pl.when(needed)
            def _start(copy=copy):
                copy.start()

    @pl.when(step == 0)
    def _first():
        fetch(t_id, e, slot)

    r0 = win_ref[slot, 0]
    r1 = win_ref[slot, 1]
    w0 = win_ref[slot, 2]
    for copy, needed in chunk_copies(e, r0, r1, w0, slot):
        @pl.when(needed)
        def _wait(copy=copy):
            copy.wait()

    @pl.when(step + 1 < n_t * n_e)
    def _prefetch():
        wrap = e == n_e - 1
        fetch(jnp.where(wrap, t_id + 1, t_id), jnp.where(wrap, 0, e + 1), 1 - slot)

    @pl.when(e == 0)
    def _init():
        oacc[...] = jnp.zeros(oacc.shape, F32)

    row_base = e * rows
    tile_start = row0 + t_id * tc

    def add_rows(j0, n):
        ts = [pl.multiple_of((idx_ref[row_base + j0 + r] - tile_start) * SUBLANES, SUBLANES) for r in range(n)]
        cur = [oacc[pl.ds(t, SUBLANES), :] for t in ts]
        new = [slab[slot, pl.ds(pl.multiple_of((j0 + r - w0) * SUBLANES, SUBLANES), SUBLANES), :] for r in range(n)]
        for r in range(n):
            oacc[pl.ds(ts[r], SUBLANES), :] = cur[r] + new[r]

    n_groups = (r1 - r0) // COMBINE_UNROLL

    def add_group(g, carry):
        add_rows(r0 + g * COMBINE_UNROLL, COMBINE_UNROLL)
        return carry
    lax.fori_loop(0, n_groups, add_group, 0)

    def add_tail(j, carry):
        add_rows(j, 1)
        return carry
    lax.fori_loop(r0 + n_groups * COMBINE_UNROLL, r1, add_tail, 0)

    @pl.when(e == n_e - 1)
    def _finish():
        for s in range(SUBLANES):
            lanes = slice(s * LANES, (s + 1) * LANES)
            o_ref[:, lanes] = x_ref[:, lanes] + g2_ref[0][:, lanes] * oacc[pl.ds(s, tc, stride=SUBLANES), :]
        if final:
            o_ref[...] = _rms(o_ref[...], fg_ref[...])


def _combine(idx, y, x_all, mod, final_g, *, n_sets, n, cap, set_col0, row0, n_batch, latent, final):
    E, rows8, _ = y.shape
    rows = rows8 // SUBLANES
    D = x_all.shape[1]
    tc = min(2048, n)
    tiles_per_set = n // tc
    chunk = min(256, tc, cap)
    win = min(tc, cap)
    assert rows >= win and win % chunk == 0
    blk0 = row0 // tc
    if latent:
        seg = lambda t: t // tiles_per_set
    else:
        seg = lambda t: n_batch
    grid_spec = pltpu.PrefetchScalarGridSpec(
        num_scalar_prefetch=1,
        grid=(n_sets * tiles_per_set, E),
        in_specs=[
            pl.BlockSpec(memory_space=pl.ANY),
            pl.BlockSpec((tc, D), lambda t, e, idx: (blk0 + t, 0)),
            pl.BlockSpec((1, 1, D), lambda t, e, idx: (seg(t), 0, 5)),
            pl.BlockSpec((1, D), lambda t, e, idx: (0, 0)),
        ],
        out_specs=pl.BlockSpec((tc, D), lambda t, e, idx: (t, 0)),
        scratch_shapes=[pltpu.VMEM((2, win * SUBLANES, LANES), F32), pltpu.VMEM((tc * SUBLANES, LANES), F32),
                        pltpu.SMEM((2, 3), I32), pltpu.SemaphoreType.DMA((2,))],
    )
    return pl.pallas_call(
        functools.partial(_combine_kernel, tc=tc, cap=cap, set_col0=set_col0, row0=row0,
                          tiles_per_set=tiles_per_set, win=win, chunk=chunk, final=final),
        grid_spec=grid_spec,
        out_shape=jax.ShapeDtypeStruct((n_sets * n, D), F32),
        compiler_params=_cparams(("arbitrary", "arbitrary")),
        name="combine_lat" if latent else "combine_ctx",
    )(idx.reshape(-1), y, x_all, mod, final_g)


def _rope_tables(n_lat, n_ctx):
    half = MLA_ROPE // 2
    quarter = half // 2
    inv = ROPE_BASE ** (-jnp.arange(quarter, dtype=F32) / quarter)
    t = jnp.arange(n_lat)
    ang_r = (t // GRID_W).astype(F32)[:, None] * inv
    ang_c = (t % GRID_W).astype(F32)[:, None] * inv
    cos32 = jnp.concatenate([jnp.cos(ang_r), jnp.cos(ang_r), jnp.cos(ang_c), jnp.cos(ang_c)], axis=1)
    sin32 = jnp.concatenate([-jnp.sin(ang_r), jnp.sin(ang_r), -jnp.sin(ang_c), jnp.sin(ang_c)], axis=1)
    pad = HEAD_PAD - MLA_NOPE - MLA_ROPE
    cos_l = jnp.concatenate([jnp.ones((n_lat, MLA_NOPE), F32), cos32, jnp.zeros((n_lat, pad), F32)], axis=1)
    sin_l = jnp.concatenate([jnp.zeros((n_lat, MLA_NOPE), F32), sin32, jnp.zeros((n_lat, pad), F32)], axis=1)
    cos_c = jnp.concatenate([jnp.ones((ROW_TILE, MLA_NOPE + MLA_ROPE), F32), jnp.zeros((ROW_TILE, pad), F32)], axis=1)
    sin_c = jnp.zeros((ROW_TILE, HEAD_PAD), F32)
    return jnp.concatenate([cos_l, cos_c]), jnp.concatenate([sin_l, sin_c])


def _rope_partner():
    half = MLA_ROPE // 2
    quarter = half // 2
    p = jnp.arange(MLA_ROPE)
    return (p // half) * half + (p % half + quarter) % half


def _layer_weights(l, w_in, mla_q_norm, mla_kv_norm, w_uq, w_ukv, w_pool, w_br_mla, w_br_na, w_br_pool, w_out,
                   w_router):
    D = w_in.shape[1]
    offs = [0]
    for s in IN_SIZES:
        offs.append(offs[-1] + s)
    wi = w_in[l]
    w_cq, w_ckv, w_kr, w_na, w_pl = [wi[:, offs[k]:offs[k + 1]] for k in range(5)]
    w_gt = wi[:, offs[5]:]
    partner = _rope_partner()
    zl = jnp.zeros((D, MLA_NOPE), F32)
    zr = jnp.zeros((D, HEAD_PAD - MLA_NOPE - MLA_ROPE), F32)
    w_kr_p = jnp.concatenate([zl, w_kr, zr], axis=1)
    w_krs_p = jnp.concatenate([zl, w_kr[:, partner], zr], axis=1)
    wb = jnp.concatenate([w_cq, w_ckv, w_kr_p, w_krs_p, w_na, w_pl], axis=1).astype(BF16)

    dq = MLA_NOPE + MLA_ROPE
    uq = w_uq[l].reshape(MLA_Q_RANK, MLA_HEADS, dq)
    zq = jnp.zeros((MLA_Q_RANK, MLA_HEADS, HEAD_PAD - dq), F32)
    wqm = jnp.concatenate([uq, zq], axis=2).reshape(MLA_Q_RANK, MLA_HEADS * HEAD_PAD).astype(BF16)
    wqs = jnp.concatenate([jnp.zeros((MLA_Q_RANK, MLA_HEADS, MLA_NOPE), F32), uq[:, :, MLA_NOPE:][:, :, partner], zq],
                          axis=2).reshape(MLA_Q_RANK, MLA_HEADS * HEAD_PAD).astype(BF16)
    ukv = w_ukv[l].reshape(MLA_KV_RANK, MLA_HEADS, MLA_NOPE + MLA_V)
    wk = jnp.concatenate([ukv[:, :, :MLA_NOPE], jnp.zeros((MLA_KV_RANK, MLA_HEADS, HEAD_PAD - MLA_NOPE), F32)],
                         axis=2).reshape(MLA_KV_RANK, MLA_HEADS * HEAD_PAD).astype(BF16)
    wv = jnp.concatenate([ukv[:, :, MLA_NOPE:], jnp.zeros((MLA_KV_RANK, MLA_HEADS, V_ROWS - MLA_V), F32)],
                         axis=2).reshape(MLA_KV_RANK, MLA_HEADS * V_ROWS).astype(BF16)
    vone = jnp.tile(jnp.concatenate([jnp.zeros((MLA_V,), F32), jnp.ones((V_ROWS - MLA_V,), F32)]), MLA_HEADS)[None]

    w_bd = jnp.zeros((POOL_WIDTH, POOL_WIDTH), F32)
    for g in range(len(POOL_WINDOWS)):
        w_bd = w_bd.at[g * POOL_GROUP:(g + 1) * POOL_GROUP, g * POOL_GROUP:(g + 1) * POOL_GROUP].set(w_pool[l, g])
    return dict(
        wb=wb, gq=mla_q_norm[l][None], gkv=mla_kv_norm[l][None], wqm=wqm, wqs=wqs, wk=wk, wv=wv, vone=vone,
        w_bd=w_bd.astype(BF16), wg=w_gt.astype(BF16), wa=w_br_mla[l].astype(BF16), wbn=w_br_na[l].astype(BF16),
        wp=w_br_pool[l].astype(BF16), wo=w_out[l].astype(BF16), wrt=w_router[l].T.astype(BF16))


def kernel(x, c, ctx, c_ctx, norm1_g, norm2_g, w_ada, b_ada, w_in, mla_q_norm, mla_kv_norm, w_uq, w_ukv, na_rpb, w_pool, pool_scale, w_br_mla, w_br_na, w_br_pool, w_out, w_router, w_gate, w_up, w_down, final_g):
    B, N, D = x.shape
    Cx = ctx.shape[1]
    L = w_in.shape[0]
    E = N_EXPERTS
    T = ROW_TILE
    assert Cx == T and N % (NA_KROWS * GRID_W) == 0 and B + 1 <= 8
    tiles_per_batch = N // T
    n_lat_tiles = B * tiles_per_batch
    cap_l = EC_CAPACITY * N // E
    cap_c = EC_CAPACITY * Cx // E

    x_all = jnp.concatenate([x.reshape(B * N, D), ctx.reshape(B * Cx, D)], axis=0)
    cc = jnp.concatenate([c, c_ctx[None], jnp.zeros((8 - B - 1, D), F32)], axis=0)
    mod_all = _ada(cc, w_ada, b_ada).reshape(L, 8, 1, 6 * D)
    cos_t, sin_t = _rope_tables(N, Cx)
    dims = dict(n_batch=B, n_lat=N, n_ctx=Cx)

    out = None
    for l in range(L):
        last = l == L - 1
        wts = _layer_weights(l, w_in, mla_q_norm, mla_kv_norm, w_uq, w_ukv, w_pool, w_br_mla, w_br_na, w_br_pool,
                             w_out, w_router)
        mod = mod_all[l]
        g1 = norm1_g[l][None]
        g2 = norm2_g[l][None]
        q_all, k_all, vt_all, na_all, pool_u = _inproj(x_all, mod, g1, wts, cos_t, sin_t, n_lat_tiles=n_lat_tiles,
                                                      tiles_per_batch=tiles_per_batch, n_batch=B)
        a_l = _flash(q_all, k_all, vt_all,latent=True, **dims)
        b_l = _na_lat(na_all, na_rpb[l], **dims)
        p_all = _pool(pool_u, wts["w_bd"], pool_scale[l][None], **dims)
        if not last:
            a_c = _flash(q_all, k_all, vt_all,latent=False, **dims)
            b_c = _na_ctx(na_all, **dims)
        else:
            a_c = jnp.zeros((B * Cx, MLA_W), BF16)
            b_c = jnp.zeros((B * Cx, NA_W), BF16)
        a_all = jnp.concatenate([a_l, a_c], axis=0)
        b_all = jnp.concatenate([b_l, b_c], axis=0)
        xn_all, h2_all, aff_t = _merge(x_all, mod, g1, g2, a_all, b_all, p_all, wts, tiles_per_batch=tiles_per_batch,
                                       n_batch=B)

        idx, gate = _topk(aff_t, n_sets=B, n=N, cap=cap_l, row0=0)
        if not last:
            idx_c, gate_c = _topk(aff_t, n_sets=B, n=Cx, cap=cap_c, row0=B * N)
            idx = jnp.concatenate([idx, idx_c], axis=1)
            gate = jnp.concatenate([gate, gate_c], axis=1)
        y = _moe(idx, gate, h2_all, w_gate, w_up, w_down, l)
        fg = final_g[None]
        xl_new = _combine(idx, y, xn_all, mod, fg, n_sets=B, n=N, cap=cap_l, set_col0=0, row0=0, n_batch=B,
                          latent=True, final=last)
        if last:
            out = xl_new.reshape(B, N, D)
        else:
            xc_new = _combine(idx, y, xn_all, mod, fg, n_sets=B, n=Cx, cap=cap_c, set_col0=B * cap_l, row0=B * N,
                              n_batch=B, latent=False, final=False)
            x_all = jnp.concatenate([xl_new, xc_new], axis=0)
    return out
```

```python
import functools

import numpy as np
import jax
import jax.numpy as jnp
from jax import lax
from jax.experimental import pallas as pl
from jax.experimental.pallas import tpu as pltpu

F32 = jnp.float32
BF16 = jnp.bfloat16
I32 = jnp.int32

GRID_W = 64
MLA_HEADS = 8
MLA_Q_RANK = 384
MLA_KV_RANK = 256
MLA_NOPE = 64
MLA_ROPE = 32
MLA_V = 64
MLA_W = MLA_HEADS * MLA_V
NA_HEADS = 4
NA_HEAD_DIM = 64
NA_W = NA_HEADS * NA_HEAD_DIM
NA_WIN_R = 8
NA_WIN_C = 16
POOL_WINDOWS = (2, 4, 8, 16)
POOL_GROUP = 64
POOL_WIDTH = POOL_GROUP * len(POOL_WINDOWS)
N_EXPERTS = 16
EC_CAPACITY = 2
ROPE_BASE = 10000.0
EPS = 1e-6
IN_SIZES = (MLA_Q_RANK, MLA_KV_RANK, MLA_ROPE, 3 * NA_W, POOL_WIDTH)

LANES = 128
SUBLANES = 8
HEAD_PAD = 128
V_ROWS = MLA_V + 16
QK_EXP2_SCALE = (MLA_NOPE + MLA_ROPE) ** -0.5 * 1.4426950408889634
ROW_TILE = 256
POOL_HALO = 8
GATHER_UNROLL = 8
COMBINE_UNROLL = 8
MOE_ROW_BLOCKS = 4
FLASH_PIECE = 512
MASK_VALUE = -1e30
VMEM_LIMIT = 56 * 1024 * 1024


def _cparams(sem, vmem=VMEM_LIMIT):
    return pltpu.CompilerParams(dimension_semantics=sem, vmem_limit_bytes=vmem)


def _rms(x, g):
    return x * lax.rsqrt(jnp.mean(x * x, axis=-1, keepdims=True) + EPS) * g


def _dot(a, b):
    return jnp.dot(a, b, preferred_element_type=F32)


def _pick_rows(lat_ref, ctx_ref, n_lat_tiles):
    return jnp.where(pl.program_id(0) >= n_lat_tiles, ctx_ref[...], lat_ref[...])


def _split_row_specs(width, n_lat_tiles):
    return [pl.BlockSpec((ROW_TILE, width), lambda i: (jnp.minimum(i, n_lat_tiles - 1), 0)),
            pl.BlockSpec((ROW_TILE, width), lambda i: (jnp.maximum(i - n_lat_tiles, 0), 0))]


def _dot_nt(a, b):
    return lax.dot_general(a, b, (((1,), (1,)), ((), ())), preferred_element_type=F32)


def _ada_kernel(c_ref, w_ref, b_ref, o_ref):
    c = c_ref[...]
    s = c * jax.nn.sigmoid(c)
    o_ref[0] = _dot(s.astype(BF16), w_ref[0].astype(BF16)) + b_ref[0]


def _ada(cc, w_ada, b_ada):
    L, D, D6 = w_ada.shape
    tn = 1536
    return pl.pallas_call(
        _ada_kernel,
        grid=(L, D6 // tn),
        in_specs=[
            pl.BlockSpec((8, D), lambda l, j: (0, 0)),
            pl.BlockSpec((1, D, tn), lambda l, j: (l, 0, j)),
            pl.BlockSpec((1, 1, tn), lambda l, j: (l, 0, j)),
        ],
        out_specs=pl.BlockSpec((1, 8, tn), lambda l, j: (l, 0, j)),
        out_shape=jax.ShapeDtypeStruct((L, 8, D6), F32),
        compiler_params=_cparams(("parallel", "parallel")),
        name="ada",
    )(cc, w_ada, b_ada.reshape(L, 1, D6))


def _inproj_kernel(xl_ref, xc_ref, sh_ref, sc_ref, g1_ref, wb_ref, gq_ref, gkv_ref, wqm_ref, wqs_ref, wk_ref, wv_ref,
                   vone_ref, cos_ref, sin_ref, qt_ref, k_ref, vt_ref, na_ref, pool_ref, *, n_lat_tiles):
    x = _pick_rows(xl_ref, xc_ref, n_lat_tiles)
    h = _rms(x, g1_ref[...]) * (1.0 + sc_ref[0]) + sh_ref[0]
    z = _dot(h.astype(BF16), wb_ref[...])
    o = 0
    cq = z[:, o:o + MLA_Q_RANK]; o += MLA_Q_RANK
    ckv = z[:, o:o + MLA_KV_RANK]; o += MLA_KV_RANK
    kr = z[:, o:o + HEAD_PAD]; o += HEAD_PAD
    krs = z[:, o:o + HEAD_PAD]; o += HEAD_PAD
    na_ref[...] = z[:, o:o + 3 * NA_W].astype(BF16); o += 3 * NA_W
    pool_ref[...] = z[:, o:o + POOL_WIDTH]
    cqn = _rms(cq, gq_ref[...]).astype(BF16)
    ckvn = _rms(ckv, gkv_ref[...]).astype(BF16)
    cos = cos_ref[...]
    sin = sin_ref[...]
    cos8 = jnp.concatenate([cos] * MLA_HEADS, axis=1)
    sin8 = jnp.concatenate([sin] * MLA_HEADS, axis=1)
    q = _dot(cqn, wqm_ref[...]) * cos8 + _dot(cqn, wqs_ref[...]) * sin8
    qt_ref[...] = (q * QK_EXP2_SCALE).T.astype(BF16)
    krot = kr * cos + krs * sin
    k = _dot(ckvn, wk_ref[...]) + jnp.concatenate([krot] * MLA_HEADS, axis=1)
    k_ref[...] = k.astype(BF16)
    vt_ref[...] = (_dot(ckvn, wv_ref[...]) + vone_ref[...]).T.astype(BF16)


def _inproj(x_lat, x_ctx, mod, g1, wts, cos_t, sin_t, *, n_lat_tiles, tiles_per_batch, n_batch):
    D = x_lat.shape[1]
    R = x_lat.shape[0] + x_ctx.shape[0]
    T = ROW_TILE

    def seg(i):
        return jnp.minimum(i // tiles_per_batch, n_batch)

    def tab(i):
        return jnp.where(i < n_lat_tiles, i % tiles_per_batch, tiles_per_batch)

    full = lambda a: pl.BlockSpec(a.shape, lambda i: (0,) * a.ndim)
    outs = [
        jax.ShapeDtypeStruct((MLA_HEADS * HEAD_PAD, R), BF16),
        jax.ShapeDtypeStruct((R, MLA_HEADS * HEAD_PAD), BF16),
        jax.ShapeDtypeStruct((MLA_HEADS * V_ROWS, R), BF16),
        jax.ShapeDtypeStruct((R, 3 * NA_W), BF16),
        jax.ShapeDtypeStruct((R, POOL_WIDTH), F32),
    ]
    row_major = lambda s: pl.BlockSpec((T, s.shape[1]), lambda i: (i, 0))
    out_specs = [pl.BlockSpec((MLA_HEADS * HEAD_PAD, T), lambda i: (0, i)), row_major(outs[1]),
                 pl.BlockSpec((MLA_HEADS * V_ROWS, T), lambda i: (0, i)),
                 row_major(outs[3]), row_major(outs[4])]
    return pl.pallas_call(
        functools.partial(_inproj_kernel, n_lat_tiles=n_lat_tiles),
        grid=(R // T,),
        in_specs=_split_row_specs(D, n_lat_tiles) + [
            pl.BlockSpec((1, 1, D), lambda i: (seg(i), 0, 0)),
            pl.BlockSpec((1, 1, D), lambda i: (seg(i), 0, 1)),
            full(g1), full(wts["wb"]), full(wts["gq"]), full(wts["gkv"]), full(wts["wqm"]), full(wts["wqs"]),
            full(wts["wk"]), full(wts["wv"]), full(wts["vone"]),
            pl.BlockSpec((T, HEAD_PAD), lambda i: (tab(i), 0)),
            pl.BlockSpec((T, HEAD_PAD), lambda i: (tab(i), 0)),
        ],
        out_specs=out_specs,
        out_shape=outs,
        compiler_params=_cparams(("parallel",)),
        name="inproj",
    )(x_lat, x_ctx, mod, mod, g1, wts["wb"], wts["gq"], wts["gkv"], wts["wqm"], wts["wqs"], wts["wk"], wts["wv"],
      wts["vone"], cos_t, sin_t)


def _flash_kernel(*refs, tk, n_chunks):
    if n_chunks:
        q_ref, kc_ref, vtc_ref, kl_ref, vtl_ref, o_ref, m_ref, acc_ref, s_ref = refs
    else:
        q_ref, kc_ref, vtc_ref, o_ref, m_ref, acc_ref = refs
    tq = q_ref.shape[1]
    pw = min(FLASH_PIECE, tq)
    pieces = [(h, slice(pc * pw, (pc + 1) * pw)) for h in range(2) for pc in range(tq // pw)]
    m_ref[...] = jnp.full(m_ref.shape, -jnp.inf, F32)
    acc_ref[...] = jnp.zeros(acc_ref.shape, F32)

    def scores(h, k2, lanes):
        return _dot(k2[:, h * HEAD_PAD:(h + 1) * HEAD_PAD], q_ref[h * HEAD_PAD:(h + 1) * HEAD_PAD, lanes])

    def softmax_pv(h, s, vt, lanes):
        m_old = m_ref[h, :, lanes]
        m_new = jnp.maximum(m_old, jnp.max(s, axis=0, keepdims=True))
        p = jnp.exp2(s - m_new)
        acc_ref[h, :, lanes] = (jnp.exp2(m_old - m_new) * acc_ref[h, :, lanes]
                                + _dot(vt[h * V_ROWS:(h + 1) * V_ROWS, :], p.astype(BF16)))
        m_ref[h, :, lanes] = m_new

    kc = kc_ref[...]
    vtc = vtc_ref[...]
    if not n_chunks:
        for h, lanes in pieces:
            softmax_pv(h, scores(h, kc, lanes), vtc, lanes)
    else:
        def kchunk(i):
            return kl_ref[pl.ds(pl.multiple_of(i * tk, tk), tk), :]

        def vchunk(i):
            return vtl_ref[:, pl.ds(pl.multiple_of(i * tk, tk), tk)]

        k0 = kchunk(0)
        for h, lanes in pieces:
            s_ref[0, h, :, lanes] = scores(h, k0, lanes)
            softmax_pv(h, scores(h, kc, lanes), vtc, lanes)

        def stage(slot_next, k_next, slot_cur, vt_cur):
            for h, lanes in pieces:
                s_ref[slot_next, h, :, lanes] = scores(h, k_next, lanes)
                softmax_pv(h, s_ref[slot_cur, h, :, lanes], vt_cur, lanes)

        def body(i2, carry):
            c0 = 2 * i2
            stage(1, kchunk(c0 + 1), 0, vchunk(c0))
            stage(0, kchunk(jnp.minimum(c0 + 2, n_chunks - 1)), 1, vchunk(c0 + 1))
            return carry
        lax.fori_loop(0, n_chunks // 2, body, 0)
    a0 = acc_ref[0]
    a1 = acc_ref[1]
    o_t = jnp.concatenate([a0[0:MLA_V] / a0[MLA_V:MLA_V + 1], a1[0:MLA_V] / a1[MLA_V:MLA_V + 1]], axis=0)
    o_ref[...] = o_t.T.astype(BF16)


def _flash(q_all, k_all, vt_all, *, n_batch, n_lat, n_ctx, latent):
    ctx_blk0 = (n_batch * n_lat) // n_ctx
    n_pairs = MLA_HEADS // 2
    if latent:
        tq = min(2048, n_lat)
        tk = min(512, n_lat // 2)
        nq = n_lat // tq
        q_map = lambda b, p, i: (p, b * nq + i)
        n_rows = n_batch * n_lat
        n_chunks = n_lat // tk
        assert n_chunks % 2 == 0
    else:
        tq, nq, tk, n_chunks = n_ctx, 1, n_ctx, 0
        q_map = lambda b, p, i: (p, ctx_blk0 + b)
        n_rows = n_batch * n_ctx
    in_specs = [
        pl.BlockSpec((2 * HEAD_PAD, tq), q_map),
        pl.BlockSpec((n_ctx, 2 * HEAD_PAD), lambda b, p, i: (ctx_blk0 + b, p)),
        pl.BlockSpec((2 * V_ROWS, n_ctx), lambda b, p, i: (p, ctx_blk0 + b)),
    ]
    args = [q_all, k_all, vt_all]
    scratch = [pltpu.VMEM((2, 1, tq), F32), pltpu.VMEM((2, V_ROWS, tq), F32)]
    if latent:
        in_specs += [
            pl.BlockSpec((n_lat, 2 * HEAD_PAD), lambda b, p, i: (b, p)),
            pl.BlockSpec((2 * V_ROWS, n_lat), lambda b, p, i: (p, b)),
        ]
        args += [k_all, vt_all]
        scratch.append(pltpu.VMEM((2, 2, tk, tq), F32))
    return pl.pallas_call(
        functools.partial(_flash_kernel, tk=tk, n_chunks=n_chunks),
        grid=(n_batch, n_pairs, nq),
        in_specs=in_specs,
        out_specs=pl.BlockSpec((tq, 2 * MLA_V), lambda b, p, i: (b * nq + i, p)),
        out_shape=jax.ShapeDtypeStruct((n_rows, MLA_W), BF16),
        scratch_shapes=scratch,
        compiler_params=_cparams(("parallel", "parallel", "arbitrary")),
        name="flash_lat" if latent else "flash_ctx",
    )(*args)


NA_QROWS = 8
NA_KROWS = 16


def _na_softmax_pv(parts):
    m = None
    for s, _ in parts:
        mi = jnp.max(s, axis=1, keepdims=True)
        m = mi if m is None else jnp.maximum(m, mi)
    l = None
    o = None
    for s, v in parts:
        p = jnp.exp(s - m)
        li = jnp.sum(p, axis=1, keepdims=True)
        oi = _dot(p.astype(BF16), v)
        l = li if l is None else l + li
        o = oi if o is None else o + oi
    return o / l


def _na_lat_kernel(q_ref, k_ref, v_ref, kc_ref, vc_ref, pairs_ref, o_ref, bias_ref, *, rows, scale, layouts):
    i = pl.program_id(1)
    nblk = rows // NA_QROWS

    def build(variant):
        lay = layouts[variant]
        for h in range(NA_HEADS):
            for qr in range(NA_QROWS):
                for kp in range(NA_KROWS // 2):
                    bias_ref[h, qr * GRID_W:(qr + 1) * GRID_W, kp * LANES:(kp + 1) * LANES] = pairs_ref[h, int(lay[qr, kp])]

    @pl.when(i == 0)
    def _first():
        build(0)

    if nblk > 2:
        @pl.when(i == 1)
        def _interior():
            build(1)

    @pl.when(i == nblk - 1)
    def _last():
        build(2)

    s0 = jnp.clip(NA_QROWS * i - NA_WIN_R // 2, 0, rows - NA_KROWS) * GRID_W
    s0 = pl.multiple_of(s0, 256)
    nk = NA_KROWS * GRID_W
    lo = lax.broadcasted_iota(I32, (1, LANES), 1) < NA_HEAD_DIM
    outs = []
    for pr in range(NA_HEADS // 2):
        cs = slice(pr * LANES, (pr + 1) * LANES)
        qp = q_ref[:, cs]
        kw = k_ref[pl.ds(s0, nk), cs]
        vw = v_ref[pl.ds(s0, nk), cs]
        kc = kc_ref[:, cs]
        vc = vc_ref[:, cs]
        o_pair = None
        for hh in range(2):
            msk = lo if hh == 0 else jnp.logical_not(lo)
            qh = jnp.where(msk, qp, jnp.zeros_like(qp))
            s_lat = _dot_nt(qh, kw) * scale + bias_ref[2 * pr + hh]
            s_ctx = _dot_nt(qh, kc) * scale
            o = _na_softmax_pv([(s_lat, vw), (s_ctx, vc)])
            o_pair = o if hh == 0 else jnp.where(lo, o_pair, o)
        outs.append(o_pair)
    o_ref[...] = jnp.concatenate(outs, axis=1).astype(BF16)


def _na_lat(na_all, rpb, *, n_batch, n_lat, n_ctx):
    rows = n_lat // GRID_W
    nblk = rows // NA_QROWS
    tq = NA_QROWS * GRID_W
    ctx_blk0 = (n_batch * n_lat) // n_ctx
    scale = NA_HEAD_DIM ** -0.5
    layouts, pair_ids = _na_layouts(rows)
    pairs = _na_pair_tables(rpb, pair_ids)

    return pl.pallas_call(
        functools.partial(_na_lat_kernel, rows=rows, scale=scale, layouts=layouts),
        grid=(n_batch, nblk),
        in_specs=[
            pl.BlockSpec((tq, NA_W), lambda b, i: (b * nblk + i, 0)),
            pl.BlockSpec((n_lat, NA_W), lambda b, i: (b, 1)),
            pl.BlockSpec((n_lat, NA_W), lambda b, i: (b, 2)),
            pl.BlockSpec((n_ctx, NA_W), lambda b, i: (ctx_blk0 + b, 1)),
            pl.BlockSpec((n_ctx, NA_W), lambda b, i: (ctx_blk0 + b, 2)),
            pl.BlockSpec(pairs.shape, lambda b, i: (0, 0, 0, 0)),
        ],
        out_specs=pl.BlockSpec((tq, NA_W), lambda b, i: (b * nblk + i, 0)),
        out_shape=jax.ShapeDtypeStruct((n_batch * n_lat, NA_W), BF16),
        scratch_shapes=[pltpu.VMEM((NA_HEADS, tq, NA_KROWS * GRID_W), F32)],
        compiler_params=_cparams(("arbitrary", "arbitrary")),
        name="na_lat",
    )(na_all, na_all, na_all, na_all, na_all, pairs)


def _na_ctx_kernel(q_ref, k_ref, v_ref, o_ref, *, scale):
    lo = lax.broadcasted_iota(I32, (1, LANES), 1) < NA_HEAD_DIM
    outs = []
    for pr in range(NA_HEADS // 2):
        cs = slice(pr * LANES, (pr + 1) * LANES)
        qp = q_ref[:, cs]
        kp = k_ref[:, cs]
        vp = v_ref[:, cs]
        o_pair = None
        for hh in range(2):
            msk = lo if hh == 0 else jnp.logical_not(lo)
            qh = jnp.where(msk, qp, jnp.zeros_like(qp))
            o = _na_softmax_pv([(_dot_nt(qh, kp) * scale, vp)])
            o_pair = o if hh == 0 else jnp.where(lo, o_pair, o)
        outs.append(o_pair)
    o_ref[...] = jnp.concatenate(outs, axis=1).astype(BF16)


def _na_ctx(na_all, *, n_batch, n_lat, n_ctx):
    ctx_blk0 = (n_batch * n_lat) // n_ctx
    return pl.pallas_call(
        functools.partial(_na_ctx_kernel, scale=NA_HEAD_DIM ** -0.5),
        grid=(n_batch,),
        in_specs=[pl.BlockSpec((n_ctx, NA_W), lambda b, j=j: (ctx_blk0 + b, j)) for j in range(3)],
        out_specs=pl.BlockSpec((n_ctx, NA_W), lambda b: (b, 0)),
        out_shape=jax.ShapeDtypeStruct((n_batch * n_ctx, NA_W), BF16),
        compiler_params=_cparams(("parallel",)),
        name="na_ctx",
    )(na_all, na_all, na_all)


NA_MASKED = 2 * NA_WIN_R - 1


def _na_layouts(rows):
    nblk = rows // NA_QROWS
    pair_ids = {}
    layouts = []
    for i in (0, min(1, nblk - 1), nblk - 1):
        s0 = min(max(NA_QROWS * i - NA_WIN_R // 2, 0), rows - NA_KROWS)
        qrow = NA_QROWS * i + np.arange(NA_QROWS)
        krow = s0 + np.arange(NA_KROWS)
        start_r = np.clip(qrow - NA_WIN_R // 2, 0, rows - NA_WIN_R)
        valid_r = (krow[None, :] >= start_r[:, None]) & (krow[None, :] < start_r[:, None] + NA_WIN_R)
        dr = np.where(valid_r, krow[None, :] - qrow[:, None] + (NA_WIN_R - 1), NA_MASKED)
        lay = np.zeros((NA_QROWS, NA_KROWS // 2), np.int64)
        for qr in range(NA_QROWS):
            for kp in range(NA_KROWS // 2):
                key = (int(dr[qr, 2 * kp]), int(dr[qr, 2 * kp + 1]))
                lay[qr, kp] = pair_ids.setdefault(key, len(pair_ids))
        layouts.append(lay)
    return layouts, list(pair_ids)


def _na_pair_tables(rpb, pair_ids):
    qc = np.arange(GRID_W)
    kc = np.arange(GRID_W)
    start_c = np.clip(qc - NA_WIN_C // 2, 0, GRID_W - NA_WIN_C)
    valid_c = (kc[None, :] >= start_c[:, None]) & (kc[None, :] < start_c[:, None] + NA_WIN_C)
    dc = np.clip(kc[None, :] - qc[:, None], -(NA_WIN_C - 1), NA_WIN_C - 1) + (NA_WIN_C - 1)
    onehot = (dc.reshape(1, -1) == np.arange(2 * NA_WIN_C - 1)[:, None]).astype(np.float32)
    by_dr = jnp.einsum("hab,bn->han", rpb.astype(F32), jnp.asarray(onehot), precision=lax.Precision.HIGHEST)
    by_dr = by_dr.reshape(NA_HEADS, NA_MASKED, GRID_W, GRID_W)
    by_dr = jnp.where(jnp.asarray(valid_c)[None, None], by_dr, MASK_VALUE)
    ext = jnp.concatenate([by_dr, jnp.full((NA_HEADS, 1, GRID_W, GRID_W), MASK_VALUE, F32)], axis=1)
    left = np.array([p[0] for p in pair_ids])
    right = np.array([p[1] for p in pair_ids])
    return jnp.concatenate([ext[:, left], ext[:, right]], axis=-1)


def _pool_kernel(prev_ref, cur_ref, next_ref, w_ref, ps_ref, o_ref, scr, *, tiles_per_batch, n_lat_tiles, n_lat, n_ctx):
    i = pl.program_id(0)
    T = ROW_TILE
    H = POOL_HALO
    is_lat = i < n_lat_tiles
    tile_in_seq = jnp.where(is_lat, i % tiles_per_batch, 0)
    n_seq = jnp.where(is_lat, n_lat, n_ctx)
    first = tile_in_seq == 0
    last = jnp.where(is_lat, tile_in_seq == tiles_per_batch - 1, True)
    u = cur_ref[...]
    scr[0:H, :] = jnp.where(first, 0.0, prev_ref[...])
    scr[H:H + T, :] = u
    scr[H + T:H + T + H, :] = jnp.where(last, 0.0, next_ref[...])

    def win(lo, hi):
        acc = None
        for j in range(lo, hi):
            piece = scr[H + j:H + j + T, :]
            acc = piece if acc is None else acc + piece
        return acc

    t = (tile_in_seq * T + lax.broadcasted_iota(I32, (T, 1), 0))
    lane = lax.broadcasted_iota(I32, (1, POOL_WIDTH), 1)
    sums = {}
    acc = win(-1, 1)
    sums[2] = acc
    acc = acc + win(-2, -1) + win(1, 2)
    sums[4] = acc
    acc = acc + win(-4, -2) + win(2, 4)
    sums[8] = acc
    acc = acc + win(-8, -4) + win(4, 8)
    sums[16] = acc
    mean = None
    for gi, w in enumerate(POOL_WINDOWS):
        cnt = (jnp.minimum(t + w // 2, n_seq) - jnp.maximum(t - w // 2, 0)).astype(F32)
        mg = sums[w] / cnt
        mean = mg if mean is None else jnp.where(lane < gi * POOL_GROUP, mean, mg)
    dlt = mean - u
    o_ref[...] = (_dot(dlt.astype(BF16), w_ref[...]) * ps_ref[...]).astype(BF16)


def _pool(u_all, w_bd, pool_scale, *, n_batch, n_lat, n_ctx):
    R = u_all.shape[0]
    T = ROW_TILE
    H = POOL_HALO
    assert n_ctx == T
    tiles_per_batch = n_lat // T
    n_lat_tiles = n_batch * tiles_per_batch
    nb8 = R // H
    return pl.pallas_call(
        functools.partial(_pool_kernel, tiles_per_batch=tiles_per_batch, n_lat_tiles=n_lat_tiles, n_lat=n_lat,
                          n_ctx=n_ctx),
        grid=(R // T,),
        in_specs=[
            pl.BlockSpec((H, POOL_WIDTH), lambda i: (jnp.maximum(i * (T // H) - 1, 0), 0)),
            pl.BlockSpec((T, POOL_WIDTH), lambda i: (i, 0)),
            pl.BlockSpec((H, POOL_WIDTH), lambda i: (jnp.minimum((i + 1) * (T // H), nb8 - 1), 0)),
            pl.BlockSpec((POOL_WIDTH, POOL_WIDTH), lambda i: (0, 0)),
            pl.BlockSpec((1, POOL_WIDTH), lambda i: (0, 0)),
        ],
        out_specs=pl.BlockSpec((T, POOL_WIDTH), lambda i: (i, 0)),
        out_shape=jax.ShapeDtypeStruct((R, POOL_WIDTH), BF16),
        scratch_shapes=[pltpu.VMEM((T + 2 * H, POOL_WIDTH), F32)],
        compiler_params=_cparams(("parallel",)),
        name="pool",
    )(u_all, u_all, u_all, w_bd, pool_scale)


def _merge_kernel(xl_ref, xc_ref, sh1_ref, sc1_ref, gt1_ref, sh2_ref, sc2_ref, g1_ref, g2_ref, al_ref, ac_ref, bl_ref,
                  bc_ref, p_ref, wg_ref, wa_ref, wb_ref, wp_ref, wo_ref, wrt_ref, xn_ref, h2_ref, afft_ref, *,
                  n_lat_tiles):
    D = xl_ref.shape[1]
    x = _pick_rows(xl_ref, xc_ref, n_lat_tiles)
    a = _pick_rows(al_ref, ac_ref, n_lat_tiles)
    b = _pick_rows(bl_ref, bc_ref, n_lat_tiles)
    h = _rms(x, g1_ref[...]) * (1.0 + sc1_ref[0]) + sh1_ref[0]
    g = jax.nn.sigmoid(_dot(h.astype(BF16), wg_ref[...]))
    m = (g[:, 0:D] * _dot(a, wa_ref[...]) + g[:, D:2 * D] * _dot(b, wb_ref[...])
         + g[:, 2 * D:3 * D] * _dot(p_ref[...], wp_ref[...]))
    xn = x + gt1_ref[0] * _dot(m.astype(BF16), wo_ref[...])
    xn_ref[...] = xn
    h2 = _rms(xn, g2_ref[...]) * (1.0 + sc2_ref[0]) + sh2_ref[0]
    h2_ref[...] = h2
    logit_t = _dot_nt(wrt_ref[...], h2.astype(BF16))
    ex_t = jnp.exp(logit_t - jnp.max(logit_t, axis=0, keepdims=True))
    afft_ref[...] = ex_t / jnp.sum(ex_t, axis=0, keepdims=True)


def _merge(x_lat, x_ctx, mod, g1, g2, a_lat, a_ctx, b_lat, b_ctx, p_all, wts, *, n_lat_tiles, tiles_per_batch, n_batch):
    D = x_lat.shape[1]
    R = x_lat.shape[0] + x_ctx.shape[0]
    T = ROW_TILE

    def seg(i):
        return jnp.minimum(i // tiles_per_batch, n_batch)

    modspec = lambda k: pl.BlockSpec((1, 1, D), lambda i: (seg(i), 0, k))
    full = lambda a: pl.BlockSpec(a.shape, lambda i: (0,) * a.ndim)
    row = lambda w: pl.BlockSpec((T, w), lambda i: (i, 0))
    outs = [
        jax.ShapeDtypeStruct((R, D), F32),
        jax.ShapeDtypeStruct((R, D), F32),
        jax.ShapeDtypeStruct((N_EXPERTS, R), F32),
    ]
    wnames = ("wg", "wa", "wbn", "wp", "wo", "wrt")
    return pl.pallas_call(
        functools.partial(_merge_kernel, n_lat_tiles=n_lat_tiles),
        grid=(R // T,),
        in_specs=(_split_row_specs(D, n_lat_tiles)
                  + [modspec(0), modspec(1), modspec(2), modspec(3), modspec(4), full(g1), full(g2)]
                  + _split_row_specs(MLA_W, n_lat_tiles) + _split_row_specs(NA_W, n_lat_tiles) + [row(POOL_WIDTH)]
                  + [full(wts[n]) for n in wnames]),
        out_specs=[row(D), row(D), pl.BlockSpec((N_EXPERTS, T), lambda i: (0, i))],
        out_shape=outs,
        compiler_params=_cparams(("parallel",)),
        name="merge",
    )(x_lat, x_ctx, mod, mod, mod, mod, mod, g1, g2, a_lat, a_ctx, b_lat, b_ctx, p_all, *[wts[n] for n in wnames])


def _onehot(mask):
    return jnp.where(mask, 1.0, 0.0).astype(BF16)


def _col_to_row(col):
    m = col.shape[0]
    wide = jnp.broadcast_to(col, (m, LANES))
    if m < LANES:
        wide = jnp.concatenate([wide, jnp.zeros((LANES - m, LANES), col.dtype)], axis=0)
    return wide.T[0:1, 0:m]


def _topk_kernel(aff_ref, idx_ref, gate_ref, wc_ref, affc_ref, cend_ref, off_ref, *, n, cap, row0, jt):
    s = pl.program_id(0)
    nc = n // LANES
    aff = aff_ref[...]
    bits = lax.bitcast_convert_type(aff, I32)

    def search(it, thr):
        cand = thr | jnp.left_shift(jnp.int32(1), 30 - it)
        cnt = jnp.sum((bits >= cand).astype(I32), axis=1, keepdims=True)
        return jnp.where(cnt >= cap, cand, thr)

    thr = lax.fori_loop(0, 31, search, jnp.zeros((N_EXPERTS, 1), I32))
    gt = bits > thr
    eq = bits == thr
    need = (cap - jnp.sum(gt.astype(I32), axis=1, keepdims=True)).astype(F32)
    lane = lax.broadcasted_iota(I32, (1, LANES), 1)
    upper = _onehot(lax.broadcasted_iota(I32, (LANES, LANES), 0) <= lax.broadcasted_iota(I32, (LANES, LANES), 1))

    wc_ref[:, nc:, :] = jnp.zeros((N_EXPERTS, LANES - nc, LANES), F32)
    affc_ref[:, nc:, :] = jnp.zeros((N_EXPERTS, LANES - nc, LANES), F32)
    ties_before = jnp.zeros((N_EXPERTS, 1), F32)
    tot = jnp.zeros((N_EXPERTS, LANES), F32)
    for ch in range(nc):
        sl = slice(ch * LANES, (ch + 1) * LANES)
        tie_rank = _dot(_onehot(eq[:, sl]), upper) + ties_before
        ties_before = tie_rank[:, LANES - 1:LANES]
        sel = gt[:, sl] | (eq[:, sl] & (tie_rank <= need))
        w = _dot(_onehot(sel), upper)
        wc_ref[:, ch, :] = w
        affc_ref[:, ch, :] = aff[:, sl]
        tot = jnp.where(lane == ch, w[:, LANES - 1:LANES], tot)
    cend = _dot(tot.astype(BF16), upper)
    cend_ref[...] = cend
    off_ref[...] = cend - tot
    base = row0 + s * n

    def per_expert(e, carry):
        wc = wc_ref[e].astype(BF16)
        a = affc_ref[e]
        hi = a.astype(BF16)
        rest = a - hi.astype(F32)
        mid = rest.astype(BF16)
        lo = (rest - mid.astype(F32)).astype(BF16)
        cend_e = cend_ref[pl.ds(e, 1), :]
        off_e = off_ref[pl.ds(e, 1), :]
        idx_row = []
        gate_row = []
        for t in range(cap // jt):
            jcol = (t * jt + lax.broadcasted_iota(I32, (jt, 1), 0)).astype(F32)
            cj = jnp.sum((cend_e <= jcol).astype(I32), axis=1, keepdims=True)
            in_chunk = lane == cj
            g = _onehot(in_chunk)
            rank = jcol - jnp.sum(jnp.where(in_chunk, off_e, 0.0), axis=1, keepdims=True)
            pos = jnp.sum((_dot(g, wc) <= rank).astype(I32), axis=1, keepdims=True)
            aff_rows = _dot(g, hi) + _dot(g, mid) + _dot(g, lo)
            gate = jnp.sum(jnp.where(lane == pos, aff_rows, 0.0), axis=1, keepdims=True)
            idx_row.append(_col_to_row(cj * LANES + pos + base))
            gate_row.append(_col_to_row(gate))
        mine = lax.broadcasted_iota(I32, (N_EXPERTS, 1), 0) == e
        idx_all, gate_all = carry
        return (jnp.where(mine, jnp.concatenate(idx_row, axis=1), idx_all),
                jnp.where(mine, jnp.concatenate(gate_row, axis=1), gate_all))

    idx_all, gate_all = lax.fori_loop(0, N_EXPERTS, per_expert, (jnp.zeros((N_EXPERTS, cap), I32),
                                                                 jnp.zeros((N_EXPERTS, cap), F32)))
    idx_ref[0] = idx_all
    gate_ref[0] = gate_all


def _topk(aff_t, *, n_sets, n, cap, row0):
    jt = min(128, cap)
    blk0 = row0 // n
    spec = pl.BlockSpec((1, N_EXPERTS, cap), lambda s: (s, 0, 0))
    idx, gate = pl.pallas_call(
        functools.partial(_topk_kernel, n=n, cap=cap, row0=row0, jt=jt),
        grid=(n_sets,),
        in_specs=[pl.BlockSpec((N_EXPERTS, n), lambda s: (0, blk0 + s))],
        out_specs=[spec, spec],
        out_shape=[jax.ShapeDtypeStruct((n_sets, N_EXPERTS, cap), I32),
                   jax.ShapeDtypeStruct((n_sets, N_EXPERTS, cap), F32)],
        scratch_shapes=[pltpu.VMEM((N_EXPERTS, LANES, LANES), F32), pltpu.VMEM((N_EXPERTS, LANES, LANES), F32),
                        pltpu.VMEM((N_EXPERTS, LANES), F32), pltpu.VMEM((N_EXPERTS, LANES), F32)],
        compiler_params=_cparams(("parallel",)),
        name="topk",
    )(aff_t)
    idx = jnp.transpose(idx, (1, 0, 2)).reshape(N_EXPERTS, n_sets * cap)
    gate = jnp.transpose(gate, (1, 0, 2)).reshape(N_EXPERTS, n_sets * cap)
    return idx, gate


def _moe_kernel(idx_ref, h_hbm, gate_ref, wg_ref, wu_ref, wd_ref, y_ref, xs, xb, yacc, sem, *, rows, nf):
    e = pl.program_id(0)
    f = pl.program_id(1)

    def row_copy(j, t):
        return pltpu.make_async_copy(h_hbm.at[t], xs.at[pl.ds(pl.multiple_of(j * SUBLANES, SUBLANES), SUBLANES), :],
                                     sem)

    def issue_gather(expert):
        base = expert * rows

        def issue(j8, carry):
            for r in range(GATHER_UNROLL):
                j = j8 * GATHER_UNROLL + r
                row_copy(j, idx_ref[base + j]).start()
            return carry
        lax.fori_loop(0, rows // GATHER_UNROLL, issue, 0)

    @pl.when(f == 0)
    def _gather():
        @pl.when(e == 0)
        def _first():
            issue_gather(0)

        def wait(j8, carry):
            for r in range(GATHER_UNROLL):
                row_copy(j8 * GATHER_UNROLL + r, 0).wait()
            return carry
        lax.fori_loop(0, rows // GATHER_UNROLL, wait, 0)
        for s in range(SUBLANES):
            xb[:, s * LANES:(s + 1) * LANES] = xs[pl.ds(s, rows, stride=SUBLANES), :].astype(BF16)

        @pl.when(e + 1 < pl.num_programs(0))
        def _next():
            issue_gather(e + 1)
        yacc[...] = jnp.zeros(yacc.shape, F32)

    wg = wg_ref[0, 0].astype(BF16)
    wu = wu_ref[0, 0].astype(BF16)
    wd = wd_ref[0, 0].astype(BF16)
    rb = rows // MOE_ROW_BLOCKS
    for b in range(MOE_ROW_BLOCKS):
        xv = xb[b * rb:(b + 1) * rb, :]
        a = _dot(xv, wg)
        u = _dot(xv, wu)
        hmid = (a * jax.nn.sigmoid(a)) * u
        yacc[b * rb:(b + 1) * rb, :] += _dot(hmid.astype(BF16), wd)

    @pl.when(f == nf - 1)
    def _scale():
        g = gate_ref[0]
        for s in range(SUBLANES):
            y_ref[0, pl.ds(s, rows, stride=SUBLANES), :] = yacc[:, s * LANES:(s + 1) * LANES] * g


def _moe(idx, gate, h2, w_gate, w_up, w_down, layer):
    E, rows = idx.shape
    _, _, D, F = w_gate.shape
    assert D == SUBLANES * LANES and rows % GATHER_UNROLL == 0 and rows % (16 * MOE_ROW_BLOCKS) == 0
    tf = 256
    nf = F // tf
    grid_spec = pltpu.PrefetchScalarGridSpec(
        num_scalar_prefetch=1,
        grid=(E, nf),
        in_specs=[
            pl.BlockSpec(memory_space=pl.ANY),
            pl.BlockSpec((1, rows, 1), lambda e, f, idx: (e, 0, 0)),
            pl.BlockSpec((1, 1, D, tf), lambda e, f, idx: (layer, e, 0, f)),
            pl.BlockSpec((1, 1, D, tf), lambda e, f, idx: (layer, e, 0, f)),
            pl.BlockSpec((1, 1, tf, D), lambda e, f, idx: (layer, e, f, 0)),
        ],
        out_specs=pl.BlockSpec((1, rows * SUBLANES, LANES), lambda e, f, idx: (e, 0, 0)),
        scratch_shapes=[
            pltpu.VMEM((rows * SUBLANES, LANES), F32),
            pltpu.VMEM((rows, D), BF16),
            pltpu.VMEM((rows, D), F32),
            pltpu.SemaphoreType.DMA,
        ],
    )
    return pl.pallas_call(
        functools.partial(_moe_kernel, rows=rows, nf=nf),
        grid_spec=grid_spec,
        out_shape=jax.ShapeDtypeStruct((E, rows * SUBLANES, LANES), F32),
        compiler_params=_cparams(("arbitrary", "arbitrary")),
        name="moe",
    )(idx.reshape(-1), h2.reshape(h2.shape[0], SUBLANES, LANES), gate[:, :, None], w_gate, w_up, w_down)


def _combine_kernel(idx_ref, y_hbm, x_ref, g2_ref, fg_ref, o_ref, slab, oacc, win_ref, sems, *, tc, cap, set_col0, row0,
                    tiles_per_set, win, chunk, final):
    t_id = pl.program_id(0)
    e = pl.program_id(1)
    n_t = pl.num_programs(0)
    n_e = pl.num_programs(1)
    rows = y_hbm.shape[1] // SUBLANES
    step = t_id * n_e + e
    slot = step % 2

    def window(tile, expert):
        col0 = expert * rows + set_col0 + (tile // tiles_per_set) * cap

        def lower_bound(v):
            def body(_, lh):
                lo, hi = lh
                mid = (lo + hi) // 2
                less = idx_ref[col0 + jnp.minimum(mid, cap - 1)] < v
                active = lo < hi
                return (jnp.where(active & less, mid + 1, lo), jnp.where(active & jnp.logical_not(less), mid, hi))
            return lax.fori_loop(0, cap.bit_length() + 1, body, (jnp.int32(0), jnp.int32(cap)))[0]

        first = col0 - expert * rows
        r0 = first + lower_bound(row0 + tile * tc)
        r1 = first + lower_bound(row0 + (tile + 1) * tc)
        w0 = jnp.minimum(r0, rows - win)
        return r0, r1, w0

    def chunk_copies(expert, r0, r1, w0, sl):
        src0 = pl.multiple_of(w0 * SUBLANES, SUBLANES)
        return [(pltpu.make_async_copy(y_hbm.at[expert, pl.ds(src0 + k * chunk * SUBLANES, chunk * SUBLANES), :],
                                       slab.at[sl, pl.ds(k * chunk * SUBLANES, chunk * SUBLANES), :], sems.at[sl]),
                 (w0 + k * chunk < r1) & (w0 + (k + 1) * chunk > r0)) for k in range(win // chunk)]

    def fetch(tile, expert, sl):
        r0, r1, w0 = window(tile, expert)
        win_ref[sl, 0] = r0
        win_ref[sl, 1] = r1
        win_ref[sl, 2] = w0
        for copy, needed in chunk_copies(expert, r0, r1, w0, sl):
            @pl.when(needed)
            def _start(copy=copy):
                copy.start()

    @pl.when(step == 0)
    def _first():
        fetch(t_id, e, slot)

    r0 = win_ref[slot, 0]
    r1 = win_ref[slot, 1]
    w0 = win_ref[slot, 2]
    for copy, needed in chunk_copies(e, r0, r1, w0, slot):
        @pl.when(needed)
        def _wait(copy=copy):
            copy.wait()

    @pl.when(step + 1 < n_t * n_e)
    def _prefetch():
        wrap = e == n_e - 1
        fetch(jnp.where(wrap, t_id + 1, t_id), jnp.where(wrap, 0, e + 1), 1 - slot)

    @pl.when(e == 0)
    def _init():
        oacc[...] = jnp.zeros(oacc.shape, F32)

    row_base = e * rows
    tile_start = row0 + t_id * tc

    def add_rows(j0, n):
        ts = [pl.multiple_of((idx_ref[row_base + j0 + r] - tile_start) * SUBLANES, SUBLANES) for r in range(n)]
        cur = [oacc[pl.ds(t, SUBLANES), :] for t in ts]
        new = [slab[slot, pl.ds(pl.multiple_of((j0 + r - w0) * SUBLANES, SUBLANES), SUBLANES), :] for r in range(n)]
        for r in range(n):
            oacc[pl.ds(ts[r], SUBLANES), :] = cur[r] + new[r]

    n_groups = (r1 - r0) // COMBINE_UNROLL

    def add_group(g, carry):
        add_rows(r0 + g * COMBINE_UNROLL, COMBINE_UNROLL)
        return carry
    lax.fori_loop(0, n_groups, add_group, 0)

    def add_tail(j, carry):
        add_rows(j, 1)
        return carry
    lax.fori_loop(r0 + n_groups * COMBINE_UNROLL, r1, add_tail, 0)

    @pl.when(e == n_e - 1)
    def _finish():
        for s in range(SUBLANES):
            lanes = slice(s * LANES, (s + 1) * LANES)
            o_ref[:, lanes] = x_ref[:, lanes] + g2_ref[0][:, lanes] * oacc[pl.ds(s, tc, stride=SUBLANES), :]
        if final:
            o_ref[...] = _rms(o_ref[...], fg_ref[...])


def _combine(idx, y, x_all, mod, final_g, *, n_sets, n, cap, set_col0, row0, n_batch, latent, final):
    E, rows8, _ = y.shape
    rows = rows8 // SUBLANES
    D = x_all.shape[1]
    tc = min(2048, n)
    tiles_per_set = n // tc
    chunk = min(256, tc, cap)
    win = min(tc, cap)
    assert rows >= win and win % chunk == 0
    blk0 = row0 // tc
    if latent:
        seg = lambda t: t // tiles_per_set
    else:
        seg = lambda t: n_batch
    grid_spec = pltpu.PrefetchScalarGridSpec(
        num_scalar_prefetch=1,
        grid=(n_sets * tiles_per_set, E),
        in_specs=[
            pl.BlockSpec(memory_space=pl.ANY),
            pl.BlockSpec((tc, D), lambda t, e, idx: (blk0 + t, 0)),
            pl.BlockSpec((1, 1, D), lambda t, e, idx: (seg(t), 0, 5)),
            pl.BlockSpec((1, D), lambda t, e, idx: (0, 0)),
        ],
        out_specs=pl.BlockSpec((tc, D), lambda t, e, idx: (t, 0)),
        scratch_shapes=[pltpu.VMEM((2, win * SUBLANES, LANES), F32), pltpu.VMEM((tc * SUBLANES, LANES), F32),
                        pltpu.SMEM((2, 3), I32), pltpu.SemaphoreType.DMA((2,))],
    )
    return pl.pallas_call(
        functools.partial(_combine_kernel, tc=tc, cap=cap, set_col0=set_col0, row0=row0,
                          tiles_per_set=tiles_per_set, win=win, chunk=chunk, final=final),
        grid_spec=grid_spec,
        out_shape=jax.ShapeDtypeStruct((n_sets * n, D), F32),
        compiler_params=_cparams(("arbitrary", "arbitrary")),
        name="combine_lat" if latent else "combine_ctx",
    )(idx.reshape(-1), y, x_all, mod, final_g)


def _rope_tables(n_lat, n_ctx):
    half = MLA_ROPE // 2
    quarter = half // 2
    inv = ROPE_BASE ** (-jnp.arange(quarter, dtype=F32) / quarter)
    t = jnp.arange(n_lat)
    ang_r = (t // GRID_W).astype(F32)[:, None] * inv
    ang_c = (t % GRID_W).astype(F32)[:, None] * inv
    cos32 = jnp.concatenate([jnp.cos(ang_r), jnp.cos(ang_r), jnp.cos(ang_c), jnp.cos(ang_c)], axis=1)
    sin32 = jnp.concatenate([-jnp.sin(ang_r), jnp.sin(ang_r), -jnp.sin(ang_c), jnp.sin(ang_c)], axis=1)
    pad = HEAD_PAD - MLA_NOPE - MLA_ROPE
    cos_l = jnp.concatenate([jnp.ones((n_lat, MLA_NOPE), F32), cos32, jnp.zeros((n_lat, pad), F32)], axis=1)
    sin_l = jnp.concatenate([jnp.zeros((n_lat, MLA_NOPE), F32), sin32, jnp.zeros((n_lat, pad), F32)], axis=1)
    cos_c = jnp.concatenate([jnp.ones((ROW_TILE, MLA_NOPE + MLA_ROPE), F32), jnp.zeros((ROW_TILE, pad), F32)], axis=1)
    sin_c = jnp.zeros((ROW_TILE, HEAD_PAD), F32)
    return jnp.concatenate([cos_l, cos_c]), jnp.concatenate([sin_l, sin_c])


def _rope_partner():
    half = MLA_ROPE // 2
    quarter = half // 2
    p = jnp.arange(MLA_ROPE)
    return (p // half) * half + (p % half + quarter) % half


def _layer_weights(l, w_in, mla_q_norm, mla_kv_norm, w_uq, w_ukv, w_pool, w_br_mla, w_br_na, w_br_pool, w_out,
                   w_router):
    D = w_in.shape[1]
    offs = [0]
    for s in IN_SIZES:
        offs.append(offs[-1] + s)
    wi = w_in[l]
    w_cq, w_ckv, w_kr, w_na, w_pl = [wi[:, offs[k]:offs[k + 1]] for k in range(5)]
    w_gt = wi[:, offs[5]:]
    partner = _rope_partner()
    zl = jnp.zeros((D, MLA_NOPE), F32)
    zr = jnp.zeros((D, HEAD_PAD - MLA_NOPE - MLA_ROPE), F32)
    w_kr_p = jnp.concatenate([zl, w_kr, zr], axis=1)
    w_krs_p = jnp.concatenate([zl, w_kr[:, partner], zr], axis=1)
    wb = jnp.concatenate([w_cq, w_ckv, w_kr_p, w_krs_p, w_na, w_pl], axis=1).astype(BF16)

    dq = MLA_NOPE + MLA_ROPE
    uq = w_uq[l].reshape(MLA_Q_RANK, MLA_HEADS, dq)
    zq = jnp.zeros((MLA_Q_RANK, MLA_HEADS, HEAD_PAD - dq), F32)
    wqm = jnp.concatenate([uq, zq], axis=2).reshape(MLA_Q_RANK, MLA_HEADS * HEAD_PAD).astype(BF16)
    wqs = jnp.concatenate([jnp.zeros((MLA_Q_RANK, MLA_HEADS, MLA_NOPE), F32), uq[:, :, MLA_NOPE:][:, :, partner], zq],
                          axis=2).reshape(MLA_Q_RANK, MLA_HEADS * HEAD_PAD).astype(BF16)
    ukv = w_ukv[l].reshape(MLA_KV_RANK, MLA_HEADS, MLA_NOPE + MLA_V)
    wk = jnp.concatenate([ukv[:, :, :MLA_NOPE], jnp.zeros((MLA_KV_RANK, MLA_HEADS, HEAD_PAD - MLA_NOPE), F32)],
                         axis=2).reshape(MLA_KV_RANK, MLA_HEADS * HEAD_PAD).astype(BF16)
    wv = jnp.concatenate([ukv[:, :, MLA_NOPE:], jnp.zeros((MLA_KV_RANK, MLA_HEADS, V_ROWS - MLA_V), F32)],
                         axis=2).reshape(MLA_KV_RANK, MLA_HEADS * V_ROWS).astype(BF16)
    vone = jnp.tile(jnp.concatenate([jnp.zeros((MLA_V,), F32), jnp.ones((V_ROWS - MLA_V,), F32)]), MLA_HEADS)[None]

    w_bd = jnp.zeros((POOL_WIDTH, POOL_WIDTH), F32)
    for g in range(len(POOL_WINDOWS)):
        w_bd = w_bd.at[g * POOL_GROUP:(g + 1) * POOL_GROUP, g * POOL_GROUP:(g + 1) * POOL_GROUP].set(w_pool[l, g])
    return dict(
        wb=wb, gq=mla_q_norm[l][None], gkv=mla_kv_norm[l][None], wqm=wqm, wqs=wqs, wk=wk, wv=wv, vone=vone,
        w_bd=w_bd.astype(BF16), wg=w_gt.astype(BF16), wa=w_br_mla[l].astype(BF16), wbn=w_br_na[l].astype(BF16),
        wp=w_br_pool[l].astype(BF16), wo=w_out[l].astype(BF16), wrt=w_router[l].T.astype(BF16))


def kernel(x, c, ctx, c_ctx, norm1_g, norm2_g, w_ada, b_ada, w_in, mla_q_norm, mla_kv_norm, w_uq, w_ukv, na_rpb, w_pool, pool_scale, w_br_mla, w_br_na, w_br_pool, w_out, w_router, w_gate, w_up, w_down, final_g):
    B, N, D = x.shape
    Cx = ctx.shape[1]
    L = w_in.shape[0]
    E = N_EXPERTS
    T = ROW_TILE
    assert Cx == T and N % (NA_KROWS * GRID_W) == 0 and B + 1 <= 8
    tiles_per_batch = N // T
    n_lat_tiles = B * tiles_per_batch
    cap_l = EC_CAPACITY * N // E
    cap_c = EC_CAPACITY * Cx // E

    x_lat = x.reshape(B * N, D)
    x_ctx = ctx.reshape(B * Cx, D)
    cc = jnp.concatenate([c, c_ctx[None], jnp.zeros((8 - B - 1, D), F32)], axis=0)
    mod_all = _ada(cc, w_ada, b_ada).reshape(L, 8, 1, 6 * D)
    cos_t, sin_t = _rope_tables(N, Cx)
    dims = dict(n_batch=B, n_lat=N, n_ctx=Cx)

    out = None
    for l in range(L):
        last = l == L - 1
        wts = _layer_weights(l, w_in, mla_q_norm, mla_kv_norm, w_uq, w_ukv, w_pool, w_br_mla, w_br_na, w_br_pool,
                             w_out, w_router)
        mod = mod_all[l]
        g1 = norm1_g[l][None]
        g2 = norm2_g[l][None]
        q_all, k_all, vt_all, na_all, pool_u = _inproj(x_lat, x_ctx, mod, g1, wts, cos_t, sin_t, n_lat_tiles=n_lat_tiles,
                                                      tiles_per_batch=tiles_per_batch, n_batch=B)
        a_l = _flash(q_all, k_all, vt_all,latent=True, **dims)
        b_l = _na_lat(na_all, na_rpb[l], **dims)
        p_all = _pool(pool_u, wts["w_bd"], pool_scale[l][None], **dims)
        if not last:
            a_c = _flash(q_all, k_all, vt_all,latent=False, **dims)
            b_c = _na_ctx(na_all, **dims)
        else:
            a_c, b_c = a_l, b_l
        xn_all, h2_all, aff_t = _merge(x_lat, x_ctx, mod, g1, g2, a_l, a_c, b_l, b_c, p_all, wts,
                                       n_lat_tiles=n_lat_tiles, tiles_per_batch=tiles_per_batch, n_batch=B)

        idx, gate = _topk(aff_t, n_sets=B, n=N, cap=cap_l, row0=0)
        if not last:
            idx_c, gate_c = _topk(aff_t, n_sets=B, n=Cx, cap=cap_c, row0=B * N)
            idx = jnp.concatenate([idx, idx_c], axis=1)
            gate = jnp.concatenate([gate, gate_c], axis=1)
        y = _moe(idx, gate, h2_all, w_gate, w_up, w_down, l)
        fg = final_g[None]
        xl_new = _combine(idx, y, xn_all, mod, fg, n_sets=B, n=N, cap=cap_l, set_col0=0, row0=0, n_batch=B,
                          latent=True, final=last)
        if last:
            out = xl_new.reshape(B, N, D)
        else:
            x_lat = xl_new
            x_ctx = _combine(idx, y, xn_all, mod, fg, n_sets=B, n=Cx, cap=cap_c, set_col0=B * cap_l, row0=B * N,
                             n_batch=B, latent=False, final=False)
    return out
```

```python
import functools

import numpy as np
import jax
import jax.numpy as jnp
from jax import lax
from jax.experimental import pallas as pl
from jax.experimental.pallas import tpu as pltpu

F32 = jnp.float32
BF16 = jnp.bfloat16
I32 = jnp.int32

GRID_W = 64
MLA_HEADS = 8
MLA_Q_RANK = 384
MLA_KV_RANK = 256
MLA_NOPE = 64
MLA_ROPE = 32
MLA_V = 64
MLA_W = MLA_HEADS * MLA_V
NA_HEADS = 4
NA_HEAD_DIM = 64
NA_W = NA_HEADS * NA_HEAD_DIM
NA_WIN_R = 8
NA_WIN_C = 16
POOL_WINDOWS = (2, 4, 8, 16)
POOL_GROUP = 64
POOL_WIDTH = POOL_GROUP * len(POOL_WINDOWS)
N_EXPERTS = 16
EC_CAPACITY = 2
ROPE_BASE = 10000.0
EPS = 1e-6
IN_SIZES = (MLA_Q_RANK, MLA_KV_RANK, MLA_ROPE, 3 * NA_W, POOL_WIDTH)

LANES = 128
SUBLANES = 8
HEAD_PAD = 128
V_ROWS = MLA_V + 16
QK_EXP2_SCALE = (MLA_NOPE + MLA_ROPE) ** -0.5 * 1.4426950408889634
ROW_TILE = 256
POOL_HALO = 8
GATHER_UNROLL = 8
COMBINE_UNROLL = 8
MOE_ROW_BLOCKS = 4
FLASH_PIECE = 512
FLASH_Q_TILE = 2048
FLASH_K_CHUNK = 512
ADA_COL_TILE = 1536
MOE_FF_TILE = 256
COMBINE_TILE = 2048
COMBINE_CHUNK = 256
MASK_VALUE = -1e30
VMEM_LIMIT = 56 * 1024 * 1024


def _cparams(sem, vmem=VMEM_LIMIT):
    return pltpu.CompilerParams(dimension_semantics=sem, vmem_limit_bytes=vmem)


def _rms(x, g):
    return x * lax.rsqrt(jnp.mean(x * x, axis=-1, keepdims=True) + EPS) * g


def _dot(a, b):
    return jnp.dot(a, b, preferred_element_type=F32)


def _pick_rows(lat_ref, ctx_ref, n_lat_tiles):
    return jnp.where(pl.program_id(0) >= n_lat_tiles, ctx_ref[...], lat_ref[...])


def _split_row_specs(width, n_lat_tiles):
    return [pl.BlockSpec((ROW_TILE, width), lambda i: (jnp.minimum(i, n_lat_tiles - 1), 0)),
            pl.BlockSpec((ROW_TILE, width), lambda i: (jnp.maximum(i - n_lat_tiles, 0), 0))]


def _dot_nt(a, b):
    return lax.dot_general(a, b, (((1,), (1,)), ((), ())), preferred_element_type=F32)


def _ada_kernel(c_ref, w_ref, b_ref, o_ref):
    c = c_ref[...]
    s = c * jax.nn.sigmoid(c)
    o_ref[0] = _dot(s.astype(BF16), w_ref[0].astype(BF16)) + b_ref[0]


def _ada(cc, w_ada, b_ada):
    L, D, D6 = w_ada.shape
    tn = ADA_COL_TILE
    return pl.pallas_call(
        _ada_kernel,
        grid=(L, D6 // tn),
        in_specs=[
            pl.BlockSpec((8, D), lambda l, j: (0, 0)),
            pl.BlockSpec((1, D, tn), lambda l, j: (l, 0, j)),
            pl.BlockSpec((1, 1, tn), lambda l, j: (l, 0, j)),
        ],
        out_specs=pl.BlockSpec((1, 8, tn), lambda l, j: (l, 0, j)),
        out_shape=jax.ShapeDtypeStruct((L, 8, D6), F32),
        compiler_params=_cparams(("parallel", "parallel")),
        name="ada",
    )(cc, w_ada, b_ada.reshape(L, 1, D6))


def _inproj_kernel(xl_ref, xc_ref, sh_ref, sc_ref, g1_ref, wb_ref, gq_ref, gkv_ref, wqm_ref, wqs_ref, wk_ref, wv_ref,
                   vone_ref, cos_ref, sin_ref, qt_ref, k_ref, vt_ref, na_ref, pool_ref, *, n_lat_tiles):
    x = _pick_rows(xl_ref, xc_ref, n_lat_tiles)
    h = _rms(x, g1_ref[...]) * (1.0 + sc_ref[0]) + sh_ref[0]
    z = _dot(h.astype(BF16), wb_ref[...])
    o = 0
    cq = z[:, o:o + MLA_Q_RANK]; o += MLA_Q_RANK
    ckv = z[:, o:o + MLA_KV_RANK]; o += MLA_KV_RANK
    kr = z[:, o:o + HEAD_PAD]; o += HEAD_PAD
    krs = z[:, o:o + HEAD_PAD]; o += HEAD_PAD
    na_ref[...] = z[:, o:o + 3 * NA_W].astype(BF16); o += 3 * NA_W
    pool_ref[...] = z[:, o:o + POOL_WIDTH]
    cqn = _rms(cq, gq_ref[...]).astype(BF16)
    ckvn = _rms(ckv, gkv_ref[...]).astype(BF16)
    cos = cos_ref[...]
    sin = sin_ref[...]
    cos8 = jnp.concatenate([cos] * MLA_HEADS, axis=1)
    sin8 = jnp.concatenate([sin] * MLA_HEADS, axis=1)
    q = _dot(cqn, wqm_ref[...]) * cos8 + _dot(cqn, wqs_ref[...]) * sin8
    qt_ref[...] = (q * QK_EXP2_SCALE).T.astype(BF16)
    krot = kr * cos + krs * sin
    k = _dot(ckvn, wk_ref[...]) + jnp.concatenate([krot] * MLA_HEADS, axis=1)
    k_ref[...] = k.astype(BF16)
    vt_ref[...] = (_dot(ckvn, wv_ref[...]) + vone_ref[...]).T.astype(BF16)


def _inproj(x_lat, x_ctx, mod, g1, wts, cos_t, sin_t, *, n_lat_tiles, tiles_per_batch, n_batch):
    D = x_lat.shape[1]
    R = x_lat.shape[0] + x_ctx.shape[0]
    T = ROW_TILE

    def seg(i):
        return jnp.minimum(i // tiles_per_batch, n_batch)

    def tab(i):
        return jnp.where(i < n_lat_tiles, i % tiles_per_batch, tiles_per_batch)

    full = lambda a: pl.BlockSpec(a.shape, lambda i: (0,) * a.ndim)
    outs = [
        jax.ShapeDtypeStruct((MLA_HEADS * HEAD_PAD, R), BF16),
        jax.ShapeDtypeStruct((R, MLA_HEADS * HEAD_PAD), BF16),
        jax.ShapeDtypeStruct((MLA_HEADS * V_ROWS, R), BF16),
        jax.ShapeDtypeStruct((R, 3 * NA_W), BF16),
        jax.ShapeDtypeStruct((R, POOL_WIDTH), F32),
    ]
    row_major = lambda s: pl.BlockSpec((T, s.shape[1]), lambda i: (i, 0))
    out_specs = [pl.BlockSpec((MLA_HEADS * HEAD_PAD, T), lambda i: (0, i)), row_major(outs[1]),
                 pl.BlockSpec((MLA_HEADS * V_ROWS, T), lambda i: (0, i)),
                 row_major(outs[3]), row_major(outs[4])]
    return pl.pallas_call(
        functools.partial(_inproj_kernel, n_lat_tiles=n_lat_tiles),
        grid=(R // T,),
        in_specs=_split_row_specs(D, n_lat_tiles) + [
            pl.BlockSpec((1, 1, D), lambda i: (seg(i), 0, 0)),
            pl.BlockSpec((1, 1, D), lambda i: (seg(i), 0, 1)),
            full(g1), full(wts["wb"]), full(wts["gq"]), full(wts["gkv"]), full(wts["wqm"]), full(wts["wqs"]),
            full(wts["wk"]), full(wts["wv"]), full(wts["vone"]),
            pl.BlockSpec((T, HEAD_PAD), lambda i: (tab(i), 0)),
            pl.BlockSpec((T, HEAD_PAD), lambda i: (tab(i), 0)),
        ],
        out_specs=out_specs,
        out_shape=outs,
        compiler_params=_cparams(("parallel",)),
        name="inproj",
    )(x_lat, x_ctx, mod, mod, g1, wts["wb"], wts["gq"], wts["gkv"], wts["wqm"], wts["wqs"], wts["wk"], wts["wv"],
      wts["vone"], cos_t, sin_t)


def _flash_kernel(*refs, tk, n_chunks):
    if n_chunks:
        q_ref, kc_ref, vtc_ref, kl_ref, vtl_ref, o_ref, m_ref, acc_ref, s_ref = refs
    else:
        q_ref, kc_ref, vtc_ref, o_ref, m_ref, acc_ref = refs
    tq = q_ref.shape[1]
    pw = min(FLASH_PIECE, tq)
    pieces = [(h, slice(pc * pw, (pc + 1) * pw)) for h in range(2) for pc in range(tq // pw)]
    m_ref[...] = jnp.full(m_ref.shape, -jnp.inf, F32)
    acc_ref[...] = jnp.zeros(acc_ref.shape, F32)

    def scores(h, k2, lanes):
        return _dot(k2[:, h * HEAD_PAD:(h + 1) * HEAD_PAD], q_ref[h * HEAD_PAD:(h + 1) * HEAD_PAD, lanes])

    def softmax_pv(h, s, vt, lanes):
        m_old = m_ref[h, :, lanes]
        m_new = jnp.maximum(m_old, jnp.max(s, axis=0, keepdims=True))
        p = jnp.exp2(s - m_new)
        acc_ref[h, :, lanes] = (jnp.exp2(m_old - m_new) * acc_ref[h, :, lanes]
                                + _dot(vt[h * V_ROWS:(h + 1) * V_ROWS, :], p.astype(BF16)))
        m_ref[h, :, lanes] = m_new

    kc = kc_ref[...]
    vtc = vtc_ref[...]
    if not n_chunks:
        for h, lanes in pieces:
            softmax_pv(h, scores(h, kc, lanes), vtc, lanes)
    else:
        def kchunk(i):
            return kl_ref[pl.ds(pl.multiple_of(i * tk, tk), tk), :]

        def vchunk(i):
            return vtl_ref[:, pl.ds(pl.multiple_of(i * tk, tk), tk)]

        k0 = kchunk(0)
        for h, lanes in pieces:
            s_ref[0, h, :, lanes] = scores(h, k0, lanes)
            softmax_pv(h, scores(h, kc, lanes), vtc, lanes)

        def stage(slot_next, k_next, slot_cur, vt_cur):
            for h, lanes in pieces:
                s_ref[slot_next, h, :, lanes] = scores(h, k_next, lanes)
                softmax_pv(h, s_ref[slot_cur, h, :, lanes], vt_cur, lanes)

        def body(i2, carry):
            c0 = 2 * i2
            stage(1, kchunk(c0 + 1), 0, vchunk(c0))
            stage(0, kchunk(jnp.minimum(c0 + 2, n_chunks - 1)), 1, vchunk(c0 + 1))
            return carry
        lax.fori_loop(0, n_chunks // 2, body, 0)
    a0 = acc_ref[0]
    a1 = acc_ref[1]
    o_t = jnp.concatenate([a0[0:MLA_V] / a0[MLA_V:MLA_V + 1], a1[0:MLA_V] / a1[MLA_V:MLA_V + 1]], axis=0)
    o_ref[...] = o_t.T.astype(BF16)


def _flash(q_all, k_all, vt_all, *, n_batch, n_lat, n_ctx, latent):
    ctx_blk0 = (n_batch * n_lat) // n_ctx
    n_pairs = MLA_HEADS // 2
    if latent:
        tq = min(FLASH_Q_TILE, n_lat)
        tk = min(FLASH_K_CHUNK, n_lat // 2)
        nq = n_lat // tq
        q_map = lambda b, p, i: (p, b * nq + i)
        n_rows = n_batch * n_lat
        n_chunks = n_lat // tk
        assert n_chunks % 2 == 0
    else:
        tq, nq, tk, n_chunks = n_ctx, 1, n_ctx, 0
        q_map = lambda b, p, i: (p, ctx_blk0 + b)
        n_rows = n_batch * n_ctx
    in_specs = [
        pl.BlockSpec((2 * HEAD_PAD, tq), q_map),
        pl.BlockSpec((n_ctx, 2 * HEAD_PAD), lambda b, p, i: (ctx_blk0 + b, p)),
        pl.BlockSpec((2 * V_ROWS, n_ctx), lambda b, p, i: (p, ctx_blk0 + b)),
    ]
    args = [q_all, k_all, vt_all]
    scratch = [pltpu.VMEM((2, 1, tq), F32), pltpu.VMEM((2, V_ROWS, tq), F32)]
    if latent:
        in_specs += [
            pl.BlockSpec((n_lat, 2 * HEAD_PAD), lambda b, p, i: (b, p)),
            pl.BlockSpec((2 * V_ROWS, n_lat), lambda b, p, i: (p, b)),
        ]
        args += [k_all, vt_all]
        scratch.append(pltpu.VMEM((2, 2, tk, tq), F32))
    return pl.pallas_call(
        functools.partial(_flash_kernel, tk=tk, n_chunks=n_chunks),
        grid=(n_batch, n_pairs, nq),
        in_specs=in_specs,
        out_specs=pl.BlockSpec((tq, 2 * MLA_V), lambda b, p, i: (b * nq + i, p)),
        out_shape=jax.ShapeDtypeStruct((n_rows, MLA_W), BF16),
        scratch_shapes=scratch,
        compiler_params=_cparams(("parallel", "parallel", "arbitrary")),
        name="flash_lat" if latent else "flash_ctx",
    )(*args)


NA_QROWS = 8
NA_KROWS = 16


def _na_softmax_pv(parts):
    m = None
    for s, _ in parts:
        mi = jnp.max(s, axis=1, keepdims=True)
        m = mi if m is None else jnp.maximum(m, mi)
    l = None
    o = None
    for s, v in parts:
        p = jnp.exp(s - m)
        li = jnp.sum(p, axis=1, keepdims=True)
        oi = _dot(p.astype(BF16), v)
        l = li if l is None else l + li
        o = oi if o is None else o + oi
    return o / l


def _na_lat_kernel(q_ref, k_ref, v_ref, kc_ref, vc_ref, pairs_ref, o_ref, bias_ref, *, rows, scale, layouts):
    i = pl.program_id(1)
    nblk = rows // NA_QROWS

    def build(variant):
        lay = layouts[variant]
        for h in range(NA_HEADS):
            for qr in range(NA_QROWS):
                for kp in range(NA_KROWS // 2):
                    bias_ref[h, qr * GRID_W:(qr + 1) * GRID_W, kp * LANES:(kp + 1) * LANES] = pairs_ref[h, int(lay[qr, kp])]

    @pl.when(i == 0)
    def _first():
        build(0)

    if nblk > 2:
        @pl.when(i == 1)
        def _interior():
            build(1)

    @pl.when(i == nblk - 1)
    def _last():
        build(2)

    s0 = jnp.clip(NA_QROWS * i - NA_WIN_R // 2, 0, rows - NA_KROWS) * GRID_W
    s0 = pl.multiple_of(s0, 256)
    nk = NA_KROWS * GRID_W
    lo = lax.broadcasted_iota(I32, (1, LANES), 1) < NA_HEAD_DIM
    outs = []
    for pr in range(NA_HEADS // 2):
        cs = slice(pr * LANES, (pr + 1) * LANES)
        qp = q_ref[:, cs]
        kw = k_ref[pl.ds(s0, nk), cs]
        vw = v_ref[pl.ds(s0, nk), cs]
        kc = kc_ref[:, cs]
        vc = vc_ref[:, cs]
        o_pair = None
        for hh in range(2):
            msk = lo if hh == 0 else jnp.logical_not(lo)
            qh = jnp.where(msk, qp, jnp.zeros_like(qp))
            s_lat = _dot_nt(qh, kw) * scale + bias_ref[2 * pr + hh]
            s_ctx = _dot_nt(qh, kc) * scale
            o = _na_softmax_pv([(s_lat, vw), (s_ctx, vc)])
            o_pair = o if hh == 0 else jnp.where(lo, o_pair, o)
        outs.append(o_pair)
    o_ref[...] = jnp.concatenate(outs, axis=1).astype(BF16)


def _na_lat(na_all, rpb, *, n_batch, n_lat, n_ctx):
    rows = n_lat // GRID_W
    nblk = rows // NA_QROWS
    tq = NA_QROWS * GRID_W
    ctx_blk0 = (n_batch * n_lat) // n_ctx
    scale = NA_HEAD_DIM ** -0.5
    layouts, pair_ids = _na_layouts(rows)
    pairs = _na_pair_tables(rpb, pair_ids)

    return pl.pallas_call(
        functools.partial(_na_lat_kernel, rows=rows, scale=scale, layouts=layouts),
        grid=(n_batch, nblk),
        in_specs=[
            pl.BlockSpec((tq, NA_W), lambda b, i: (b * nblk + i, 0)),
            pl.BlockSpec((n_lat, NA_W), lambda b, i: (b, 1)),
            pl.BlockSpec((n_lat, NA_W), lambda b, i: (b, 2)),
            pl.BlockSpec((n_ctx, NA_W), lambda b, i: (ctx_blk0 + b, 1)),
            pl.BlockSpec((n_ctx, NA_W), lambda b, i: (ctx_blk0 + b, 2)),
            pl.BlockSpec(pairs.shape, lambda b, i: (0, 0, 0, 0)),
        ],
        out_specs=pl.BlockSpec((tq, NA_W), lambda b, i: (b * nblk + i, 0)),
        out_shape=jax.ShapeDtypeStruct((n_batch * n_lat, NA_W), BF16),
        scratch_shapes=[pltpu.VMEM((NA_HEADS, tq, NA_KROWS * GRID_W), F32)],
        compiler_params=_cparams(("arbitrary", "arbitrary")),
        name="na_lat",
    )(na_all, na_all, na_all, na_all, na_all, pairs)


def _na_ctx_kernel(q_ref, k_ref, v_ref, o_ref, *, scale):
    lo = lax.broadcasted_iota(I32, (1, LANES), 1) < NA_HEAD_DIM
    outs = []
    for pr in range(NA_HEADS // 2):
        cs = slice(pr * LANES, (pr + 1) * LANES)
        qp = q_ref[:, cs]
        kp = k_ref[:, cs]
        vp = v_ref[:, cs]
        o_pair = None
        for hh in range(2):
            msk = lo if hh == 0 else jnp.logical_not(lo)
            qh = jnp.where(msk, qp, jnp.zeros_like(qp))
            o = _na_softmax_pv([(_dot_nt(qh, kp) * scale, vp)])
            o_pair = o if hh == 0 else jnp.where(lo, o_pair, o)
        outs.append(o_pair)
    o_ref[...] = jnp.concatenate(outs, axis=1).astype(BF16)


def _na_ctx(na_all, *, n_batch, n_lat, n_ctx):
    ctx_blk0 = (n_batch * n_lat) // n_ctx
    return pl.pallas_call(
        functools.partial(_na_ctx_kernel, scale=NA_HEAD_DIM ** -0.5),
        grid=(n_batch,),
        in_specs=[pl.BlockSpec((n_ctx, NA_W), lambda b, j=j: (ctx_blk0 + b, j)) for j in range(3)],
        out_specs=pl.BlockSpec((n_ctx, NA_W), lambda b: (b, 0)),
        out_shape=jax.ShapeDtypeStruct((n_batch * n_ctx, NA_W), BF16),
        compiler_params=_cparams(("parallel",)),
        name="na_ctx",
    )(na_all, na_all, na_all)


NA_MASKED = 2 * NA_WIN_R - 1


def _na_layouts(rows):
    nblk = rows // NA_QROWS
    pair_ids = {}
    layouts = []
    for i in (0, min(1, nblk - 1), nblk - 1):
        s0 = min(max(NA_QROWS * i - NA_WIN_R // 2, 0), rows - NA_KROWS)
        qrow = NA_QROWS * i + np.arange(NA_QROWS)
        krow = s0 + np.arange(NA_KROWS)
        start_r = np.clip(qrow - NA_WIN_R // 2, 0, rows - NA_WIN_R)
        valid_r = (krow[None, :] >= start_r[:, None]) & (krow[None, :] < start_r[:, None] + NA_WIN_R)
        dr = np.where(valid_r, krow[None, :] - qrow[:, None] + (NA_WIN_R - 1), NA_MASKED)
        lay = np.zeros((NA_QROWS, NA_KROWS // 2), np.int64)
        for qr in range(NA_QROWS):
            for kp in range(NA_KROWS // 2):
                key = (int(dr[qr, 2 * kp]), int(dr[qr, 2 * kp + 1]))
                lay[qr, kp] = pair_ids.setdefault(key, len(pair_ids))
        layouts.append(lay)
    return layouts, list(pair_ids)


def _na_pair_tables(rpb, pair_ids):
    qc = np.arange(GRID_W)
    kc = np.arange(GRID_W)
    start_c = np.clip(qc - NA_WIN_C // 2, 0, GRID_W - NA_WIN_C)
    valid_c = (kc[None, :] >= start_c[:, None]) & (kc[None, :] < start_c[:, None] + NA_WIN_C)
    dc = np.clip(kc[None, :] - qc[:, None], -(NA_WIN_C - 1), NA_WIN_C - 1) + (NA_WIN_C - 1)
    onehot = (dc.reshape(1, -1) == np.arange(2 * NA_WIN_C - 1)[:, None]).astype(np.float32)
    by_dr = jnp.einsum("hab,bn->han", rpb.astype(F32), jnp.asarray(onehot), precision=lax.Precision.HIGHEST)
    by_dr = by_dr.reshape(NA_HEADS, NA_MASKED, GRID_W, GRID_W)
    by_dr = jnp.where(jnp.asarray(valid_c)[None, None], by_dr, MASK_VALUE)
    ext = jnp.concatenate([by_dr, jnp.full((NA_HEADS, 1, GRID_W, GRID_W), MASK_VALUE, F32)], axis=1)
    left = np.array([p[0] for p in pair_ids])
    right = np.array([p[1] for p in pair_ids])
    return jnp.concatenate([ext[:, left], ext[:, right]], axis=-1)


def _pool_kernel(prev_ref, cur_ref, next_ref, w_ref, ps_ref, o_ref, scr, *, tiles_per_batch, n_lat_tiles, n_lat, n_ctx):
    i = pl.program_id(0)
    T = ROW_TILE
    H = POOL_HALO
    is_lat = i < n_lat_tiles
    tile_in_seq = jnp.where(is_lat, i % tiles_per_batch, 0)
    n_seq = jnp.where(is_lat, n_lat, n_ctx)
    first = tile_in_seq == 0
    last = jnp.where(is_lat, tile_in_seq == tiles_per_batch - 1, True)
    u = cur_ref[...]
    scr[0:H, :] = jnp.where(first, 0.0, prev_ref[...])
    scr[H:H + T, :] = u
    scr[H + T:H + T + H, :] = jnp.where(last, 0.0, next_ref[...])

    def win(lo, hi):
        acc = None
        for j in range(lo, hi):
            piece = scr[H + j:H + j + T, :]
            acc = piece if acc is None else acc + piece
        return acc

    t = (tile_in_seq * T + lax.broadcasted_iota(I32, (T, 1), 0))
    lane = lax.broadcasted_iota(I32, (1, POOL_WIDTH), 1)
    sums = {}
    acc = win(-1, 1)
    sums[2] = acc
    acc = acc + win(-2, -1) + win(1, 2)
    sums[4] = acc
    acc = acc + win(-4, -2) + win(2, 4)
    sums[8] = acc
    acc = acc + win(-8, -4) + win(4, 8)
    sums[16] = acc
    mean = None
    for gi, w in enumerate(POOL_WINDOWS):
        cnt = (jnp.minimum(t + w // 2, n_seq) - jnp.maximum(t - w // 2, 0)).astype(F32)
        mg = sums[w] / cnt
        mean = mg if mean is None else jnp.where(lane < gi * POOL_GROUP, mean, mg)
    dlt = mean - u
    o_ref[...] = (_dot(dlt.astype(BF16), w_ref[...]) * ps_ref[...]).astype(BF16)


def _pool(u_all, w_bd, pool_scale, *, n_batch, n_lat, n_ctx):
    R = u_all.shape[0]
    T = ROW_TILE
    H = POOL_HALO
    assert n_ctx == T
    tiles_per_batch = n_lat // T
    n_lat_tiles = n_batch * tiles_per_batch
    nb8 = R // H
    return pl.pallas_call(
        functools.partial(_pool_kernel, tiles_per_batch=tiles_per_batch, n_lat_tiles=n_lat_tiles, n_lat=n_lat,
                          n_ctx=n_ctx),
        grid=(R // T,),
        in_specs=[
            pl.BlockSpec((H, POOL_WIDTH), lambda i: (jnp.maximum(i * (T // H) - 1, 0), 0)),
            pl.BlockSpec((T, POOL_WIDTH), lambda i: (i, 0)),
            pl.BlockSpec((H, POOL_WIDTH), lambda i: (jnp.minimum((i + 1) * (T // H), nb8 - 1), 0)),
            pl.BlockSpec((POOL_WIDTH, POOL_WIDTH), lambda i: (0, 0)),
            pl.BlockSpec((1, POOL_WIDTH), lambda i: (0, 0)),
        ],
        out_specs=pl.BlockSpec((T, POOL_WIDTH), lambda i: (i, 0)),
        out_shape=jax.ShapeDtypeStruct((R, POOL_WIDTH), BF16),
        scratch_shapes=[pltpu.VMEM((T + 2 * H, POOL_WIDTH), F32)],
        compiler_params=_cparams(("parallel",)),
        name="pool",
    )(u_all, u_all, u_all, w_bd, pool_scale)


def _merge_kernel(xl_ref, xc_ref, sh1_ref, sc1_ref, gt1_ref, sh2_ref, sc2_ref, g1_ref, g2_ref, al_ref, ac_ref, bl_ref,
                  bc_ref, p_ref, wg_ref, wa_ref, wb_ref, wp_ref, wo_ref, wrt_ref, xn_ref, h2_ref, afft_ref, *,
                  n_lat_tiles):
    D = xl_ref.shape[1]
    x = _pick_rows(xl_ref, xc_ref, n_lat_tiles)
    a = _pick_rows(al_ref, ac_ref, n_lat_tiles)
    b = _pick_rows(bl_ref, bc_ref, n_lat_tiles)
    h = _rms(x, g1_ref[...]) * (1.0 + sc1_ref[0]) + sh1_ref[0]
    g = jax.nn.sigmoid(_dot(h.astype(BF16), wg_ref[...]))
    m = (g[:, 0:D] * _dot(a, wa_ref[...]) + g[:, D:2 * D] * _dot(b, wb_ref[...])
         + g[:, 2 * D:3 * D] * _dot(p_ref[...], wp_ref[...]))
    xn = x + gt1_ref[0] * _dot(m.astype(BF16), wo_ref[...])
    xn_ref[...] = xn
    h2 = _rms(xn, g2_ref[...]) * (1.0 + sc2_ref[0]) + sh2_ref[0]
    h2_ref[...] = h2
    logit_t = _dot_nt(wrt_ref[...], h2.astype(BF16))
    ex_t = jnp.exp(logit_t - jnp.max(logit_t, axis=0, keepdims=True))
    afft_ref[...] = ex_t / jnp.sum(ex_t, axis=0, keepdims=True)


def _merge(x_lat, x_ctx, mod, g1, g2, a_lat, a_ctx, b_lat, b_ctx, p_all, wts, *, n_lat_tiles, tiles_per_batch, n_batch):
    D = x_lat.shape[1]
    R = x_lat.shape[0] + x_ctx.shape[0]
    T = ROW_TILE

    def seg(i):
        return jnp.minimum(i // tiles_per_batch, n_batch)

    modspec = lambda k: pl.BlockSpec((1, 1, D), lambda i: (seg(i), 0, k))
    full = lambda a: pl.BlockSpec(a.shape, lambda i: (0,) * a.ndim)
    row = lambda w: pl.BlockSpec((T, w), lambda i: (i, 0))
    outs = [
        jax.ShapeDtypeStruct((R, D), F32),
        jax.ShapeDtypeStruct((R, D), F32),
        jax.ShapeDtypeStruct((N_EXPERTS, R), F32),
    ]
    wnames = ("wg", "wa", "wbn", "wp", "wo", "wrt")
    return pl.pallas_call(
        functools.partial(_merge_kernel, n_lat_tiles=n_lat_tiles),
        grid=(R // T,),
        in_specs=(_split_row_specs(D, n_lat_tiles)
                  + [modspec(0), modspec(1), modspec(2), modspec(3), modspec(4), full(g1), full(g2)]
                  + _split_row_specs(MLA_W, n_lat_tiles) + _split_row_specs(NA_W, n_lat_tiles) + [row(POOL_WIDTH)]
                  + [full(wts[n]) for n in wnames]),
        out_specs=[row(D), row(D), pl.BlockSpec((N_EXPERTS, T), lambda i: (0, i))],
        out_shape=outs,
        compiler_params=_cparams(("parallel",)),
        name="merge",
    )(x_lat, x_ctx, mod, mod, mod, mod, mod, g1, g2, a_lat, a_ctx, b_lat, b_ctx, p_all, *[wts[n] for n in wnames])


def _onehot(mask):
    return jnp.where(mask, 1.0, 0.0).astype(BF16)


def _pad_rows(col):
    m = col.shape[0]
    if m == LANES:
        return col
    return jnp.concatenate([col, jnp.zeros((LANES - m, 1), col.dtype)], axis=0)


def _topk_kernel(aff_ref, idx_ref, gate_ref, wc_ref, affc_ref, cend_ref, off_ref, *, n, cap, row0, jt):
    s = pl.program_id(0)
    nc = n // LANES
    aff = aff_ref[...]
    bits = lax.bitcast_convert_type(aff, I32)

    def search(it, thr):
        cand = thr | jnp.left_shift(jnp.int32(1), 30 - it)
        cnt = jnp.sum((bits >= cand).astype(I32), axis=1, keepdims=True)
        return jnp.where(cnt >= cap, cand, thr)

    thr = lax.fori_loop(0, 31, search, jnp.zeros((N_EXPERTS, 1), I32))
    gt = bits > thr
    eq = bits == thr
    need = (cap - jnp.sum(gt.astype(I32), axis=1, keepdims=True)).astype(F32)
    lane = lax.broadcasted_iota(I32, (1, LANES), 1)
    upper = _onehot(lax.broadcasted_iota(I32, (LANES, LANES), 0) <= lax.broadcasted_iota(I32, (LANES, LANES), 1))

    wc_ref[:, nc:, :] = jnp.zeros((N_EXPERTS, LANES - nc, LANES), F32)
    affc_ref[:, nc:, :] = jnp.zeros((N_EXPERTS, LANES - nc, LANES), F32)
    ties_before = jnp.zeros((N_EXPERTS, 1), F32)
    tot = jnp.zeros((N_EXPERTS, LANES), F32)
    for ch in range(nc):
        sl = slice(ch * LANES, (ch + 1) * LANES)
        tie_rank = _dot(_onehot(eq[:, sl]), upper) + ties_before
        ties_before = tie_rank[:, LANES - 1:LANES]
        sel = gt[:, sl] | (eq[:, sl] & (tie_rank <= need))
        w = _dot(_onehot(sel), upper)
        wc_ref[:, ch, :] = w
        affc_ref[:, ch, :] = aff[:, sl]
        tot = jnp.where(lane == ch, w[:, LANES - 1:LANES], tot)
    cend = _dot(tot.astype(BF16), upper)
    cend_ref[...] = cend
    off_ref[...] = cend - tot
    base = row0 + s * n

    def per_expert(e, carry):
        wc = wc_ref[e].astype(BF16)
        a = affc_ref[e]
        hi = a.astype(BF16)
        rest = a - hi.astype(F32)
        mid = rest.astype(BF16)
        lo = (rest - mid.astype(F32)).astype(BF16)
        cend_e = cend_ref[pl.ds(e, 1), :]
        off_e = off_ref[pl.ds(e, 1), :]
        idx_cols = jnp.zeros((LANES, LANES), I32)
        gate_cols = jnp.zeros((LANES, LANES), F32)
        for t in range(cap // jt):
            jcol = (t * jt + lax.broadcasted_iota(I32, (jt, 1), 0)).astype(F32)
            cj = jnp.sum((cend_e <= jcol).astype(I32), axis=1, keepdims=True)
            in_chunk = lane == cj
            g = _onehot(in_chunk)
            rank = jcol - jnp.sum(jnp.where(in_chunk, off_e, 0.0), axis=1, keepdims=True)
            pos = jnp.sum((_dot(g, wc) <= rank).astype(I32), axis=1, keepdims=True)
            aff_rows = _dot(g, hi) + _dot(g, mid) + _dot(g, lo)
            gate = jnp.sum(jnp.where(lane == pos, aff_rows, 0.0), axis=1, keepdims=True)
            idx_cols = jnp.where(lane == t, _pad_rows(cj * LANES + pos + base), idx_cols)
            gate_cols = jnp.where(lane == t, _pad_rows(gate), gate_cols)
        out_rows = pl.ds(pl.multiple_of(e * SUBLANES, SUBLANES), SUBLANES)
        idx_ref[0, out_rows, :] = idx_cols.T[0:SUBLANES, 0:jt]
        gate_ref[0, out_rows, :] = gate_cols.T[0:SUBLANES, 0:jt]
        return carry

    lax.fori_loop(0, N_EXPERTS, per_expert, 0)


def _topk(aff_t, *, n_sets, n, cap, row0):
    jt = min(LANES, cap)
    njt = cap // jt
    assert njt <= SUBLANES and n <= LANES * LANES
    blk0 = row0 // n
    spec = pl.BlockSpec((1, N_EXPERTS * SUBLANES, jt), lambda s: (s, 0, 0))
    idx, gate = pl.pallas_call(
        functools.partial(_topk_kernel, n=n, cap=cap, row0=row0, jt=jt),
        grid=(n_sets,),
        in_specs=[pl.BlockSpec((N_EXPERTS, n), lambda s: (0, blk0 + s))],
        out_specs=[spec, spec],
        out_shape=[jax.ShapeDtypeStruct((n_sets, N_EXPERTS * SUBLANES, jt), I32),
                   jax.ShapeDtypeStruct((n_sets, N_EXPERTS * SUBLANES, jt), F32)],
        scratch_shapes=[pltpu.VMEM((N_EXPERTS, LANES, LANES), F32), pltpu.VMEM((N_EXPERTS, LANES, LANES), F32),
                        pltpu.VMEM((N_EXPERTS, LANES), F32), pltpu.VMEM((N_EXPERTS, LANES), F32)],
        compiler_params=_cparams(("parallel",)),
        name="topk",
    )(aff_t)
    def per_expert_lists(a):
        a = a.reshape(n_sets, N_EXPERTS, SUBLANES, jt)[:, :, :njt]
        return jnp.transpose(a, (1, 0, 2, 3)).reshape(N_EXPERTS, n_sets * cap)
    return per_expert_lists(idx), per_expert_lists(gate)


def _moe_kernel(idx_ref, h_hbm, gate_ref, wg_ref, wu_ref, wd_ref, y_ref, xs, xb, yacc, sem, *, rows, nf):
    e = pl.program_id(0)
    f = pl.program_id(1)

    def row_copy(j, t):
        return pltpu.make_async_copy(h_hbm.at[t], xs.at[pl.ds(pl.multiple_of(j * SUBLANES, SUBLANES), SUBLANES), :],
                                     sem)

    def issue_gather(expert):
        base = expert * rows

        def issue(j8, carry):
            for r in range(GATHER_UNROLL):
                j = j8 * GATHER_UNROLL + r
                row_copy(j, idx_ref[base + j]).start(priority=r % 2)
            return carry
        lax.fori_loop(0, rows // GATHER_UNROLL, issue, 0)

    @pl.when(f == 0)
    def _gather():
        @pl.when(e == 0)
        def _first():
            issue_gather(0)

        def wait(j8, carry):
            for r in range(GATHER_UNROLL):
                row_copy(j8 * GATHER_UNROLL + r, 0).wait()
            return carry
        lax.fori_loop(0, rows // GATHER_UNROLL, wait, 0)
        for s in range(SUBLANES):
            xb[:, s * LANES:(s + 1) * LANES] = xs[pl.ds(s, rows, stride=SUBLANES), :].astype(BF16)

        @pl.when(e + 1 < pl.num_programs(0))
        def _next():
            issue_gather(e + 1)
        yacc[...] = jnp.zeros(yacc.shape, F32)

    wg = wg_ref[0, 0].astype(BF16)
    wu = wu_ref[0, 0].astype(BF16)
    wd = wd_ref[0, 0].astype(BF16)
    rb = rows // MOE_ROW_BLOCKS
    for b in range(MOE_ROW_BLOCKS):
        xv = xb[b * rb:(b + 1) * rb, :]
        a = _dot(xv, wg)
        u = _dot(xv, wu)
        hmid = (a * jax.nn.sigmoid(a)) * u
        yacc[b * rb:(b + 1) * rb, :] += _dot(hmid.astype(BF16), wd)

    @pl.when(f == nf - 1)
    def _scale():
        g = gate_ref[0]
        for s in range(SUBLANES):
            y_ref[0, pl.ds(s, rows, stride=SUBLANES), :] = yacc[:, s * LANES:(s + 1) * LANES] * g


def _moe(idx, gate, h2, w_gate, w_up, w_down, layer):
    E, rows = idx.shape
    _, _, D, F = w_gate.shape
    assert D == SUBLANES * LANES and rows % GATHER_UNROLL == 0 and rows % (16 * MOE_ROW_BLOCKS) == 0
    tf = MOE_FF_TILE
    nf = F // tf
    grid_spec = pltpu.PrefetchScalarGridSpec(
        num_scalar_prefetch=1,
        grid=(E, nf),
        in_specs=[
            pl.BlockSpec(memory_space=pl.ANY),
            pl.BlockSpec((1, rows, 1), lambda e, f, idx: (e, 0, 0)),
            pl.BlockSpec((1, 1, D, tf), lambda e, f, idx: (layer, e, 0, f)),
            pl.BlockSpec((1, 1, D, tf), lambda e, f, idx: (layer, e, 0, f)),
            pl.BlockSpec((1, 1, tf, D), lambda e, f, idx: (layer, e, f, 0)),
        ],
        out_specs=pl.BlockSpec((1, rows * SUBLANES, LANES), lambda e, f, idx: (e, 0, 0)),
        scratch_shapes=[
            pltpu.VMEM((rows * SUBLANES, LANES), F32),
            pltpu.VMEM((rows, D), BF16),
            pltpu.VMEM((rows, D), F32),
            pltpu.SemaphoreType.DMA,
        ],
    )
    return pl.pallas_call(
        functools.partial(_moe_kernel, rows=rows, nf=nf),
        grid_spec=grid_spec,
        out_shape=jax.ShapeDtypeStruct((E, rows * SUBLANES, LANES), F32),
        compiler_params=_cparams(("arbitrary", "arbitrary")),
        name="moe",
    )(idx.reshape(-1), h2.reshape(h2.shape[0], SUBLANES, LANES), gate[:, :, None], w_gate, w_up, w_down)


def _combine_kernel(idx_ref, y_hbm, x_ref, g2_ref, fg_ref, o_ref, slab, oacc, win_ref, sems, *, tc, cap, set_col0, row0,
                    tiles_per_set, win, chunk, final):
    t_id = pl.program_id(0)
    e = pl.program_id(1)
    n_t = pl.num_programs(0)
    n_e = pl.num_programs(1)
    rows = y_hbm.shape[1] // SUBLANES
    step = t_id * n_e + e
    slot = step % 2

    def window(tile, expert):
        col0 = expert * rows + set_col0 + (tile // tiles_per_set) * cap

        def lower_bound(v):
            def body(_, lh):
                lo, hi = lh
                mid = (lo + hi) // 2
                less = idx_ref[col0 + jnp.minimum(mid, cap - 1)] < v
                active = lo < hi
                return (jnp.where(active & less, mid + 1, lo), jnp.where(active & jnp.logical_not(less), mid, hi))
            return lax.fori_loop(0, cap.bit_length() + 1, body, (jnp.int32(0), jnp.int32(cap)))[0]

        first = col0 - expert * rows
        r0 = first + lower_bound(row0 + tile * tc)
        r1 = first + lower_bound(row0 + (tile + 1) * tc)
        w0 = jnp.minimum(r0, rows - win)
        return r0, r1, w0

    def chunk_copies(expert, r0, r1, w0, sl):
        src0 = pl.multiple_of(w0 * SUBLANES, SUBLANES)
        return [(pltpu.make_async_copy(y_hbm.at[expert, pl.ds(src0 + k * chunk * SUBLANES, chunk * SUBLANES), :],
                                       slab.at[sl, pl.ds(k * chunk * SUBLANES, chunk * SUBLANES), :], sems.at[sl]),
                 (w0 + k * chunk < r1) & (w0 + (k + 1) * chunk > r0)) for k in range(win // chunk)]

    def fetch(tile, expert, sl):
        r0, r1, w0 = window(tile, expert)
        win_ref[sl, 0] = r0
        win_ref[sl, 1] = r1
        win_ref[sl, 2] = w0
        for copy, needed in chunk_copies(expert, r0, r1, w0, sl):
            @pl.when(needed)
            def _start(copy=copy):
                copy.start()

    @pl.when(step == 0)
    def _first():
        fetch(t_id, e, slot)

    r0 = win_ref[slot, 0]
    r1 = win_ref[slot, 1]
    w0 = win_ref[slot, 2]
    for copy, needed in chunk_copies(e, r0, r1, w0, slot):
        @pl.when(needed)
        def _wait(copy=copy):
            copy.wait()

    @pl.when(step + 1 < n_t * n_e)
    def _prefetch():
        wrap = e == n_e - 1
        fetch(jnp.where(wrap, t_id + 1, t_id), jnp.where(wrap, 0, e + 1), 1 - slot)

    @pl.when(e == 0)
    def _init():
        oacc[...] = jnp.zeros(oacc.shape, F32)

    row_base = e * rows
    tile_start = row0 + t_id * tc

    def add_rows(j0, n):
        ts = [pl.multiple_of((idx_ref[row_base + j0 + r] - tile_start) * SUBLANES, SUBLANES) for r in range(n)]
        cur = [oacc[pl.ds(t, SUBLANES), :] for t in ts]
        new = [slab[slot, pl.ds(pl.multiple_of((j0 + r - w0) * SUBLANES, SUBLANES), SUBLANES), :] for r in range(n)]
        for r in range(n):
            oacc[pl.ds(ts[r], SUBLANES), :] = cur[r] + new[r]

    n_groups = (r1 - r0) // COMBINE_UNROLL

    def add_group(g, carry):
        add_rows(r0 + g * COMBINE_UNROLL, COMBINE_UNROLL)
        return carry
    lax.fori_loop(0, n_groups, add_group, 0)

    def add_tail(j, carry):
        add_rows(j, 1)
        return carry
    lax.fori_loop(r0 + n_groups * COMBINE_UNROLL, r1, add_tail, 0)

    @pl.when(e == n_e - 1)
    def _finish():
        for s in range(SUBLANES):
            lanes = slice(s * LANES, (s + 1) * LANES)
            o_ref[:, lanes] = x_ref[:, lanes] + g2_ref[0][:, lanes] * oacc[pl.ds(s, tc, stride=SUBLANES), :]
        if final:
            o_ref[...] = _rms(o_ref[...], fg_ref[...])


def _combine(idx, y, x_all, mod, final_g, *, n_sets, n, cap, set_col0, row0, n_batch, latent, final):
    E, rows8, _ = y.shape
    rows = rows8 // SUBLANES
    D = x_all.shape[1]
    tc = min(COMBINE_TILE, n)
    tiles_per_set = n // tc
    chunk = min(COMBINE_CHUNK, tc, cap)
    win = min(tc, cap)
    assert rows >= win and win % chunk == 0
    blk0 = row0 // tc
    if latent:
        seg = lambda t: t // tiles_per_set
    else:
        seg = lambda t: n_batch
    grid_spec = pltpu.PrefetchScalarGridSpec(
        num_scalar_prefetch=1,
        grid=(n_sets * tiles_per_set, E),
        in_specs=[
            pl.BlockSpec(memory_space=pl.ANY),
            pl.BlockSpec((tc, D), lambda t, e, idx: (blk0 + t, 0)),
            pl.BlockSpec((1, 1, D), lambda t, e, idx: (seg(t), 0, 5)),
            pl.BlockSpec((1, D), lambda t, e, idx: (0, 0)),
        ],
        out_specs=pl.BlockSpec((tc, D), lambda t, e, idx: (t, 0)),
        scratch_shapes=[pltpu.VMEM((2, win * SUBLANES, LANES), F32), pltpu.VMEM((tc * SUBLANES, LANES), F32),
                        pltpu.SMEM((2, 3), I32), pltpu.SemaphoreType.DMA((2,))],
    )
    return pl.pallas_call(
        functools.partial(_combine_kernel, tc=tc, cap=cap, set_col0=set_col0, row0=row0,
                          tiles_per_set=tiles_per_set, win=win, chunk=chunk, final=final),
        grid_spec=grid_spec,
        out_shape=jax.ShapeDtypeStruct((n_sets * n, D), F32),
        compiler_params=_cparams(("arbitrary", "arbitrary")),
        name="combine_lat" if latent else "combine_ctx",
    )(idx.reshape(-1), y, x_all, mod, final_g)


def _rope_tables(n_lat, n_ctx):
    half = MLA_ROPE // 2
    quarter = half // 2
    inv = ROPE_BASE ** (-jnp.arange(quarter, dtype=F32) / quarter)
    t = jnp.arange(n_lat)
    ang_r = (t // GRID_W).astype(F32)[:, None] * inv
    ang_c = (t % GRID_W).astype(F32)[:, None] * inv
    cos32 = jnp.concatenate([jnp.cos(ang_r), jnp.cos(ang_r), jnp.cos(ang_c), jnp.cos(ang_c)], axis=1)
    sin32 = jnp.concatenate([-jnp.sin(ang_r), jnp.sin(ang_r), -jnp.sin(ang_c), jnp.sin(ang_c)], axis=1)
    pad = HEAD_PAD - MLA_NOPE - MLA_ROPE
    cos_l = jnp.concatenate([jnp.ones((n_lat, MLA_NOPE), F32), cos32, jnp.zeros((n_lat, pad), F32)], axis=1)
    sin_l = jnp.concatenate([jnp.zeros((n_lat, MLA_NOPE), F32), sin32, jnp.zeros((n_lat, pad), F32)], axis=1)
    cos_c = jnp.concatenate([jnp.ones((ROW_TILE, MLA_NOPE + MLA_ROPE), F32), jnp.zeros((ROW_TILE, pad), F32)], axis=1)
    sin_c = jnp.zeros((ROW_TILE, HEAD_PAD), F32)
    return jnp.concatenate([cos_l, cos_c]), jnp.concatenate([sin_l, sin_c])


def _rope_partner():
    half = MLA_ROPE // 2
    quarter = half // 2
    p = jnp.arange(MLA_ROPE)
    return (p // half) * half + (p % half + quarter) % half


def _layer_weights(l, w_in, mla_q_norm, mla_kv_norm, w_uq, w_ukv, w_pool, w_br_mla, w_br_na, w_br_pool, w_out,
                   w_router):
    D = w_in.shape[1]
    offs = [0]
    for s in IN_SIZES:
        offs.append(offs[-1] + s)
    wi = w_in[l]
    w_cq, w_ckv, w_kr, w_na, w_pl = [wi[:, offs[k]:offs[k + 1]] for k in range(5)]
    w_gt = wi[:, offs[5]:]
    partner = _rope_partner()
    zl = jnp.zeros((D, MLA_NOPE), F32)
    zr = jnp.zeros((D, HEAD_PAD - MLA_NOPE - MLA_ROPE), F32)
    w_kr_p = jnp.concatenate([zl, w_kr, zr], axis=1)
    w_krs_p = jnp.concatenate([zl, w_kr[:, partner], zr], axis=1)
    wb = jnp.concatenate([w_cq, w_ckv, w_kr_p, w_krs_p, w_na, w_pl], axis=1).astype(BF16)

    dq = MLA_NOPE + MLA_ROPE
    uq = w_uq[l].reshape(MLA_Q_RANK, MLA_HEADS, dq)
    zq = jnp.zeros((MLA_Q_RANK, MLA_HEADS, HEAD_PAD - dq), F32)
    wqm = jnp.concatenate([uq, zq], axis=2).reshape(MLA_Q_RANK, MLA_HEADS * HEAD_PAD).astype(BF16)
    wqs = jnp.concatenate([jnp.zeros((MLA_Q_RANK, MLA_HEADS, MLA_NOPE), F32), uq[:, :, MLA_NOPE:][:, :, partner], zq],
                          axis=2).reshape(MLA_Q_RANK, MLA_HEADS * HEAD_PAD).astype(BF16)
    ukv = w_ukv[l].reshape(MLA_KV_RANK, MLA_HEADS, MLA_NOPE + MLA_V)
    wk = jnp.concatenate([ukv[:, :, :MLA_NOPE], jnp.zeros((MLA_KV_RANK, MLA_HEADS, HEAD_PAD - MLA_NOPE), F32)],
                         axis=2).reshape(MLA_KV_RANK, MLA_HEADS * HEAD_PAD).astype(BF16)
    wv = jnp.concatenate([ukv[:, :, MLA_NOPE:], jnp.zeros((MLA_KV_RANK, MLA_HEADS, V_ROWS - MLA_V), F32)],
                         axis=2).reshape(MLA_KV_RANK, MLA_HEADS * V_ROWS).astype(BF16)
    vone = jnp.tile(jnp.concatenate([jnp.zeros((MLA_V,), F32), jnp.ones((V_ROWS - MLA_V,), F32)]), MLA_HEADS)[None]

    w_bd = jnp.zeros((POOL_WIDTH, POOL_WIDTH), F32)
    for g in range(len(POOL_WINDOWS)):
        w_bd = w_bd.at[g * POOL_GROUP:(g + 1) * POOL_GROUP, g * POOL_GROUP:(g + 1) * POOL_GROUP].set(w_pool[l, g])
    return dict(
        wb=wb, gq=mla_q_norm[l][None], gkv=mla_kv_norm[l][None], wqm=wqm, wqs=wqs, wk=wk, wv=wv, vone=vone,
        w_bd=w_bd.astype(BF16), wg=w_gt.astype(BF16), wa=w_br_mla[l].astype(BF16), wbn=w_br_na[l].astype(BF16),
        wp=w_br_pool[l].astype(BF16), wo=w_out[l].astype(BF16), wrt=w_router[l].T.astype(BF16))


def kernel(x, c, ctx, c_ctx, norm1_g, norm2_g, w_ada, b_ada, w_in, mla_q_norm, mla_kv_norm, w_uq, w_ukv, na_rpb, w_pool, pool_scale, w_br_mla, w_br_na, w_br_pool, w_out, w_router, w_gate, w_up, w_down, final_g):
    B, N, D = x.shape
    Cx = ctx.shape[1]
    L = w_in.shape[0]
    E = N_EXPERTS
    T = ROW_TILE
    assert Cx == T and N % (NA_KROWS * GRID_W) == 0 and B + 1 <= 8
    tiles_per_batch = N // T
    n_lat_tiles = B * tiles_per_batch
    cap_l = EC_CAPACITY * N // E
    cap_c = EC_CAPACITY * Cx // E

    x_lat = x.reshape(B * N, D)
    x_ctx = ctx.reshape(B * Cx, D)
    cc = jnp.concatenate([c, c_ctx[None], jnp.zeros((8 - B - 1, D), F32)], axis=0)
    mod_all = _ada(cc, w_ada, b_ada).reshape(L, 8, 1, 6 * D)
    cos_t, sin_t = _rope_tables(N, Cx)
    dims = dict(n_batch=B, n_lat=N, n_ctx=Cx)

    out = None
    for l in range(L):
        last = l == L - 1
        wts = _layer_weights(l, w_in, mla_q_norm, mla_kv_norm, w_uq, w_ukv, w_pool, w_br_mla, w_br_na, w_br_pool,
                             w_out, w_router)
        mod = mod_all[l]
        g1 = norm1_g[l][None]
        g2 = norm2_g[l][None]
        q_all, k_all, vt_all, na_all, pool_u = _inproj(x_lat, x_ctx, mod, g1, wts, cos_t, sin_t, n_lat_tiles=n_lat_tiles,
                                                      tiles_per_batch=tiles_per_batch, n_batch=B)
        a_l = _flash(q_all, k_all, vt_all,latent=True, **dims)
        b_l = _na_lat(na_all, na_rpb[l], **dims)
        p_all = _pool(pool_u, wts["w_bd"], pool_scale[l][None], **dims)
        if not last:
            a_c = _flash(q_all, k_all, vt_all,latent=False, **dims)
            b_c = _na_ctx(na_all, **dims)
        else:
            a_c, b_c = a_l, b_l
        xn_all, h2_all, aff_t = _merge(x_lat, x_ctx, mod, g1, g2, a_l, a_c, b_l, b_c, p_all, wts,
                                       n_lat_tiles=n_lat_tiles, tiles_per_batch=tiles_per_batch, n_batch=B)

        idx, gate = _topk(aff_t, n_sets=B, n=N, cap=cap_l, row0=0)
        if not last:
            idx_c, gate_c = _topk(aff_t, n_sets=B, n=Cx, cap=cap_c, row0=B * N)
            idx = jnp.concatenate([idx, idx_c], axis=1)
            gate = jnp.concatenate([gate, gate_c], axis=1)
        y = _moe(idx, gate, h2_all, w_gate, w_up, w_down, l)
        fg = final_g[None]
        xl_new = _combine(idx, y, xn_all, mod, fg, n_sets=B, n=N, cap=cap_l, set_col0=0, row0=0, n_batch=B,
                          latent=True, final=last)
        if last:
            out = xl_new.reshape(B, N, D)
        else:
            x_lat = xl_new
            x_ctx = _combine(idx, y, xn_all, mod, fg, n_sets=B, n=Cx, cap=cap_c, set_col0=B * cap_l, row0=B * N,
                             n_batch=B, latent=False, final=False)
    return out
```
